```python
import jax, jax.numpy as jnp
from jax import lax
import numpy as np

D_MODEL = 2048
BATCH = 4
SEQ = 2048
DEPTH = 2
DEC_BATCH = 128
DEC_SEQ = 1
PAST_LEN = 16384
PAGE_SIZE = 128

MIX_WIDTH = D_MODEL
W_A = MIX_WIDTH // 2
H_A = 8
HD_A = W_A // H_A
K_A = 4
LRU_C = 8.0
W_B = MIX_WIDTH // 4
K_B = 31
W_C = MIX_WIDTH // 4
H_C = 4
HD_C = W_C // H_C
CHUNK = 128
D_IN = 2 * (W_A + W_B + W_C)
D_FF = 4 * D_MODEL
EPS = 1e-6

kernel_name = "hybrid_rglru_convmod_chunkmlp_decode"


def rms_norm(x, g):
    x32 = x.astype(jnp.float32)
    y = x32 * lax.rsqrt(jnp.mean(x32 * x32, axis=-1, keepdims=True) + EPS)
    return (y * g.astype(jnp.float32)).astype(x.dtype)


def layer_norm(x, g, b):
    x32 = x.astype(jnp.float32)
    xc = x32 - jnp.mean(x32, axis=-1, keepdims=True)
    y = xc * lax.rsqrt(jnp.mean(xc * xc, axis=-1, keepdims=True) + EPS)
    return (y * g.astype(jnp.float32) + b.astype(jnp.float32)).astype(x.dtype)


def causal_dwconv(x_full, w):
    c = x_full.shape[-1]
    return lax.conv_general_dilated(
        x_full, w.astype(x_full.dtype)[:, None, :], window_strides=(1,), padding='VALID',
        dimension_numbers=('NWC', 'WIO', 'NWC'), feature_group_count=c)


def rg_lru(x, h0, w_r, b_r, w_i, b_i, lam):
    bsz, seq_len, _ = x.shape
    xh = x.reshape(bsz, seq_len, H_A, HD_A)
    r = jax.nn.sigmoid(jnp.einsum('blhi,hij->blhj', xh, w_r.astype(x.dtype)).reshape(bsz, seq_len, W_A).astype(jnp.float32) + b_r.astype(jnp.float32))
    i = jax.nn.sigmoid(jnp.einsum('blhi,hij->blhj', xh, w_i.astype(x.dtype)).reshape(bsz, seq_len, W_A).astype(jnp.float32) + b_i.astype(jnp.float32))
    log_a = -LRU_C * r * jax.nn.softplus(-lam.astype(jnp.float32))
    a = jnp.exp(log_a)
    u = jnp.sqrt(-jnp.expm1(2.0 * log_a)) * (i * x.astype(jnp.float32))

    def step(h, au):
        a_t, u_t = au
        h = a_t * h + u_t
        return h, h

    h_last, hs = lax.scan(step, h0.astype(jnp.float32), (jnp.swapaxes(a, 0, 1), jnp.swapaxes(u, 0, 1)))
    return jnp.swapaxes(hs, 0, 1).astype(x.dtype), h_last


def spatial_gate(u, v, w_s, b_s):
    bsz, seq_len, _ = v.shape
    pad = (-seq_len) % CHUNK
    vp = jnp.pad(v, ((0, 0), (0, pad), (0, 0)))
    n_chunks = (seq_len + pad) // CHUNK
    vr = vp.reshape(bsz, n_chunks, CHUNK, H_C, HD_C)
    mask = jnp.tril(jnp.ones((CHUNK, CHUNK), dtype=w_s.dtype))
    w = (w_s * mask[None]).astype(v.dtype)
    mixed = jnp.einsum('hts,bcshd->bcthd', w, vr) + b_s.T.astype(v.dtype)[None, None, :, :, None]
    mixed = mixed.reshape(bsz, n_chunks * CHUNK, W_C)[:, :seq_len]
    return u * mixed


def hybrid_layer(x, conv_a_buf, h0, conv_b_buf, p, l):
    hn = rms_norm(x, p['norm_mix'][l])
    z = jnp.einsum('bld,de->ble', hn, p['w_in'][l].astype(x.dtype))
    xa, ga, xb, gb, zc = jnp.split(z, [W_A, 2 * W_A, 2 * W_A + W_B, 2 * W_A + 2 * W_B], axis=-1)
    xa_full = jnp.concatenate([conv_a_buf.astype(x.dtype), xa], axis=1)
    xa_conv = causal_dwconv(xa_full, p['conv_a_w'][l]) + p['conv_a_b'][l].astype(x.dtype)
    y_a, h_last = rg_lru(xa_conv, h0, p['gate_r_w'][l], p['gate_r_b'][l], p['gate_i_w'][l], p['gate_i_b'][l], p['lru_lambda'][l])
    y_a = y_a * jax.nn.gelu(ga)
    ub = xb * jax.nn.sigmoid(gb)
    ub_full = jnp.concatenate([conv_b_buf.astype(x.dtype), ub], axis=1)
    y_b = causal_dwconv(ub_full, p['conv_b_w'][l])
    y_b = jax.nn.silu(layer_norm(y_b, p['ln_b_g'][l], p['ln_b_b'][l]))
    uc, vc = jnp.split(jax.nn.gelu(zc), 2, axis=-1)
    vn = layer_norm(vc, p['sgu_ln_g'][l], p['sgu_ln_b'][l])
    y_c = spatial_gate(uc, vn, p['sgu_w'][l], p['sgu_b'][l])
    mix = jnp.concatenate([y_a, y_b, y_c], axis=-1)
    x = x + jnp.einsum('blm,md->bld', mix, p['w_out'][l].astype(x.dtype))
    hf = rms_norm(x, p['norm_ffn'][l])
    ff = jnp.square(jax.nn.relu(jnp.einsum('bld,df->blf', hf, p['w_ff1'][l].astype(x.dtype))))
    x = x + jnp.einsum('blf,fd->bld', ff, p['w_ff2'][l].astype(x.dtype))
    return x, xa_full[:, -(K_A - 1):], h_last, ub_full[:, -(K_B - 1):], vn


def trunk(x, conv_a_bufs, h0s, conv_b_bufs, p):
    ca, hh, cb, vv = [], [], [], []
    for l in range(DEPTH):
        x, c_a, h_l, c_b, v_l = hybrid_layer(x, conv_a_bufs[l], h0s[l], conv_b_bufs[l], p, l)
        ca.append(c_a)
        hh.append(h_l)
        cb.append(c_b)
        vv.append(v_l)
    y = rms_norm(x, p['norm_final'])
    return y, jnp.stack(ca), jnp.stack(hh), jnp.stack(cb), jnp.stack(vv)


def setup_inputs(seed: int = 0) -> dict:
    key = jax.random.key(seed)
    ks = jax.random.split(key, 32)
    f32 = jnp.float32

    def nrm(k, shape, s):
        return jax.random.normal(k, shape, f32) * s

    p_a = jax.random.uniform(ks[10], (DEPTH, W_A), f32, minval=0.9, maxval=0.999)
    a0 = p_a ** (1.0 / LRU_C)
    lru_lambda = jnp.log(a0) - jnp.log1p(-a0)
    return {
        'x_prompt': nrm(ks[0], (BATCH, SEQ, D_MODEL), 1.0),
        'x_sample': nrm(ks[1], (DEC_BATCH, DEC_SEQ, D_MODEL), 1.0),
        'state_conv_a': nrm(ks[2], (DEPTH, DEC_BATCH, K_A - 1, W_A), 1.0),
        'state_lru_h': nrm(ks[3], (DEPTH, DEC_BATCH, W_A), 0.5),
        'state_conv_b': nrm(ks[4], (DEPTH, DEC_BATCH, K_B - 1, W_B), 1.0),
        'norm_mix': 1.0 + nrm(ks[5], (DEPTH, D_MODEL), 0.02),
        'w_in': nrm(ks[6], (DEPTH, D_MODEL, D_IN), D_MODEL ** -0.5),
        'conv_a_w': nrm(ks[7], (DEPTH, K_A, W_A), K_A ** -0.5),
        'conv_a_b': nrm(ks[8], (DEPTH, W_A), 0.01),
        'gate_r_w': nrm(ks[9], (DEPTH, H_A, HD_A, HD_A), HD_A ** -0.5),
        'gate_r_b': nrm(ks[11], (DEPTH, W_A), 0.01),
        'gate_i_w': nrm(ks[12], (DEPTH, H_A, HD_A, HD_A), HD_A ** -0.5),
        'gate_i_b': nrm(ks[13], (DEPTH, W_A), 0.01),
        'lru_lambda': lru_lambda,
        'conv_b_w': nrm(ks[14], (DEPTH, K_B, W_B), K_B ** -0.5),
        'ln_b_g': 1.0 + nrm(ks[15], (DEPTH, W_B), 0.02),
        'ln_b_b': nrm(ks[16], (DEPTH, W_B), 0.01),
        'sgu_ln_g': 1.0 + nrm(ks[17], (DEPTH, W_C), 0.02),
        'sgu_ln_b': nrm(ks[18], (DEPTH, W_C), 0.01),
        'sgu_w': nrm(ks[19], (DEPTH, H_C, CHUNK, CHUNK), CHUNK ** -0.5),
        'sgu_b': 1.0 + nrm(ks[20], (DEPTH, H_C, CHUNK), 0.02),
        'w_out': nrm(ks[21], (DEPTH, MIX_WIDTH, D_MODEL), MIX_WIDTH ** -0.5),
        'norm_ffn': 1.0 + nrm(ks[22], (DEPTH, D_MODEL), 0.02),
        'w_ff1': nrm(ks[23], (DEPTH, D_MODEL, D_FF), D_MODEL ** -0.5),
        'w_ff2': nrm(ks[24], (DEPTH, D_FF, D_MODEL), D_FF ** -0.5),
        'norm_final': 1.0 + nrm(ks[25], (D_MODEL,), 0.02),
    }


def reference(x_prompt, x_sample, state_conv_a, state_lru_h, state_conv_b,
              norm_mix, w_in, conv_a_w, conv_a_b, gate_r_w, gate_r_b, gate_i_w, gate_i_b,
              lru_lambda, conv_b_w, ln_b_g, ln_b_b, sgu_ln_g, sgu_ln_b, sgu_w, sgu_b,
              w_out, norm_ffn, w_ff1, w_ff2, norm_final):
    p = dict(norm_mix=norm_mix, w_in=w_in, conv_a_w=conv_a_w, conv_a_b=conv_a_b,
             gate_r_w=gate_r_w, gate_r_b=gate_r_b, gate_i_w=gate_i_w, gate_i_b=gate_i_b,
             lru_lambda=lru_lambda, conv_b_w=conv_b_w, ln_b_g=ln_b_g, ln_b_b=ln_b_b,
             sgu_ln_g=sgu_ln_g, sgu_ln_b=sgu_ln_b, sgu_w=sgu_w, sgu_b=sgu_b,
             w_out=w_out, norm_ffn=norm_ffn, w_ff1=w_ff1, w_ff2=w_ff2, norm_final=norm_final)
    nb = x_prompt.shape[0]
    zero_ca = jnp.zeros((DEPTH, nb, K_A - 1, W_A), x_prompt.dtype)
    zero_h = jnp.zeros((DEPTH, nb, W_A), jnp.float32)
    zero_cb = jnp.zeros((DEPTH, nb, K_B - 1, W_B), x_prompt.dtype)
    y_prompt, ca_p, h_p, cb_p, _ = trunk(x_prompt, zero_ca, zero_h, zero_cb, p)
    y_sample, ca_s, h_s, cb_s, v_s = trunk(x_sample, state_conv_a, state_lru_h, state_conv_b, p)
    return (y_prompt, y_sample, ca_p, h_p, cb_p, ca_s, h_s, cb_s, v_s)
```

```python
import functools

import jax
import jax.numpy as jnp
from jax import lax
from jax.experimental import pallas as pl
from jax.experimental.pallas import tpu as pltpu

F32 = jnp.float32
BF16 = jnp.bfloat16

D_MODEL = 2048
W_A = 1024
H_A = 8
HD_A = W_A // H_A
K_A = 4
LRU_C = 8.0
W_B = 512
K_B = 31
W_C = 512
H_C = 4
HD_C = W_C // H_C
CHUNK = 128
D_IN = 2 * (W_A + W_B + W_C)
D_FF = 4 * D_MODEL
EPS = 1e-6

OFF_XA, OFF_GA, OFF_XB, OFF_GB, OFF_ZC = 0, W_A, 2 * W_A, 2 * W_A + W_B, 2 * W_A + 2 * W_B
OFF_YA, OFF_YB, OFF_YC = 0, W_A, W_A + W_B

SUBLANES = 8
LANES = 128
A_TAIL = SUBLANES
B_TAIL = 32
V7X_VMEM_BYTES = 64 * 1024 * 1024


def _vmem_limit(block_bytes, scratch_bytes, temp_bytes):
    need = 2 * sum(block_bytes) + scratch_bytes + temp_bytes
    return int(min(need + need // 8, V7X_VMEM_BYTES - 4 * 1024 * 1024))


def _rms(x, g):
    return (x * lax.rsqrt(jnp.mean(x * x, axis=-1, keepdims=True) + EPS)) * g


def _layer_norm(x, g, b):
    xc = x - jnp.mean(x, axis=-1, keepdims=True)
    y = xc * lax.rsqrt(jnp.mean(xc * xc, axis=-1, keepdims=True) + EPS)
    return y * g + b


def _softplus(x):
    return jnp.maximum(x, 0.0) + jnp.log1p(jnp.exp(-jnp.abs(x)))


def _lru_gates(conv_h, pre, b_r, b_i, sp):
    r = jax.nn.sigmoid(pre[:, :HD_A] + b_r)
    i = jax.nn.sigmoid(pre[:, HD_A:] + b_i)
    log_a = (-LRU_C * r) * sp
    a = jnp.exp(log_a)
    t = jnp.tanh(log_a)
    u = jnp.sqrt(-2.0 * t / (1.0 - t)) * (i * conv_h)
    return a, u


def _in_proj_kernel(x_ref, g_ref, w_ref, z_ref, hn_ref):
    @pl.when(pl.program_id(1) == 0)
    def _():
        hn_ref[...] = _rms(x_ref[...], g_ref[...]).astype(BF16)

    z_ref[...] = jnp.dot(hn_ref[...], w_ref[...], preferred_element_type=F32)


def _in_proj(x, g, w, layer, tm, tn):
    m = x.shape[0]
    return pl.pallas_call(
        _in_proj_kernel,
        grid=(m // tm, D_IN // tn),
        in_specs=[
            pl.BlockSpec((tm, D_MODEL), lambda i, j: (i, 0)),
            pl.BlockSpec((None, 1, D_MODEL), lambda i, j: (layer, 0, 0)),
            pl.BlockSpec((None, D_MODEL, tn), lambda i, j: (layer, 0, j)),
        ],
        out_specs=pl.BlockSpec((tm, tn), lambda i, j: (i, j)),
        out_shape=jax.ShapeDtypeStruct((m, D_IN), F32),
        scratch_shapes=[pltpu.VMEM((tm, D_MODEL), BF16)],
        compiler_params=pltpu.CompilerParams(
            dimension_semantics=("arbitrary", "arbitrary"),
            vmem_limit_bytes=_vmem_limit(
                [tm * D_MODEL * 4, D_MODEL * 4, D_MODEL * tn * 2, tm * tn * 4],
                tm * D_MODEL * 2, tm * D_MODEL * 4 + tm * tn * 4),
        ),
        name="in_proj",
    )(x, g, w)


def _out_proj_kernel(x_ref, mix_ref, w_ref, o_ref):
    o_ref[...] = x_ref[...] + jnp.dot(mix_ref[...], w_ref[...], preferred_element_type=F32)


def _out_proj(x, mix, w, layer, tm, tn):
    m = x.shape[0]
    return pl.pallas_call(
        _out_proj_kernel,
        grid=(m // tm, D_MODEL // tn),
        in_specs=[
            pl.BlockSpec((tm, tn), lambda i, j: (i, j)),
            pl.BlockSpec((tm, D_MODEL), lambda i, j: (i, 0)),
            pl.BlockSpec((None, D_MODEL, tn), lambda i, j: (layer, 0, j)),
        ],
        out_specs=pl.BlockSpec((tm, tn), lambda i, j: (i, j)),
        out_shape=jax.ShapeDtypeStruct((m, D_MODEL), F32),
        compiler_params=pltpu.CompilerParams(
            dimension_semantics=("arbitrary", "arbitrary"),
            vmem_limit_bytes=_vmem_limit(
                [tm * tn * 4, tm * D_MODEL * 2, D_MODEL * tn * 2, tm * tn * 4],
                0, tm * tn * 4),
        ),
        name="out_proj",
    )(x, mix, w)


def _ffn_kernel(*refs, final_norm):
    if final_norm:
        x_ref, g_ref, w1_ref, w2_ref, gf_ref, o_ref, hf_ref = refs
    else:
        x_ref, g_ref, w1_ref, w2_ref, o_ref, hf_ref = refs
    j = pl.program_id(1)

    @pl.when(j == 0)
    def _():
        x = x_ref[...]
        hf_ref[...] = _rms(x, g_ref[...]).astype(BF16)
        o_ref[...] = x

    h = jnp.dot(hf_ref[...], w1_ref[...], preferred_element_type=F32)
    h = jnp.square(jnp.maximum(h, 0.0)).astype(BF16)
    o_ref[...] += jnp.dot(h, w2_ref[...], preferred_element_type=F32)

    if final_norm:
        @pl.when(j == pl.num_programs(1) - 1)
        def _():
            o_ref[...] = _rms(o_ref[...], gf_ref[...])


def _ffn(x, g, w1, w2, layer, tm, tf, g_final=None):
    m = x.shape[0]
    final_norm = g_final is not None
    in_specs = [
        pl.BlockSpec((tm, D_MODEL), lambda i, j: (i, 0)),
        pl.BlockSpec((None, 1, D_MODEL), lambda i, j: (layer, 0, 0)),
        pl.BlockSpec((None, D_MODEL, tf), lambda i, j: (layer, 0, j)),
        pl.BlockSpec((None, tf, D_MODEL), lambda i, j: (layer, j, 0)),
    ]
    args = [x, g, w1, w2]
    if final_norm:
        in_specs.append(pl.BlockSpec((1, D_MODEL), lambda i, j: (0, 0)))
        args.append(g_final)
    return pl.pallas_call(
        functools.partial(_ffn_kernel, final_norm=final_norm),
        grid=(m // tm, D_FF // tf),
        in_specs=in_specs,
        out_specs=pl.BlockSpec((tm, D_MODEL), lambda i, j: (i, 0)),
        out_shape=jax.ShapeDtypeStruct((m, D_MODEL), F32),
        scratch_shapes=[pltpu.VMEM((tm, D_MODEL), BF16)],
        compiler_params=pltpu.CompilerParams(
            dimension_semantics=("arbitrary", "arbitrary"),
            vmem_limit_bytes=_vmem_limit(
                [tm * D_MODEL * 4, D_MODEL * 4, D_MODEL * tf * 2, tf * D_MODEL * 2, tm * D_MODEL * 4],
                tm * D_MODEL * 2, tm * tf * 6 + tm * D_MODEL * 4),
        ),
        name="ffn",
    )(*args)


def _mixer_prompt_kernel(
        z_ref, caw_ref, cab_ref, gw_ref, br_ref, bi_ref, lam_ref, cbw_ref, lbg_ref, lbb_ref,
        sg_ref, sb_ref, sw_ref, sbt_ref,
        mix_ref, h_out_ref, cb_out_ref,
        xa_buf, ub_buf, a_s, u_s, h_s, carry, *, tl):
    j = pl.program_id(1)

    @pl.when(j == 0)
    def _():
        xa_buf[0:A_TAIL, :] = jnp.zeros((A_TAIL, W_A), F32)
        ub_buf[0:B_TAIL, :] = jnp.zeros((B_TAIL, W_B), F32)
        carry[...] = jnp.zeros((1, W_A), F32)

    xa_buf[A_TAIL:A_TAIL + tl, :] = z_ref[:, OFF_XA:OFF_XA + W_A]
    sp = _softplus(-lam_ref[...])
    for h in range(H_A):
        cs = slice(h * HD_A, (h + 1) * HD_A)
        conv_h = cab_ref[:, cs]
        for k in range(K_A):
            start = A_TAIL - (K_A - 1) + k
            conv_h = conv_h + caw_ref[k:k + 1, cs] * xa_buf[start:start + tl, cs]
        pre = jnp.dot(conv_h.astype(BF16), gw_ref[h], preferred_element_type=F32)
        a, u = _lru_gates(conv_h, pre, br_ref[:, cs], bi_ref[:, cs], sp[:, cs])
        a_s[:, cs] = a
        u_s[:, cs] = u
    xa_buf[0:A_TAIL, :] = xa_buf[tl:tl + A_TAIL, :]

    def scan_row(t, h):
        h = a_s[pl.ds(t, 1), :] * h + u_s[pl.ds(t, 1), :]
        h_s[pl.ds(t, 1), :] = h
        return h

    h_last = lax.fori_loop(0, tl, scan_row, carry[...], unroll=8)
    carry[...] = h_last
    h_out_ref[...] = h_last
    mix_ref[:, OFF_YA:OFF_YA + W_A] = (
        h_s[...] * jax.nn.gelu(z_ref[:, OFF_GA:OFF_GA + W_A])).astype(BF16)

    ub_buf[B_TAIL:B_TAIL + tl, :] = (
        z_ref[:, OFF_XB:OFF_XB + W_B] * jax.nn.sigmoid(z_ref[:, OFF_GB:OFF_GB + W_B]))
    rows = 64
    for r0 in range(0, tl, rows):
        cols = []
        for c0 in range(0, W_B, LANES):
            acc = jnp.zeros((rows, LANES), F32)
            for k in range(K_B):
                start = B_TAIL - (K_B - 1) + k + r0
                acc = acc + cbw_ref[k:k + 1, c0:c0 + LANES] * ub_buf[start:start + rows, c0:c0 + LANES]
            cols.append(acc)
        y_b = _layer_norm(jnp.concatenate(cols, axis=-1), lbg_ref[...], lbb_ref[...])
        mix_ref[r0:r0 + rows, OFF_YB:OFF_YB + W_B] = jax.nn.silu(y_b).astype(BF16)
    tail = ub_buf[tl:tl + B_TAIL, :]
    ub_buf[0:B_TAIL, :] = tail
    cb_out_ref[...] = tail

    g_c = jax.nn.gelu(z_ref[:, OFF_ZC:OFF_ZC + 2 * W_C])
    u_c = g_c[:, :W_C]
    v_n = _layer_norm(g_c[:, W_C:], sg_ref[...], sb_ref[...]).astype(BF16)
    causal = (lax.broadcasted_iota(jnp.int32, (CHUNK, CHUNK), 0)
              >= lax.broadcasted_iota(jnp.int32, (CHUNK, CHUNK), 1))
    for h in range(H_C):
        cs = slice(h * HD_C, (h + 1) * HD_C)
        w_h = jnp.where(causal, sw_ref[h], 0.0).astype(BF16)
        for r0 in range(0, tl, CHUNK):
            mixed = jnp.dot(w_h, v_n[r0:r0 + CHUNK, cs], preferred_element_type=F32)
            mixed = mixed + sbt_ref[:, h:h + 1]
            mix_ref[r0:r0 + CHUNK, OFF_YC + h * HD_C:OFF_YC + (h + 1) * HD_C] = (
                u_c[r0:r0 + CHUNK, cs] * mixed).astype(BF16)


def _mixer_prompt(z, p, layer, tl):
    nb, seq, _ = z.shape

    def per_layer(shape):
        return pl.BlockSpec((None,) + shape, lambda b, j: (layer,) + (0,) * len(shape))

    scratch = [
        pltpu.VMEM((A_TAIL + tl, W_A), F32),
        pltpu.VMEM((B_TAIL + tl, W_B), F32),
        pltpu.VMEM((tl, W_A), F32),
        pltpu.VMEM((tl, W_A), F32),
        pltpu.VMEM((tl, W_A), F32),
        pltpu.VMEM((1, W_A), F32),
    ]
    return pl.pallas_call(
        functools.partial(_mixer_prompt_kernel, tl=tl),
        grid=(nb, seq // tl),
        in_specs=[
            pl.BlockSpec((None, tl, D_IN), lambda b, j: (b, j, 0)),
            per_layer((K_A, W_A)), per_layer((1, W_A)), per_layer((H_A, HD_A, 2 * HD_A)),
            per_layer((1, W_A)), per_layer((1, W_A)), per_layer((1, W_A)),
            per_layer((K_B, W_B)), per_layer((1, W_B)), per_layer((1, W_B)),
            per_layer((1, W_C)), per_layer((1, W_C)), per_layer((H_C, CHUNK, CHUNK)),
            per_layer((CHUNK, H_C)),
        ],
        out_specs=[
            pl.BlockSpec((None, tl, D_MODEL), lambda b, j: (b, j, 0)),
            pl.BlockSpec((None, 1, W_A), lambda b, j: (b, 0, 0)),
            pl.BlockSpec((None, B_TAIL, W_B), lambda b, j: (b, 0, 0)),
        ],
        out_shape=[
            jax.ShapeDtypeStruct((nb, seq, D_MODEL), BF16),
            jax.ShapeDtypeStruct((nb, 1, W_A), F32),
            jax.ShapeDtypeStruct((nb, B_TAIL, W_B), F32),
        ],
        scratch_shapes=scratch,
        compiler_params=pltpu.CompilerParams(
            dimension_semantics=("arbitrary", "arbitrary"),
            vmem_limit_bytes=_vmem_limit(
                [tl * D_IN * 4, tl * D_MODEL * 2, 1024 * 1024],
                (5 * tl + A_TAIL) * W_A * 4, 8 * tl * W_A * 4),
        ),
        name="mixer_prompt",
    )(z, p["conv_a_w"], p["conv_a_b"], p["gate_w"], p["gate_r_b"], p["gate_i_b"], p["lru_lambda"],
      p["conv_b_w"], p["ln_b_g"], p["ln_b_b"], p["sgu_ln_g"], p["sgu_ln_b"], p["sgu_w"], p["sgu_b_t"])


def _mixer_sample_kernel(
        z_ref, ca_ref, h0_ref, cb_ref, caw_ref, cab_ref, gw_ref, br_ref, bi_ref, lam_ref, cbw_ref,
        lbg_ref, lbb_ref, sg_ref, sb_ref, sw0_ref, sb0_ref,
        mix_ref, ca_out_ref, h_out_ref, cb_out_ref, vn_out_ref):
    xa = z_ref[:, OFF_XA:OFF_XA + W_A]
    for k in range(K_A - 2):
        ca_out_ref[k] = ca_ref[k + 1]
    ca_out_ref[K_A - 2] = xa
    sp = _softplus(-lam_ref[...])
    for h in range(H_A):
        cs = slice(h * HD_A, (h + 1) * HD_A)
        conv_h = cab_ref[:, cs] + caw_ref[K_A - 1:K_A, cs] * xa[:, cs]
        for k in range(K_A - 1):
            conv_h = conv_h + caw_ref[k:k + 1, cs] * ca_ref[k, :, cs]
        pre = jnp.dot(conv_h.astype(BF16), gw_ref[h], preferred_element_type=F32)
        a, u = _lru_gates(conv_h, pre, br_ref[:, cs], bi_ref[:, cs], sp[:, cs])
        h_new = a * h0_ref[:, cs] + u
        h_out_ref[:, cs] = h_new
        mix_ref[:, OFF_YA + h * HD_A:OFF_YA + (h + 1) * HD_A] = (
            h_new * jax.nn.gelu(z_ref[:, OFF_GA + h * HD_A:OFF_GA + (h + 1) * HD_A])).astype(BF16)

    ub = z_ref[:, OFF_XB:OFF_XB + W_B] * jax.nn.sigmoid(z_ref[:, OFF_GB:OFF_GB + W_B])
    acc = cbw_ref[K_B - 1:K_B, :] * ub
    for k in range(K_B - 1):
        acc = acc + cbw_ref[k:k + 1, :] * cb_ref[k]
    for k in range(K_B - 2):
        cb_out_ref[k] = cb_ref[k + 1]
    cb_out_ref[K_B - 2] = ub
    y_b = _layer_norm(acc, lbg_ref[...], lbb_ref[...])
    mix_ref[:, OFF_YB:OFF_YB + W_B] = jax.nn.silu(y_b).astype(BF16)

    g_c = jax.nn.gelu(z_ref[:, OFF_ZC:OFF_ZC + 2 * W_C])
    v_n = _layer_norm(g_c[:, W_C:], sg_ref[...], sb_ref[...])
    vn_out_ref[...] = v_n
    mix_ref[:, OFF_YC:OFF_YC + W_C] = (g_c[:, :W_C] * (sw0_ref[...] * v_n + sb0_ref[...])).astype(BF16)


def _mixer_sample(z, ca_t, h0, cb_t, p, layer, tb):
    nb = z.shape[0]

    def per_layer(shape):
        return pl.BlockSpec((None,) + shape, lambda b: (layer,) + (0,) * len(shape))

    return pl.pallas_call(
        _mixer_sample_kernel,
        grid=(nb // tb,),
        in_specs=[
            pl.BlockSpec((tb, D_IN), lambda b: (b, 0)),
            pl.BlockSpec((None, K_A - 1, tb, W_A), lambda b: (layer, 0, b, 0)),
            pl.BlockSpec((None, tb, W_A), lambda b: (layer, b, 0)),
            pl.BlockSpec((None, K_B - 1, tb, W_B), lambda b: (layer, 0, b, 0)),
            per_layer((K_A, W_A)), per_layer((1, W_A)), per_layer((H_A, HD_A, 2 * HD_A)),
            per_layer((1, W_A)), per_layer((1, W_A)), per_layer((1, W_A)),
            per_layer((K_B, W_B)), per_layer((1, W_B)), per_layer((1, W_B)),
            per_layer((1, W_C)), per_layer((1, W_C)), per_layer((1, W_C)), per_layer((1, W_C)),
        ],
        out_specs=[
            pl.BlockSpec((tb, D_MODEL), lambda b: (b, 0)),
            pl.BlockSpec((K_A - 1, tb, W_A), lambda b: (0, b, 0)),
            pl.BlockSpec((tb, W_A), lambda b: (b, 0)),
            pl.BlockSpec((K_B - 1, tb, W_B), lambda b: (0, b, 0)),
            pl.BlockSpec((tb, W_C), lambda b: (b, 0)),
        ],
        out_shape=[
            jax.ShapeDtypeStruct((nb, D_MODEL), BF16),
            jax.ShapeDtypeStruct((K_A - 1, nb, W_A), F32),
            jax.ShapeDtypeStruct((nb, W_A), F32),
            jax.ShapeDtypeStruct((K_B - 1, nb, W_B), F32),
            jax.ShapeDtypeStruct((nb, W_C), F32),
        ],
        compiler_params=pltpu.CompilerParams(
            dimension_semantics=("arbitrary",),
            vmem_limit_bytes=_vmem_limit(
                [tb * D_IN * 4, 2 * (K_A - 1) * tb * W_A * 4, 2 * tb * W_A * 4,
                 2 * (K_B - 1) * tb * W_B * 4, tb * D_MODEL * 2, tb * W_C * 4, 1024 * 1024],
                0, 8 * tb * W_A * 4),
        ),
        name="mixer_sample",
    )(z, ca_t, h0, cb_t, p["conv_a_w"], p["conv_a_b"], p["gate_w"], p["gate_r_b"], p["gate_i_b"],
      p["lru_lambda"], p["conv_b_w"], p["ln_b_g"], p["ln_b_b"], p["sgu_ln_g"], p["sgu_ln_b"],
      p["sgu_w00"], p["sgu_b0"])


def kernel(x_prompt, x_sample, state_conv_a, state_lru_h, state_conv_b, norm_mix, w_in, conv_a_w,
           conv_a_b, gate_r_w, gate_r_b, gate_i_w, gate_i_b, lru_lambda, conv_b_w, ln_b_g, ln_b_b,
           sgu_ln_g, sgu_ln_b, sgu_w, sgu_b, w_out, norm_ffn, w_ff1, w_ff2, norm_final):
    depth = w_in.shape[0]
    nb, seq, _ = x_prompt.shape
    ns = x_sample.shape[0]

    def row(v):
        return v[:, None, :]

    p = {
        "conv_a_w": conv_a_w, "conv_a_b": row(conv_a_b),
        "gate_w": jnp.concatenate([gate_r_w, gate_i_w], axis=-1).astype(BF16),
        "gate_r_b": row(gate_r_b), "gate_i_b": row(gate_i_b), "lru_lambda": row(lru_lambda),
        "conv_b_w": conv_b_w, "ln_b_g": row(ln_b_g), "ln_b_b": row(ln_b_b),
        "sgu_ln_g": row(sgu_ln_g), "sgu_ln_b": row(sgu_ln_b),
        "sgu_w": sgu_w, "sgu_b_t": jnp.swapaxes(sgu_b, 1, 2),
        "sgu_w00": row(jnp.repeat(sgu_w[:, :, 0, 0], HD_C, axis=-1)),
        "sgu_b0": row(jnp.repeat(sgu_b[:, :, 0], HD_C, axis=-1)),
    }
    g_mix, g_ffn, g_final = row(norm_mix), row(norm_ffn), norm_final[None, :]
    w_in_b, w_out_b = w_in.astype(BF16), w_out.astype(BF16)
    w_ff1_b, w_ff2_b = w_ff1.astype(BF16), w_ff2.astype(BF16)

    ca_t = jnp.swapaxes(state_conv_a, 1, 2)
    cb_t = jnp.swapaxes(state_conv_b, 1, 2)

    xp = x_prompt.reshape(nb * seq, D_MODEL)
    xs = x_sample.reshape(ns, D_MODEL)
    ca_p, h_p, cb_p, ca_s, h_s, cb_s, v_s = [], [], [], [], [], [], []
    for l in range(depth):
        last = g_final if l == depth - 1 else None
        z = _in_proj(xp, g_mix, w_in_b, l, tm=1024, tn=1024).reshape(nb, seq, D_IN)
        mix, h_l, cb_l = _mixer_prompt(z, p, l, tl=256)
        ca_p.append(z[:, seq - (K_A - 1):, OFF_XA:OFF_XA + W_A])
        h_p.append(h_l[:, 0, :])
        cb_p.append(cb_l[:, B_TAIL - (K_B - 1):, :])
        xp = _out_proj(xp, mix.reshape(nb * seq, D_MODEL), w_out_b, l, tm=1024, tn=1024)
        xp = _ffn(xp, g_ffn, w_ff1_b, w_ff2_b, l, tm=512, tf=512, g_final=last)
        zs = _in_proj(xs, g_mix, w_in_b, l, tm=ns, tn=1024)
        mix_s, ca_l, hs_l, cbs_l, v_l = _mixer_sample(zs, ca_t, state_lru_h, cb_t, p, l, tb=32)
        ca_s.append(jnp.swapaxes(ca_l, 0, 1))
        h_s.append(hs_l)
        cb_s.append(jnp.swapaxes(cbs_l, 0, 1))
        v_s.append(v_l[:, None, :])
        xs = _out_proj(xs, mix_s, w_out_b, l, tm=ns, tn=1024)
        xs = _ffn(xs, g_ffn, w_ff1_b, w_ff2_b, l, tm=ns, tf=1024, g_final=last)

    return (xp.reshape(nb, seq, D_MODEL), xs.reshape(ns, 1, D_MODEL),
            jnp.stack(ca_p), jnp.stack(h_p), jnp.stack(cb_p),
            jnp.stack(ca_s), jnp.stack(h_s), jnp.stack(cb_s), jnp.stack(v_s))
```

```python
import functools

import jax
import jax.numpy as jnp
from jax import lax
from jax.experimental import pallas as pl
from jax.experimental.pallas import tpu as pltpu

F32 = jnp.float32
BF16 = jnp.bfloat16

D_MODEL = 2048
W_A = 1024
H_A = 8
HD_A = W_A // H_A
K_A = 4
LRU_C = 8.0
W_B = 512
K_B = 31
W_C = 512
H_C = 4
HD_C = W_C // H_C
CHUNK = 128
D_IN = 2 * (W_A + W_B + W_C)
D_FF = 4 * D_MODEL
EPS = 1e-6

OFF_XA, OFF_GA, OFF_XB, OFF_GB, OFF_ZC = 0, W_A, 2 * W_A, 2 * W_A + W_B, 2 * W_A + 2 * W_B
OFF_YA, OFF_YB, OFF_YC = 0, W_A, W_A + W_B

SUBLANES = 8
LANES = 128
A_TAIL = SUBLANES
B_TAIL = 32
V7X_VMEM_BYTES = 64 * 1024 * 1024


def _vmem_limit(block_bytes, scratch_bytes, temp_bytes):
    need = 2 * sum(block_bytes) + scratch_bytes + temp_bytes
    return int(min(need + need // 8, V7X_VMEM_BYTES - 4 * 1024 * 1024))


def _rms(x, g):
    return (x * lax.rsqrt(jnp.mean(x * x, axis=-1, keepdims=True) + EPS)) * g


def _layer_norm(x, g, b):
    xc = x - jnp.mean(x, axis=-1, keepdims=True)
    y = xc * lax.rsqrt(jnp.mean(xc * xc, axis=-1, keepdims=True) + EPS)
    return y * g + b


def _softplus(x):
    return jnp.maximum(x, 0.0) + jnp.log1p(jnp.exp(-jnp.abs(x)))


def _lru_gates(conv_h, pre, b_r, b_i, sp):
    r = jax.nn.sigmoid(pre[:, :HD_A] + b_r)
    i = jax.nn.sigmoid(pre[:, HD_A:] + b_i)
    log_a = (-LRU_C * r) * sp
    a = jnp.exp(log_a)
    t = jnp.tanh(log_a)
    u = jnp.sqrt(-2.0 * t / (1.0 - t)) * (i * conv_h)
    return a, u


def _in_proj_kernel(x_ref, g_ref, w_ref, z_ref, hn_ref):
    @pl.when(pl.program_id(1) == 0)
    def _():
        hn_ref[...] = _rms(x_ref[...], g_ref[...]).astype(BF16)

    z_ref[...] = jnp.dot(hn_ref[...], w_ref[...], preferred_element_type=F32)


def _in_proj(x, g, w, layer, tm, tn):
    m = x.shape[0]
    return pl.pallas_call(
        _in_proj_kernel,
        grid=(m // tm, D_IN // tn),
        in_specs=[
            pl.BlockSpec((tm, D_MODEL), lambda i, j: (i, 0)),
            pl.BlockSpec((None, 1, D_MODEL), lambda i, j: (layer, 0, 0)),
            pl.BlockSpec((None, D_MODEL, tn), lambda i, j: (layer, 0, j)),
        ],
        out_specs=pl.BlockSpec((tm, tn), lambda i, j: (i, j)),
        out_shape=jax.ShapeDtypeStruct((m, D_IN), F32),
        scratch_shapes=[pltpu.VMEM((tm, D_MODEL), BF16)],
        compiler_params=pltpu.CompilerParams(
            dimension_semantics=("arbitrary", "arbitrary"),
            vmem_limit_bytes=_vmem_limit(
                [tm * D_MODEL * 4, D_MODEL * 4, D_MODEL * tn * 2, tm * tn * 4],
                tm * D_MODEL * 2, tm * D_MODEL * 4 + tm * tn * 4),
        ),
        name="in_proj",
    )(x, g, w)


def _out_proj_kernel(x_ref, mix_ref, w_ref, o_ref):
    o_ref[...] = x_ref[...] + jnp.dot(mix_ref[...], w_ref[...], preferred_element_type=F32)


def _out_proj(x, mix, w, layer, tm, tn):
    m = x.shape[0]
    return pl.pallas_call(
        _out_proj_kernel,
        grid=(m // tm, D_MODEL // tn),
        in_specs=[
            pl.BlockSpec((tm, tn), lambda i, j: (i, j)),
            pl.BlockSpec((tm, D_MODEL), lambda i, j: (i, 0)),
            pl.BlockSpec((None, D_MODEL, tn), lambda i, j: (layer, 0, j)),
        ],
        out_specs=pl.BlockSpec((tm, tn), lambda i, j: (i, j)),
        out_shape=jax.ShapeDtypeStruct((m, D_MODEL), F32),
        compiler_params=pltpu.CompilerParams(
            dimension_semantics=("arbitrary", "arbitrary"),
            vmem_limit_bytes=_vmem_limit(
                [tm * tn * 4, tm * D_MODEL * 2, D_MODEL * tn * 2, tm * tn * 4],
                0, tm * tn * 4),
        ),
        name="out_proj",
    )(x, mix, w)


def _ffn_kernel(*refs, final_norm):
    if final_norm:
        x_ref, g_ref, w1_ref, w2_ref, gf_ref, o_ref, hf_ref = refs
    else:
        x_ref, g_ref, w1_ref, w2_ref, o_ref, hf_ref = refs
    j = pl.program_id(1)

    @pl.when(j == 0)
    def _():
        x = x_ref[...]
        hf_ref[...] = _rms(x, g_ref[...]).astype(BF16)
        o_ref[...] = x

    h = jnp.dot(hf_ref[...], w1_ref[...], preferred_element_type=F32)
    h = jnp.square(jnp.maximum(h, 0.0)).astype(BF16)
    o_ref[...] += jnp.dot(h, w2_ref[...], preferred_element_type=F32)

    if final_norm:
        @pl.when(j == pl.num_programs(1) - 1)
        def _():
            o_ref[...] = _rms(o_ref[...], gf_ref[...])


def _ffn(x, g, w1, w2, layer, tm, tf, g_final=None):
    m = x.shape[0]
    final_norm = g_final is not None
    in_specs = [
        pl.BlockSpec((tm, D_MODEL), lambda i, j: (i, 0), pipeline_mode=pl.Buffered(1)),
        pl.BlockSpec((None, 1, D_MODEL), lambda i, j: (layer, 0, 0)),
        pl.BlockSpec((None, D_MODEL, tf), lambda i, j: (layer, 0, j)),
        pl.BlockSpec((None, tf, D_MODEL), lambda i, j: (layer, j, 0)),
    ]
    args = [x, g, w1, w2]
    if final_norm:
        in_specs.append(pl.BlockSpec((1, D_MODEL), lambda i, j: (0, 0)))
        args.append(g_final)
    return pl.pallas_call(
        functools.partial(_ffn_kernel, final_norm=final_norm),
        grid=(m // tm, D_FF // tf),
        in_specs=in_specs,
        out_specs=pl.BlockSpec((tm, D_MODEL), lambda i, j: (i, 0)),
        out_shape=jax.ShapeDtypeStruct((m, D_MODEL), F32),
        scratch_shapes=[pltpu.VMEM((tm, D_MODEL), BF16)],
        compiler_params=pltpu.CompilerParams(
            dimension_semantics=("arbitrary", "arbitrary"),
            vmem_limit_bytes=_vmem_limit(
                [D_MODEL * 4, D_MODEL * tf * 2, tf * D_MODEL * 2, tm * D_MODEL * 4],
                tm * D_MODEL * 4 + tm * D_MODEL * 2, tm * tf * 6 + tm * D_MODEL * 4),
        ),
        name="ffn",
    )(*args)


def _conv_b_tile(w_ref, buf, r0, c0, rows):
    first = B_TAIL - (K_B - 1)
    cs = slice(c0, c0 + LANES)
    y = None
    for s in range(SUBLANES):
        n = rows if s == 0 else rows + SUBLANES
        part = None
        for kp in range(s, B_TAIL + 1, SUBLANES):
            if kp < first:
                continue
            start = r0 + kp - s
            term = w_ref[kp - first:kp - first + 1, cs] * buf[start:start + n, cs]
            part = term if part is None else part + term
        y = part if s == 0 else y + pltpu.roll(part, n - s, axis=0)[0:rows, :]
    return y


def _mixer_prompt_kernel(
        z_ref, caw_ref, cab_ref, gw_ref, br_ref, bi_ref, lam_ref, cbw_ref, lbg_ref, lbb_ref,
        sg_ref, sb_ref, sw_ref, sbt_ref,
        mix_ref, h_out_ref, cb_out_ref,
        xa_buf, ub_buf, a_s, u_s, h_s, carry, *, tl):
    j = pl.program_id(1)

    @pl.when(j == 0)
    def _():
        xa_buf[0:A_TAIL, :] = jnp.zeros((A_TAIL, W_A), F32)
        ub_buf[0:B_TAIL, :] = jnp.zeros((B_TAIL, W_B), F32)
        carry[...] = jnp.zeros((1, W_A), F32)

    xa_buf[A_TAIL:A_TAIL + tl, :] = z_ref[:, OFF_XA:OFF_XA + W_A]
    sp = _softplus(-lam_ref[...])
    for h in range(H_A):
        cs = slice(h * HD_A, (h + 1) * HD_A)
        xa_full = xa_buf[0:A_TAIL + tl, cs]
        conv_h = cab_ref[:, cs] + caw_ref[K_A - 1:K_A, cs] * xa_full[A_TAIL:, :]
        for k in range(K_A - 1):
            shifted = pltpu.roll(xa_full, K_A - 1 - k, axis=0)[A_TAIL:, :]
            conv_h = conv_h + caw_ref[k:k + 1, cs] * shifted
        pre = jnp.dot(conv_h.astype(BF16), gw_ref[h], preferred_element_type=F32)
        a, u = _lru_gates(conv_h, pre, br_ref[:, cs], bi_ref[:, cs], sp[:, cs])
        a_s[:, cs] = a
        u_s[:, cs] = u
    xa_buf[0:A_TAIL, :] = xa_buf[tl:tl + A_TAIL, :]

    def scan_row(t, h):
        h = a_s[pl.ds(t, 1), :] * h + u_s[pl.ds(t, 1), :]
        h_s[pl.ds(t, 1), :] = h
        return h

    h_last = lax.fori_loop(0, tl, scan_row, carry[...], unroll=8)
    carry[...] = h_last
    h_out_ref[...] = h_last
    mix_ref[:, OFF_YA:OFF_YA + W_A] = (
        h_s[...] * jax.nn.gelu(z_ref[:, OFF_GA:OFF_GA + W_A])).astype(BF16)

    ub_buf[B_TAIL:B_TAIL + tl, :] = (
        z_ref[:, OFF_XB:OFF_XB + W_B] * jax.nn.sigmoid(z_ref[:, OFF_GB:OFF_GB + W_B]))
    rows = 64
    for r0 in range(0, tl, rows):
        cols = []
        for c0 in range(0, W_B, LANES):
            cols.append(_conv_b_tile(cbw_ref, ub_buf, r0, c0, rows))
        y_b = _layer_norm(jnp.concatenate(cols, axis=-1), lbg_ref[...], lbb_ref[...])
        mix_ref[r0:r0 + rows, OFF_YB:OFF_YB + W_B] = jax.nn.silu(y_b).astype(BF16)
    tail = ub_buf[tl:tl + B_TAIL, :]
    ub_buf[0:B_TAIL, :] = tail
    cb_out_ref[...] = tail

    g_c = jax.nn.gelu(z_ref[:, OFF_ZC:OFF_ZC + 2 * W_C])
    u_c = g_c[:, :W_C]
    v_n = _layer_norm(g_c[:, W_C:], sg_ref[...], sb_ref[...]).astype(BF16)
    causal = (lax.broadcasted_iota(jnp.int32, (CHUNK, CHUNK), 0)
              >= lax.broadcasted_iota(jnp.int32, (CHUNK, CHUNK), 1))
    for h in range(H_C):
        cs = slice(h * HD_C, (h + 1) * HD_C)
        w_h = jnp.where(causal, sw_ref[h], 0.0).astype(BF16)
        for r0 in range(0, tl, CHUNK):
            mixed = jnp.dot(w_h, v_n[r0:r0 + CHUNK, cs], preferred_element_type=F32)
            mixed = mixed + sbt_ref[:, h:h + 1]
            mix_ref[r0:r0 + CHUNK, OFF_YC + h * HD_C:OFF_YC + (h + 1) * HD_C] = (
                u_c[r0:r0 + CHUNK, cs] * mixed).astype(BF16)


def _mixer_prompt(z, p, layer, tl):
    nb, seq, _ = z.shape

    def per_layer(shape):
        return pl.BlockSpec((None,) + shape, lambda b, j: (layer,) + (0,) * len(shape))

    scratch = [
        pltpu.VMEM((A_TAIL + tl, W_A), F32),
        pltpu.VMEM((B_TAIL + tl, W_B), F32),
        pltpu.VMEM((tl, W_A), F32),
        pltpu.VMEM((tl, W_A), F32),
        pltpu.VMEM((tl, W_A), F32),
        pltpu.VMEM((1, W_A), F32),
    ]
    return pl.pallas_call(
        functools.partial(_mixer_prompt_kernel, tl=tl),
        grid=(nb, seq // tl),
        in_specs=[
            pl.BlockSpec((None, tl, D_IN), lambda b, j: (b, j, 0)),
            per_layer((K_A, W_A)), per_layer((1, W_A)), per_layer((H_A, HD_A, 2 * HD_A)),
            per_layer((1, W_A)), per_layer((1, W_A)), per_layer((1, W_A)),
            per_layer((K_B, W_B)), per_layer((1, W_B)), per_layer((1, W_B)),
            per_layer((1, W_C)), per_layer((1, W_C)), per_layer((H_C, CHUNK, CHUNK)),
            per_layer((CHUNK, H_C)),
        ],
        out_specs=[
            pl.BlockSpec((None, tl, D_MODEL), lambda b, j: (b, j, 0)),
            pl.BlockSpec((None, 1, W_A), lambda b, j: (b, 0, 0)),
            pl.BlockSpec((None, B_TAIL, W_B), lambda b, j: (b, 0, 0)),
        ],
        out_shape=[
            jax.ShapeDtypeStruct((nb, seq, D_MODEL), BF16),
            jax.ShapeDtypeStruct((nb, 1, W_A), F32),
            jax.ShapeDtypeStruct((nb, B_TAIL, W_B), F32),
        ],
        scratch_shapes=scratch,
        compiler_params=pltpu.CompilerParams(
            dimension_semantics=("arbitrary", "arbitrary"),
            vmem_limit_bytes=_vmem_limit(
                [tl * D_IN * 4, tl * D_MODEL * 2, 1024 * 1024],
                (5 * tl + A_TAIL) * W_A * 4, 8 * tl * W_A * 4),
        ),
        name="mixer_prompt",
    )(z, p["conv_a_w"], p["conv_a_b"], p["gate_w"], p["gate_r_b"], p["gate_i_b"], p["lru_lambda"],
      p["conv_b_w"], p["ln_b_g"], p["ln_b_b"], p["sgu_ln_g"], p["sgu_ln_b"], p["sgu_w"], p["sgu_b_t"])


def _mixer_sample_kernel(
        z_ref, ca_ref, h0_ref, cb_ref, caw_ref, cab_ref, gw_ref, br_ref, bi_ref, lam_ref, cbw_ref,
        lbg_ref, lbb_ref, sg_ref, sb_ref, sw0_ref, sb0_ref,
        mix_ref, ca_out_ref, h_out_ref, cb_out_ref, vn_out_ref):
    xa = z_ref[:, OFF_XA:OFF_XA + W_A]
    for k in range(K_A - 2):
        ca_out_ref[k] = ca_ref[k + 1]
    ca_out_ref[K_A - 2] = xa
    sp = _softplus(-lam_ref[...])
    for h in range(H_A):
        cs = slice(h * HD_A, (h + 1) * HD_A)
        conv_h = cab_ref[:, cs] + caw_ref[K_A - 1:K_A, cs] * xa[:, cs]
        for k in range(K_A - 1):
            conv_h = conv_h + caw_ref[k:k + 1, cs] * ca_ref[k, :, cs]
        pre = jnp.dot(conv_h.astype(BF16), gw_ref[h], preferred_element_type=F32)
        a, u = _lru_gates(conv_h, pre, br_ref[:, cs], bi_ref[:, cs], sp[:, cs])
        h_new = a * h0_ref[:, cs] + u
        h_out_ref[:, cs] = h_new
        mix_ref[:, OFF_YA + h * HD_A:OFF_YA + (h + 1) * HD_A] = (
            h_new * jax.nn.gelu(z_ref[:, OFF_GA + h * HD_A:OFF_GA + (h + 1) * HD_A])).astype(BF16)

    ub = z_ref[:, OFF_XB:OFF_XB + W_B] * jax.nn.sigmoid(z_ref[:, OFF_GB:OFF_GB + W_B])
    acc = cbw_ref[K_B - 1:K_B, :] * ub
    for k in range(K_B - 1):
        acc = acc + cbw_ref[k:k + 1, :] * cb_ref[k]
    for k in range(K_B - 2):
        cb_out_ref[k] = cb_ref[k + 1]
    cb_out_ref[K_B - 2] = ub
    y_b = _layer_norm(acc, lbg_ref[...], lbb_ref[...])
    mix_ref[:, OFF_YB:OFF_YB + W_B] = jax.nn.silu(y_b).astype(BF16)

    g_c = jax.nn.gelu(z_ref[:, OFF_ZC:OFF_ZC + 2 * W_C])
    v_n = _layer_norm(g_c[:, W_C:], sg_ref[...], sb_ref[...])
    vn_out_ref[...] = v_n
    mix_ref[:, OFF_YC:OFF_YC + W_C] = (g_c[:, :W_C] * (sw0_ref[...] * v_n + sb0_ref[...])).astype(BF16)


def _mixer_sample(z, ca_t, h0, cb_t, p, layer, tb):
    nb = z.shape[0]

    def per_layer(shape):
        return pl.BlockSpec((None,) + shape, lambda b: (layer,) + (0,) * len(shape))

    return pl.pallas_call(
        _mixer_sample_kernel,
        grid=(nb // tb,),
        in_specs=[
            pl.BlockSpec((tb, D_IN), lambda b: (b, 0)),
            pl.BlockSpec((None, K_A - 1, tb, W_A), lambda b: (layer, 0, b, 0)),
            pl.BlockSpec((None, tb, W_A), lambda b: (layer, b, 0)),
            pl.BlockSpec((None, K_B - 1, tb, W_B), lambda b: (layer, 0, b, 0)),
            per_layer((K_A, W_A)), per_layer((1, W_A)), per_layer((H_A, HD_A, 2 * HD_A)),
            per_layer((1, W_A)), per_layer((1, W_A)), per_layer((1, W_A)),
            per_layer((K_B, W_B)), per_layer((1, W_B)), per_layer((1, W_B)),
            per_layer((1, W_C)), per_layer((1, W_C)), per_layer((1, W_C)), per_layer((1, W_C)),
        ],
        out_specs=[
            pl.BlockSpec((tb, D_MODEL), lambda b: (b, 0)),
            pl.BlockSpec((K_A - 1, tb, W_A), lambda b: (0, b, 0)),
            pl.BlockSpec((tb, W_A), lambda b: (b, 0)),
            pl.BlockSpec((K_B - 1, tb, W_B), lambda b: (0, b, 0)),
            pl.BlockSpec((tb, W_C), lambda b: (b, 0)),
        ],
        out_shape=[
            jax.ShapeDtypeStruct((nb, D_MODEL), BF16),
            jax.ShapeDtypeStruct((K_A - 1, nb, W_A), F32),
            jax.ShapeDtypeStruct((nb, W_A), F32),
            jax.ShapeDtypeStruct((K_B - 1, nb, W_B), F32),
            jax.ShapeDtypeStruct((nb, W_C), F32),
        ],
        compiler_params=pltpu.CompilerParams(
            dimension_semantics=("arbitrary",),
            vmem_limit_bytes=_vmem_limit(
                [tb * D_IN * 4, 2 * (K_A - 1) * tb * W_A * 4, 2 * tb * W_A * 4,
                 2 * (K_B - 1) * tb * W_B * 4, tb * D_MODEL * 2, tb * W_C * 4, 1024 * 1024],
                0, 8 * tb * W_A * 4),
        ),
        name="mixer_sample",
    )(z, ca_t, h0, cb_t, p["conv_a_w"], p["conv_a_b"], p["gate_w"], p["gate_r_b"], p["gate_i_b"],
      p["lru_lambda"], p["conv_b_w"], p["ln_b_g"], p["ln_b_b"], p["sgu_ln_g"], p["sgu_ln_b"],
      p["sgu_w00"], p["sgu_b0"])


def kernel(x_prompt, x_sample, state_conv_a, state_lru_h, state_conv_b, norm_mix, w_in, conv_a_w,
           conv_a_b, gate_r_w, gate_r_b, gate_i_w, gate_i_b, lru_lambda, conv_b_w, ln_b_g, ln_b_b,
           sgu_ln_g, sgu_ln_b, sgu_w, sgu_b, w_out, norm_ffn, w_ff1, w_ff2, norm_final):
    depth = w_in.shape[0]
    nb, seq, _ = x_prompt.shape
    ns = x_sample.shape[0]

    def row(v):
        return v[:, None, :]

    p = {
        "conv_a_w": conv_a_w, "conv_a_b": row(conv_a_b),
        "gate_w": jnp.concatenate([gate_r_w, gate_i_w], axis=-1).astype(BF16),
        "gate_r_b": row(gate_r_b), "gate_i_b": row(gate_i_b), "lru_lambda": row(lru_lambda),
        "conv_b_w": conv_b_w, "ln_b_g": row(ln_b_g), "ln_b_b": row(ln_b_b),
        "sgu_ln_g": row(sgu_ln_g), "sgu_ln_b": row(sgu_ln_b),
        "sgu_w": sgu_w, "sgu_b_t": jnp.swapaxes(sgu_b, 1, 2),
        "sgu_w00": row(jnp.repeat(sgu_w[:, :, 0, 0], HD_C, axis=-1)),
        "sgu_b0": row(jnp.repeat(sgu_b[:, :, 0], HD_C, axis=-1)),
    }
    g_mix, g_ffn, g_final = row(norm_mix), row(norm_ffn), norm_final[None, :]
    w_in_b, w_out_b = w_in.astype(BF16), w_out.astype(BF16)
    w_ff1_b, w_ff2_b = w_ff1.astype(BF16), w_ff2.astype(BF16)

    ca_t = jnp.swapaxes(state_conv_a, 1, 2)
    cb_t = jnp.swapaxes(state_conv_b, 1, 2)

    xp = x_prompt.reshape(nb * seq, D_MODEL)
    xs = x_sample.reshape(ns, D_MODEL)
    ca_p, h_p, cb_p, ca_s, h_s, cb_s, v_s = [], [], [], [], [], [], []
    for l in range(depth):
        last = g_final if l == depth - 1 else None
        z = _in_proj(xp, g_mix, w_in_b, l, tm=1024, tn=1024).reshape(nb, seq, D_IN)
        mix, h_l, cb_l = _mixer_prompt(z, p, l, tl=256)
        ca_p.append(z[:, seq - (K_A - 1):, OFF_XA:OFF_XA + W_A])
        h_p.append(h_l[:, 0, :])
        cb_p.append(cb_l[:, B_TAIL - (K_B - 1):, :])
        xp = _out_proj(xp, mix.reshape(nb * seq, D_MODEL), w_out_b, l, tm=1024, tn=1024)
        xp = _ffn(xp, g_ffn, w_ff1_b, w_ff2_b, l, tm=1024, tf=512, g_final=last)
        zs = _in_proj(xs, g_mix, w_in_b, l, tm=ns, tn=1024)
        mix_s, ca_l, hs_l, cbs_l, v_l = _mixer_sample(zs, ca_t, state_lru_h, cb_t, p, l, tb=32)
        ca_s.append(jnp.swapaxes(ca_l, 0, 1))
        h_s.append(hs_l)
        cb_s.append(jnp.swapaxes(cbs_l, 0, 1))
        v_s.append(v_l[:, None, :])
        xs = _out_proj(xs, mix_s, w_out_b, l, tm=ns, tn=1024)
        xs = _ffn(xs, g_ffn, w_ff1_b, w_ff2_b, l, tm=ns, tf=1024, g_final=last)

    return (xp.reshape(nb, seq, D_MODEL), xs.reshape(ns, 1, D_MODEL),
            jnp.stack(ca_p), jnp.stack(h_p), jnp.stack(cb_p),
            jnp.stack(ca_s), jnp.stack(h_s), jnp.stack(cb_s), jnp.stack(v_s))
```

```python
import functools

import jax
import jax.numpy as jnp
from jax import lax
from jax.experimental import pallas as pl
from jax.experimental.pallas import tpu as pltpu

F32 = jnp.float32
BF16 = jnp.bfloat16

D_MODEL = 2048
W_A = 1024
H_A = 8
HD_A = W_A // H_A
K_A = 4
LRU_C = 8.0
W_B = 512
K_B = 31
W_C = 512
H_C = 4
HD_C = W_C // H_C
CHUNK = 128
D_IN = 2 * (W_A + W_B + W_C)
D_FF = 4 * D_MODEL
EPS = 1e-6

OFF_XA, OFF_GA, OFF_XB, OFF_GB, OFF_ZC = 0, W_A, 2 * W_A, 2 * W_A + W_B, 2 * W_A + 2 * W_B
OFF_YA, OFF_YB, OFF_YC = 0, W_A, W_A + W_B

SUBLANES = 8
LANES = 128
A_TAIL = SUBLANES
B_TAIL = 32
V7X_VMEM_BYTES = 64 * 1024 * 1024


def _vmem_limit(block_bytes, scratch_bytes, temp_bytes):
    need = 2 * sum(block_bytes) + scratch_bytes + temp_bytes
    return int(min(need + need // 8, V7X_VMEM_BYTES - 4 * 1024 * 1024))


def _rms(x, g):
    return (x * lax.rsqrt(jnp.mean(x * x, axis=-1, keepdims=True) + EPS)) * g


def _layer_norm(x, g, b):
    xc = x - jnp.mean(x, axis=-1, keepdims=True)
    y = xc * lax.rsqrt(jnp.mean(xc * xc, axis=-1, keepdims=True) + EPS)
    return y * g + b


def _softplus(x):
    return jnp.maximum(x, 0.0) + jnp.log1p(jnp.exp(-jnp.abs(x)))


def _lru_gates(conv_h, pre, b_r, b_i, sp):
    r = jax.nn.sigmoid(pre[:, :HD_A] + b_r)
    i = jax.nn.sigmoid(pre[:, HD_A:] + b_i)
    log_a = (-LRU_C * r) * sp
    a = jnp.exp(log_a)
    t = jnp.tanh(log_a)
    u = jnp.sqrt(-2.0 * t / (1.0 - t)) * (i * conv_h)
    return a, u


def _in_proj_kernel(x_ref, g_ref, w_ref, z_ref, hn_ref):
    @pl.when(pl.program_id(1) == 0)
    def _():
        hn_ref[...] = _rms(x_ref[...], g_ref[...]).astype(BF16)

    z_ref[...] = jnp.dot(hn_ref[...], w_ref[...], preferred_element_type=F32)


def _in_proj(x, g, w, layer, tm, tn):
    m = x.shape[0]
    return pl.pallas_call(
        _in_proj_kernel,
        grid=(m // tm, D_IN // tn),
        in_specs=[
            pl.BlockSpec((tm, D_MODEL), lambda i, j: (i, 0)),
            pl.BlockSpec((None, 1, D_MODEL), lambda i, j: (layer, 0, 0)),
            pl.BlockSpec((None, D_MODEL, tn), lambda i, j: (layer, 0, j)),
        ],
        out_specs=pl.BlockSpec((tm, tn), lambda i, j: (i, j)),
        out_shape=jax.ShapeDtypeStruct((m, D_IN), F32),
        scratch_shapes=[pltpu.VMEM((tm, D_MODEL), BF16)],
        compiler_params=pltpu.CompilerParams(
            dimension_semantics=("arbitrary", "arbitrary"),
            vmem_limit_bytes=_vmem_limit(
                [tm * D_MODEL * 4, D_MODEL * 4, D_MODEL * tn * 2, tm * tn * 4],
                tm * D_MODEL * 2, tm * D_MODEL * 4 + tm * tn * 4),
        ),
        name="in_proj",
    )(x, g, w)


def _out_proj_kernel(x_ref, mix_ref, w_ref, o_ref):
    o_ref[...] = x_ref[...] + jnp.dot(mix_ref[...], w_ref[...], preferred_element_type=F32)


def _out_proj(x, mix, w, layer, tm, tn):
    m = x.shape[0]
    return pl.pallas_call(
        _out_proj_kernel,
        grid=(m // tm, D_MODEL // tn),
        in_specs=[
            pl.BlockSpec((tm, tn), lambda i, j: (i, j)),
            pl.BlockSpec((tm, D_MODEL), lambda i, j: (i, 0)),
            pl.BlockSpec((None, D_MODEL, tn), lambda i, j: (layer, 0, j)),
        ],
        out_specs=pl.BlockSpec((tm, tn), lambda i, j: (i, j)),
        out_shape=jax.ShapeDtypeStruct((m, D_MODEL), F32),
        compiler_params=pltpu.CompilerParams(
            dimension_semantics=("arbitrary", "arbitrary"),
            vmem_limit_bytes=_vmem_limit(
                [tm * tn * 4, tm * D_MODEL * 2, D_MODEL * tn * 2, tm * tn * 4],
                0, tm * tn * 4),
        ),
        name="out_proj",
    )(x, mix, w)


def _ffn_kernel(*refs, final_norm):
    if final_norm:
        x_ref, g_ref, w1_ref, w2_ref, gf_ref, o_ref, hf_ref = refs
    else:
        x_ref, g_ref, w1_ref, w2_ref, o_ref, hf_ref = refs
    j = pl.program_id(1)

    @pl.when(j == 0)
    def _():
        x = x_ref[...]
        hf_ref[...] = _rms(x, g_ref[...]).astype(BF16)
        o_ref[...] = x

    h = jnp.dot(hf_ref[...], w1_ref[...], preferred_element_type=F32)
    h = jnp.square(jnp.maximum(h, 0.0)).astype(BF16)
    o_ref[...] += jnp.dot(h, w2_ref[...], preferred_element_type=F32)

    if final_norm:
        @pl.when(j == pl.num_programs(1) - 1)
        def _():
            o_ref[...] = _rms(o_ref[...], gf_ref[...])


def _ffn(x, g, w1, w2, layer, tm, tf, g_final=None):
    m = x.shape[0]
    final_norm = g_final is not None
    in_specs = [
        pl.BlockSpec((tm, D_MODEL), lambda i, j: (i, 0), pipeline_mode=pl.Buffered(1)),
        pl.BlockSpec((None, 1, D_MODEL), lambda i, j: (layer, 0, 0)),
        pl.BlockSpec((None, D_MODEL, tf), lambda i, j: (layer, 0, j)),
        pl.BlockSpec((None, tf, D_MODEL), lambda i, j: (layer, j, 0)),
    ]
    args = [x, g, w1, w2]
    if final_norm:
        in_specs.append(pl.BlockSpec((1, D_MODEL), lambda i, j: (0, 0)))
        args.append(g_final)
    return pl.pallas_call(
        functools.partial(_ffn_kernel, final_norm=final_norm),
        grid=(m // tm, D_FF // tf),
        in_specs=in_specs,
        out_specs=pl.BlockSpec((tm, D_MODEL), lambda i, j: (i, 0)),
        out_shape=jax.ShapeDtypeStruct((m, D_MODEL), F32),
        scratch_shapes=[pltpu.VMEM((tm, D_MODEL), BF16)],
        compiler_params=pltpu.CompilerParams(
            dimension_semantics=("arbitrary", "arbitrary"),
            vmem_limit_bytes=_vmem_limit(
                [D_MODEL * 4, D_MODEL * tf * 2, tf * D_MODEL * 2, tm * D_MODEL * 4],
                tm * D_MODEL * 4 + tm * D_MODEL * 2, tm * tf * 6 + tm * D_MODEL * 4),
        ),
        name="ffn",
    )(*args)


def _conv_b_tile(w_ref, buf, r0, c0, rows):
    first = B_TAIL - (K_B - 1)
    cs = slice(c0, c0 + LANES)
    y = None
    for s in range(SUBLANES):
        n = rows if s == 0 else rows + SUBLANES
        part = None
        for kp in range(s, B_TAIL + 1, SUBLANES):
            if kp < first:
                continue
            start = r0 + kp - s
            term = w_ref[kp - first:kp - first + 1, cs] * buf[start:start + n, cs]
            part = term if part is None else part + term
        y = part if s == 0 else y + pltpu.roll(part, n - s, axis=0)[0:rows, :]
    return y


N_MIXER_PARAMS = 13
N_MIXER_SCRATCH = 10
PROJ_COLS = 512
CONV_B_ROWS = 64


def _mixer_pieces(z_ref, out_ref, r_out, prm, scr, tl):
    (caw_ref, cab_ref, gw_ref, br_ref, bi_ref, _, cbw_ref, lbg_ref, lbb_ref,
     sg_ref, sb_ref, _, sbt_ref) = prm
    xa_buf, ub_buf, a_s, u_s, _, g_s, _, vn_s, wm_s, sp_s = scr
    pieces = []

    def c_norm(r):
        v = jax.nn.gelu(z_ref[r:r + CHUNK, OFF_ZC + W_C:OFF_ZC + 2 * W_C])
        vn_s[r:r + CHUNK, :] = _layer_norm(v, sg_ref[...], sb_ref[...]).astype(BF16)

    def c_head(r, h):
        cs = slice(h * HD_C, (h + 1) * HD_C)
        mixed = jnp.dot(wm_s[h], vn_s[r:r + CHUNK, cs], preferred_element_type=F32)
        mixed = mixed + sbt_ref[:, h:h + 1]
        u_c = jax.nn.gelu(z_ref[r:r + CHUNK, OFF_ZC + h * HD_C:OFF_ZC + (h + 1) * HD_C])
        out_ref[r_out + r:r_out + r + CHUNK, OFF_YC + h * HD_C:OFF_YC + (h + 1) * HD_C] = (
            u_c * mixed).astype(BF16)

    def b_rows(r):
        n = CONV_B_ROWS
        ub_buf[B_TAIL + r:B_TAIL + r + n, :] = (
            z_ref[r:r + n, OFF_XB:OFF_XB + W_B] * jax.nn.sigmoid(z_ref[r:r + n, OFF_GB:OFF_GB + W_B]))
        cols = [_conv_b_tile(cbw_ref, ub_buf, r, c0, n) for c0 in range(0, W_B, LANES)]
        y_b = _layer_norm(jnp.concatenate(cols, axis=-1), lbg_ref[...], lbb_ref[...])
        out_ref[r_out + r:r_out + r + n, OFF_YB:OFF_YB + W_B] = jax.nn.silu(y_b).astype(BF16)

    def a_head(h):
        cs = slice(h * HD_A, (h + 1) * HD_A)
        xa_buf[A_TAIL:A_TAIL + tl, cs] = z_ref[:, OFF_XA + h * HD_A:OFF_XA + (h + 1) * HD_A]
        xa_full = xa_buf[0:A_TAIL + tl, cs]
        conv_h = cab_ref[:, cs] + caw_ref[K_A - 1:K_A, cs] * xa_full[A_TAIL:, :]
        for k in range(K_A - 1):
            shifted = pltpu.roll(xa_full, K_A - 1 - k, axis=0)[A_TAIL:, :]
            conv_h = conv_h + caw_ref[k:k + 1, cs] * shifted
        pre = jnp.dot(conv_h.astype(BF16), gw_ref[h], preferred_element_type=F32)
        a, u = _lru_gates(conv_h, pre, br_ref[:, cs], bi_ref[:, cs], sp_s[:, cs])
        a_s[:, cs] = a
        u_s[:, cs] = u
        g_s[:, cs] = jax.nn.gelu(z_ref[:, OFF_GA + h * HD_A:OFF_GA + (h + 1) * HD_A])

    for r in range(0, tl, CHUNK):
        pieces.append((1500, functools.partial(c_norm, r)))
        for h in range(H_C):
            pieces.append((500, functools.partial(c_head, r, h)))
    for h in range(H_A):
        pieces.append((2000 * tl // 256, functools.partial(a_head, h)))
        if h % 2 == 1:
            for r in range((h // 2) * tl // 4, (h // 2 + 1) * tl // 4, CONV_B_ROWS):
                pieces.append((3500, functools.partial(b_rows, r)))
    return pieces


def _front_kernel(x_ref, g_ref, w_ref, *rest, tl, chunks_per_seq):
    prm = rest[:N_MIXER_PARAMS]
    lam_ref, sw_ref = prm[5], prm[11]
    mix_ref, ca_out_ref, h_out_ref, cb_out_ref = rest[N_MIXER_PARAMS:N_MIXER_PARAMS + 4]
    z_even, z_odd, held, hn_s = rest[N_MIXER_PARAMS + 4:N_MIXER_PARAMS + 8]
    scr = rest[N_MIXER_PARAMS + 8:]
    xa_buf, ub_buf, a_s, u_s, h_s, g_s, carry, _, wm_s, sp_s = scr
    s = pl.program_id(0)

    def fresh_sequence():
        xa_buf[0:A_TAIL, :] = jnp.zeros((A_TAIL, W_A), F32)
        ub_buf[0:B_TAIL, :] = jnp.zeros((B_TAIL, W_B), F32)
        carry[...] = jnp.zeros((1, W_A), F32)

    def project_and_mix(x_r0, z_next, z_cur, out_ref, r_out):
        hn_s[...] = _rms(x_ref[x_r0:x_r0 + tl, :], g_ref[...]).astype(BF16)
        pieces = _mixer_pieces(z_cur, out_ref, r_out, prm, scr, tl)
        total = sum(cost for cost, _ in pieces)
        n_proj = D_IN // PROJ_COLS
        done, issued = 0, 0
        for cost, piece in pieces:
            while issued < n_proj and issued * total <= done * n_proj:
                c0 = issued * PROJ_COLS
                z_next[:, c0:c0 + PROJ_COLS] = jnp.dot(
                    hn_s[...], w_ref[:, c0:c0 + PROJ_COLS], preferred_element_type=F32)
                issued += 1
            piece()
            done += cost
        assert issued == n_proj
        xa_buf[0:A_TAIL, :] = xa_buf[tl:tl + A_TAIL, :]
        ub_buf[0:B_TAIL, :] = ub_buf[tl:tl + B_TAIL, :]

        def scan_row(t, h):
            h = a_s[pl.ds(t, 1), :] * h + u_s[pl.ds(t, 1), :]
            h_s[pl.ds(t, 1), :] = h
            return h

        carry[...] = lax.fori_loop(0, tl, scan_row, carry[...], unroll=8)
        out_ref[r_out:r_out + tl, OFF_YA:OFF_YA + W_A] = (h_s[...] * g_s[...]).astype(BF16)

    @pl.when(s == 0)
    def _():
        z_odd[...] = jnp.zeros(z_odd.shape, F32)
        held[...] = jnp.zeros(held.shape, BF16)
        fresh_sequence()
        causal = (lax.broadcasted_iota(jnp.int32, (CHUNK, CHUNK), 0)
                  >= lax.broadcasted_iota(jnp.int32, (CHUNK, CHUNK), 1))
        for h in range(H_C):
            wm_s[h] = jnp.where(causal, sw_ref[h], 0.0).astype(BF16)
        sp_s[...] = _softplus(-lam_ref[...])

    mix_ref[0:tl, :] = held[...]
    project_and_mix(0, z_even, z_odd, mix_ref, tl)
    ca_out_ref[...] = xa_buf[0:A_TAIL, :]
    h_out_ref[...] = carry[...]
    cb_out_ref[...] = ub_buf[0:B_TAIL, :]

    @pl.when(s % (chunks_per_seq // 2) == 0)
    def _():
        fresh_sequence()

    project_and_mix(tl, z_odd, z_even, held, 0)


def _front(x, g, w, p, layer, nb, seq, tl):
    chunks_per_seq = seq // tl
    assert seq % tl == 0 and chunks_per_seq % 2 == 0 and tl % CHUNK == 0
    pairs = nb * chunks_per_seq // 2
    pairs_per_seq = chunks_per_seq // 2

    def per_layer(shape):
        return pl.BlockSpec((None,) + shape, lambda s: (layer,) + (0,) * len(shape))

    def seq_of(s):
        return jnp.maximum(s - 1, 0) // pairs_per_seq

    scratch = [
        pltpu.VMEM((tl, D_IN), F32), pltpu.VMEM((tl, D_IN), F32),
        pltpu.VMEM((tl, D_MODEL), BF16), pltpu.VMEM((tl, D_MODEL), BF16),
        pltpu.VMEM((A_TAIL + tl, W_A), F32), pltpu.VMEM((B_TAIL + tl, W_B), F32),
        pltpu.VMEM((tl, W_A), F32), pltpu.VMEM((tl, W_A), F32), pltpu.VMEM((tl, W_A), F32),
        pltpu.VMEM((tl, W_A), F32), pltpu.VMEM((1, W_A), F32),
        pltpu.VMEM((tl, W_C), BF16), pltpu.VMEM((H_C, CHUNK, CHUNK), BF16), pltpu.VMEM((1, W_A), F32),
    ]
    assert len(scratch) == 4 + N_MIXER_SCRATCH
    scratch_bytes = (2 * tl * D_IN * 4 + 2 * tl * D_MODEL * 2 + (A_TAIL + 5 * tl) * W_A * 4
                     + (B_TAIL + tl) * W_B * 4 + tl * W_C * 2 + H_C * CHUNK * CHUNK * 2)
    return pl.pallas_call(
        functools.partial(_front_kernel, tl=tl, chunks_per_seq=chunks_per_seq),
        grid=(pairs + 1,),
        in_specs=[
            pl.BlockSpec((2 * tl, D_MODEL), lambda s: (jnp.minimum(s, pairs - 1), 0)),
            pl.BlockSpec((None, 1, D_MODEL), lambda s: (layer, 0, 0)),
            pl.BlockSpec((None, D_MODEL, D_IN), lambda s: (layer, 0, 0), pipeline_mode=pl.Buffered(1)),
            per_layer((K_A, W_A)), per_layer((1, W_A)), per_layer((H_A, HD_A, 2 * HD_A)),
            per_layer((1, W_A)), per_layer((1, W_A)), per_layer((1, W_A)),
            per_layer((K_B, W_B)), per_layer((1, W_B)), per_layer((1, W_B)),
            per_layer((1, W_C)), per_layer((1, W_C)), per_layer((H_C, CHUNK, CHUNK)),
            per_layer((CHUNK, H_C)),
        ],
        out_specs=[
            pl.BlockSpec((2 * tl, D_MODEL), lambda s: (jnp.maximum(s - 1, 0), 0)),
            pl.BlockSpec((None, A_TAIL, W_A), lambda s: (seq_of(s), 0, 0)),
            pl.BlockSpec((None, 1, W_A), lambda s: (seq_of(s), 0, 0)),
            pl.BlockSpec((None, B_TAIL, W_B), lambda s: (seq_of(s), 0, 0)),
        ],
        out_shape=[
            jax.ShapeDtypeStruct((nb * seq, D_MODEL), BF16),
            jax.ShapeDtypeStruct((nb, A_TAIL, W_A), F32),
            jax.ShapeDtypeStruct((nb, 1, W_A), F32),
            jax.ShapeDtypeStruct((nb, B_TAIL, W_B), F32),
        ],
        scratch_shapes=scratch,
        compiler_params=pltpu.CompilerParams(
            dimension_semantics=("arbitrary",),
            vmem_limit_bytes=_vmem_limit(
                [2 * tl * D_MODEL * 4, 2 * tl * D_MODEL * 2, 1024 * 1024],
                D_MODEL * D_IN * 2 + scratch_bytes, tl * D_IN * 4 + 4 * tl * W_A * 4),
        ),
        name="front",
    )(x, g, w, p["conv_a_w"], p["conv_a_b"], p["gate_w"], p["gate_r_b"], p["gate_i_b"],
      p["lru_lambda"], p["conv_b_w"], p["ln_b_g"], p["ln_b_b"], p["sgu_ln_g"], p["sgu_ln_b"],
      p["sgu_w"], p["sgu_b_t"])


def _mixer_sample_kernel(
        z_ref, ca_ref, h0_ref, cb_ref, caw_ref, cab_ref, gw_ref, br_ref, bi_ref, lam_ref, cbw_ref,
        lbg_ref, lbb_ref, sg_ref, sb_ref, sw0_ref, sb0_ref,
        mix_ref, ca_out_ref, h_out_ref, cb_out_ref, vn_out_ref):
    xa = z_ref[:, OFF_XA:OFF_XA + W_A]
    for k in range(K_A - 2):
        ca_out_ref[k] = ca_ref[k + 1]
    ca_out_ref[K_A - 2] = xa
    sp = _softplus(-lam_ref[...])
    for h in range(H_A):
        cs = slice(h * HD_A, (h + 1) * HD_A)
        conv_h = cab_ref[:, cs] + caw_ref[K_A - 1:K_A, cs] * xa[:, cs]
        for k in range(K_A - 1):
            conv_h = conv_h + caw_ref[k:k + 1, cs] * ca_ref[k, :, cs]
        pre = jnp.dot(conv_h.astype(BF16), gw_ref[h], preferred_element_type=F32)
        a, u = _lru_gates(conv_h, pre, br_ref[:, cs], bi_ref[:, cs], sp[:, cs])
        h_new = a * h0_ref[:, cs] + u
        h_out_ref[:, cs] = h_new
        mix_ref[:, OFF_YA + h * HD_A:OFF_YA + (h + 1) * HD_A] = (
            h_new * jax.nn.gelu(z_ref[:, OFF_GA + h * HD_A:OFF_GA + (h + 1) * HD_A])).astype(BF16)

    ub = z_ref[:, OFF_XB:OFF_XB + W_B] * jax.nn.sigmoid(z_ref[:, OFF_GB:OFF_GB + W_B])
    acc = cbw_ref[K_B - 1:K_B, :] * ub
    for k in range(K_B - 1):
        acc = acc + cbw_ref[k:k + 1, :] * cb_ref[k]
    for k in range(K_B - 2):
        cb_out_ref[k] = cb_ref[k + 1]
    cb_out_ref[K_B - 2] = ub
    y_b = _layer_norm(acc, lbg_ref[...], lbb_ref[...])
    mix_ref[:, OFF_YB:OFF_YB + W_B] = jax.nn.silu(y_b).astype(BF16)

    g_c = jax.nn.gelu(z_ref[:, OFF_ZC:OFF_ZC + 2 * W_C])
    v_n = _layer_norm(g_c[:, W_C:], sg_ref[...], sb_ref[...])
    vn_out_ref[...] = v_n
    mix_ref[:, OFF_YC:OFF_YC + W_C] = (g_c[:, :W_C] * (sw0_ref[...] * v_n + sb0_ref[...])).astype(BF16)


def _mixer_sample(z, ca_t, h0, cb_t, p, layer, tb):
    nb = z.shape[0]

    def per_layer(shape):
        return pl.BlockSpec((None,) + shape, lambda b: (layer,) + (0,) * len(shape))

    return pl.pallas_call(
        _mixer_sample_kernel,
        grid=(nb // tb,),
        in_specs=[
            pl.BlockSpec((tb, D_IN), lambda b: (b, 0)),
            pl.BlockSpec((None, K_A - 1, tb, W_A), lambda b: (layer, 0, b, 0)),
            pl.BlockSpec((None, tb, W_A), lambda b: (layer, b, 0)),
            pl.BlockSpec((None, K_B - 1, tb, W_B), lambda b: (layer, 0, b, 0)),
            per_layer((K_A, W_A)), per_layer((1, W_A)), per_layer((H_A, HD_A, 2 * HD_A)),
            per_layer((1, W_A)), per_layer((1, W_A)), per_layer((1, W_A)),
            per_layer((K_B, W_B)), per_layer((1, W_B)), per_layer((1, W_B)),
            per_layer((1, W_C)), per_layer((1, W_C)), per_layer((1, W_C)), per_layer((1, W_C)),
        ],
        out_specs=[
            pl.BlockSpec((tb, D_MODEL), lambda b: (b, 0)),
            pl.BlockSpec((K_A - 1, tb, W_A), lambda b: (0, b, 0)),
            pl.BlockSpec((tb, W_A), lambda b: (b, 0)),
            pl.BlockSpec((K_B - 1, tb, W_B), lambda b: (0, b, 0)),
            pl.BlockSpec((tb, W_C), lambda b: (b, 0)),
        ],
        out_shape=[
            jax.ShapeDtypeStruct((nb, D_MODEL), BF16),
            jax.ShapeDtypeStruct((K_A - 1, nb, W_A), F32),
            jax.ShapeDtypeStruct((nb, W_A), F32),
            jax.ShapeDtypeStruct((K_B - 1, nb, W_B), F32),
            jax.ShapeDtypeStruct((nb, W_C), F32),
        ],
        compiler_params=pltpu.CompilerParams(
            dimension_semantics=("arbitrary",),
            vmem_limit_bytes=_vmem_limit(
                [tb * D_IN * 4, 2 * (K_A - 1) * tb * W_A * 4, 2 * tb * W_A * 4,
                 2 * (K_B - 1) * tb * W_B * 4, tb * D_MODEL * 2, tb * W_C * 4, 1024 * 1024],
                0, 8 * tb * W_A * 4),
        ),
        name="mixer_sample",
    )(z, ca_t, h0, cb_t, p["conv_a_w"], p["conv_a_b"], p["gate_w"], p["gate_r_b"], p["gate_i_b"],
      p["lru_lambda"], p["conv_b_w"], p["ln_b_g"], p["ln_b_b"], p["sgu_ln_g"], p["sgu_ln_b"],
      p["sgu_w00"], p["sgu_b0"])


def kernel(x_prompt, x_sample, state_conv_a, state_lru_h, state_conv_b, norm_mix, w_in, conv_a_w,
           conv_a_b, gate_r_w, gate_r_b, gate_i_w, gate_i_b, lru_lambda, conv_b_w, ln_b_g, ln_b_b,
           sgu_ln_g, sgu_ln_b, sgu_w, sgu_b, w_out, norm_ffn, w_ff1, w_ff2, norm_final):
    depth = w_in.shape[0]
    nb, seq, _ = x_prompt.shape
    ns = x_sample.shape[0]

    def row(v):
        return v[:, None, :]

    p = {
        "conv_a_w": conv_a_w, "conv_a_b": row(conv_a_b),
        "gate_w": jnp.concatenate([gate_r_w, gate_i_w], axis=-1).astype(BF16),
        "gate_r_b": row(gate_r_b), "gate_i_b": row(gate_i_b), "lru_lambda": row(lru_lambda),
        "conv_b_w": conv_b_w, "ln_b_g": row(ln_b_g), "ln_b_b": row(ln_b_b),
        "sgu_ln_g": row(sgu_ln_g), "sgu_ln_b": row(sgu_ln_b),
        "sgu_w": sgu_w, "sgu_b_t": jnp.swapaxes(sgu_b, 1, 2),
        "sgu_w00": row(jnp.repeat(sgu_w[:, :, 0, 0], HD_C, axis=-1)),
        "sgu_b0": row(jnp.repeat(sgu_b[:, :, 0], HD_C, axis=-1)),
    }
    g_mix, g_ffn, g_final = row(norm_mix), row(norm_ffn), norm_final[None, :]
    w_in_b, w_out_b = w_in.astype(BF16), w_out.astype(BF16)
    w_ff1_b, w_ff2_b = w_ff1.astype(BF16), w_ff2.astype(BF16)

    ca_t = jnp.swapaxes(state_conv_a, 1, 2)
    cb_t = jnp.swapaxes(state_conv_b, 1, 2)

    xp = x_prompt.reshape(nb * seq, D_MODEL)
    xs = x_sample.reshape(ns, D_MODEL)
    ca_p, h_p, cb_p, ca_s, h_s, cb_s, v_s = [], [], [], [], [], [], []
    for l in range(depth):
        last = g_final if l == depth - 1 else None
        mix, ca_l, h_l, cb_l = _front(xp, g_mix, w_in_b, p, l, nb, seq, tl=256)
        ca_p.append(ca_l[:, A_TAIL - (K_A - 1):, :])
        h_p.append(h_l[:, 0, :])
        cb_p.append(cb_l[:, B_TAIL - (K_B - 1):, :])
        xp = _out_proj(xp, mix, w_out_b, l, tm=1024, tn=1024)
        xp = _ffn(xp, g_ffn, w_ff1_b, w_ff2_b, l, tm=1024, tf=512, g_final=last)
        zs = _in_proj(xs, g_mix, w_in_b, l, tm=ns, tn=1024)
        mix_s, ca_l, hs_l, cbs_l, v_l = _mixer_sample(zs, ca_t, state_lru_h, cb_t, p, l, tb=32)
        ca_s.append(jnp.swapaxes(ca_l, 0, 1))
        h_s.append(hs_l)
        cb_s.append(jnp.swapaxes(cbs_l, 0, 1))
        v_s.append(v_l[:, None, :])
        xs = _out_proj(xs, mix_s, w_out_b, l, tm=ns, tn=1024)
        xs = _ffn(xs, g_ffn, w_ff1_b, w_ff2_b, l, tm=ns, tf=1024, g_final=last)

    return (xp.reshape(nb, seq, D_MODEL), xs.reshape(ns, 1, D_MODEL),
            jnp.stack(ca_p), jnp.stack(h_p), jnp.stack(cb_p),
            jnp.stack(ca_s), jnp.stack(h_s), jnp.stack(cb_s), jnp.stack(v_s))
```

```python
import functools

import jax
import jax.numpy as jnp
from jax import lax
from jax.experimental import pallas as pl
from jax.experimental.pallas import tpu as pltpu

F32 = jnp.float32
BF16 = jnp.bfloat16

D_MODEL = 2048
W_A = 1024
H_A = 8
HD_A = W_A // H_A
K_A = 4
LRU_C = 8.0
W_B = 512
K_B = 31
W_C = 512
H_C = 4
HD_C = W_C // H_C
CHUNK = 128
D_IN = 2 * (W_A + W_B + W_C)
D_FF = 4 * D_MODEL
EPS = 1e-6

OFF_XA, OFF_GA, OFF_XB, OFF_GB, OFF_ZC = 0, W_A, 2 * W_A, 2 * W_A + W_B, 2 * W_A + 2 * W_B
OFF_YA, OFF_YB, OFF_YC = 0, W_A, W_A + W_B

SUBLANES = 8
LANES = 128
A_TAIL = SUBLANES
B_TAIL = 32
V7X_VMEM_BYTES = 64 * 1024 * 1024


def _vmem_limit(block_bytes, scratch_bytes, temp_bytes):
    need = 2 * sum(block_bytes) + scratch_bytes + temp_bytes
    return int(min(need + need // 8, V7X_VMEM_BYTES - 4 * 1024 * 1024))


def _rms(x, g):
    return (x * lax.rsqrt(jnp.mean(x * x, axis=-1, keepdims=True) + EPS)) * g


def _layer_norm(x, g, b):
    xc = x - jnp.mean(x, axis=-1, keepdims=True)
    y = xc * lax.rsqrt(jnp.mean(xc * xc, axis=-1, keepdims=True) + EPS)
    return y * g + b


def _softplus(x):
    return jnp.maximum(x, 0.0) + jnp.log1p(jnp.exp(-jnp.abs(x)))


def _lru_gates(conv_h, pre, b_r, b_i, sp):
    r = jax.nn.sigmoid(pre[:, :HD_A] + b_r)
    i = jax.nn.sigmoid(pre[:, HD_A:] + b_i)
    log_a = (-LRU_C * r) * sp
    a = jnp.exp(log_a)
    t = jnp.tanh(log_a)
    u = jnp.sqrt(-2.0 * t / (1.0 - t)) * (i * conv_h)
    return a, u


def _interleave(pieces, fillers):
    total = sum(cost for cost, _ in pieces)
    done, issued = 0, 0
    for cost, piece in pieces:
        while issued < len(fillers) and fillers[issued][1] * total <= done:
            fillers[issued][0]()
            issued += 1
        piece()
        done += cost
    for filler, _ in fillers[issued:]:
        filler()


def _ffn_kernel(*refs, final_norm):
    if final_norm:
        xp_ref, xs_ref, g_ref, w1_ref, w2_ref, gf_ref, op_ref, os_ref, hf_ref = refs
    else:
        xp_ref, xs_ref, g_ref, w1_ref, w2_ref, op_ref, os_ref, hf_ref = refs
    i, j = pl.program_id(0), pl.program_id(1)
    tm = xp_ref.shape[0]
    last = pl.num_programs(1) - 1

    @pl.when(j == 0)
    def _():
        x = xp_ref[...]
        hf_ref[0:tm, :] = _rms(x, g_ref[...]).astype(BF16)
        op_ref[...] = x

    @pl.when((j == 0) & (i == 0))
    def _():
        x = xs_ref[...]
        hf_ref[tm:, :] = _rms(x, g_ref[...]).astype(BF16)
        os_ref[...] = x

    def mlp_block(rows, outs):
        h = jnp.dot(hf_ref[rows, :], w1_ref[...].astype(BF16), preferred_element_type=F32)
        h = jnp.square(jnp.maximum(h, 0.0)).astype(BF16)
        for c0 in range(0, D_MODEL, PROJ_COLS):
            cs = slice(c0, c0 + PROJ_COLS)
            acc = jnp.dot(h, w2_ref[:, cs].astype(BF16), preferred_element_type=F32)
            for o_ref, rs in outs:
                o_ref[:, cs] += acc[rs, :]

    @pl.when(i == 0)
    def _():
        mlp_block(slice(None), [(op_ref, slice(0, tm)), (os_ref, slice(tm, None))])

    @pl.when(i != 0)
    def _():
        mlp_block(slice(0, tm), [(op_ref, slice(None))])

    if final_norm:
        @pl.when(j == last)
        def _():
            op_ref[...] = _rms(op_ref[...], gf_ref[...])

        @pl.when((j == last) & (i == 0))
        def _():
            os_ref[...] = _rms(os_ref[...], gf_ref[...])


def _ffn(xp, xs, g, w1, w2, layer, tm, tf, g_final=None):
    m, ns = xp.shape[0], xs.shape[0]
    final_norm = g_final is not None
    in_specs = [
        pl.BlockSpec((tm, D_MODEL), lambda i, j: (i, 0), pipeline_mode=pl.Buffered(1)),
        pl.BlockSpec((ns, D_MODEL), lambda i, j: (0, 0)),
        pl.BlockSpec((None, 1, D_MODEL), lambda i, j: (layer, 0, 0)),
        pl.BlockSpec((None, D_MODEL, tf), lambda i, j: (layer, 0, j)),
        pl.BlockSpec((None, tf, D_MODEL), lambda i, j: (layer, j, 0)),
    ]
    args = [xp, xs, g, w1, w2]
    if final_norm:
        in_specs.append(pl.BlockSpec((1, D_MODEL), lambda i, j: (0, 0)))
        args.append(g_final)
    return pl.pallas_call(
        functools.partial(_ffn_kernel, final_norm=final_norm),
        grid=(m // tm, D_FF // tf),
        in_specs=in_specs,
        out_specs=[pl.BlockSpec((tm, D_MODEL), lambda i, j: (i, 0)),
                   pl.BlockSpec((ns, D_MODEL), lambda i, j: (0, 0))],
        out_shape=[jax.ShapeDtypeStruct((m, D_MODEL), F32), jax.ShapeDtypeStruct((ns, D_MODEL), F32)],
        scratch_shapes=[pltpu.VMEM((tm + ns, D_MODEL), BF16)],
        compiler_params=pltpu.CompilerParams(
            dimension_semantics=("arbitrary", "arbitrary"),
            vmem_limit_bytes=_vmem_limit(
                [D_MODEL * 4, D_MODEL * tf * 4, tf * D_MODEL * 4, tm * D_MODEL * 4, 2 * ns * D_MODEL * 4],
                tm * D_MODEL * 4 + (tm + ns) * D_MODEL * 2,
                tm * tf * 6 + tm * PROJ_COLS * 4 + 2 * D_MODEL * tf * 2),
        ),
        name="ffn",
    )(*args)


N_FRONT_PARAMS = 10
N_FRONT_SCRATCH = 9
PROJ_COLS = 512


def _front_pieces(z_ref, outs, r_out, prm, scr, tl):
    caw_ref, cab_ref, gw_ref, br_ref, bi_ref, _, sg_ref, sb_ref, _, sbt_ref = prm
    xa_buf, a_s, u_s, _, g_s, _, vn_s, wm_s, sp_s = scr
    _, yc_ref, ub_ref = outs
    pieces = []

    def c_norm(r):
        v = jax.nn.gelu(z_ref[r:r + CHUNK, OFF_ZC + W_C:OFF_ZC + 2 * W_C])
        vn_s[r:r + CHUNK, :] = _layer_norm(v, sg_ref[...], sb_ref[...]).astype(BF16)

    def c_head(r, h):
        cs = slice(h * HD_C, (h + 1) * HD_C)
        mixed = jnp.dot(wm_s[h], vn_s[r:r + CHUNK, cs], preferred_element_type=F32)
        mixed = mixed + sbt_ref[:, h:h + 1]
        u_c = jax.nn.gelu(z_ref[r:r + CHUNK, OFF_ZC + h * HD_C:OFF_ZC + (h + 1) * HD_C])
        yc_ref[r_out + r:r_out + r + CHUNK, cs] = (u_c * mixed).astype(BF16)

    def glu(r):
        ub_ref[r_out + r:r_out + r + CHUNK, :] = (
            z_ref[r:r + CHUNK, OFF_XB:OFF_XB + W_B]
            * jax.nn.sigmoid(z_ref[r:r + CHUNK, OFF_GB:OFF_GB + W_B]))

    def a_head(h):
        cs = slice(h * HD_A, (h + 1) * HD_A)
        xa_buf[A_TAIL:A_TAIL + tl, cs] = z_ref[:, OFF_XA + h * HD_A:OFF_XA + (h + 1) * HD_A]
        xa_full = xa_buf[0:A_TAIL + tl, cs]
        conv_h = cab_ref[:, cs] + caw_ref[K_A - 1:K_A, cs] * xa_full[A_TAIL:, :]
        for k in range(K_A - 1):
            shifted = pltpu.roll(xa_full, K_A - 1 - k, axis=0)[A_TAIL:, :]
            conv_h = conv_h + caw_ref[k:k + 1, cs] * shifted
        pre = jnp.dot(conv_h.astype(BF16), gw_ref[h], preferred_element_type=F32)
        a, u = _lru_gates(conv_h, pre, br_ref[:, cs], bi_ref[:, cs], sp_s[:, cs])
        a_s[:, cs] = a
        u_s[:, cs] = u
        g_s[:, cs] = jax.nn.gelu(z_ref[:, OFF_GA + h * HD_A:OFF_GA + (h + 1) * HD_A])

    for c, r in enumerate(range(0, tl, CHUNK)):
        pieces.append((1500, functools.partial(c_norm, r)))
        for h in range(H_C):
            pieces.append((500, functools.partial(c_head, r, h)))
        pieces.append((600, functools.partial(glu, r)))
    for h in range(H_A):
        pieces.append((2000 * tl // 256, functools.partial(a_head, h)))
    return pieces


def _front_kernel(x_ref, x_next_ref, xs_ref, g_ref, w_ref, *rest, tl, chunks_per_seq):
    prm = rest[:N_FRONT_PARAMS]
    lam_ref, sw_ref = prm[5], prm[8]
    ya_ref, yc_ref, ub_ref, ca_out_ref, h_out_ref, zs_ref = rest[N_FRONT_PARAMS:N_FRONT_PARAMS + 6]
    z_even, z_odd, held_a, held_c, held_u, hn_even, hn_odd = rest[N_FRONT_PARAMS + 6:N_FRONT_PARAMS + 13]
    scr = rest[N_FRONT_PARAMS + 13:]
    xa_buf, a_s, u_s, h_s, g_s, carry, _, wm_s, sp_s = scr
    s = pl.program_id(0)

    def fresh_sequence():
        xa_buf[0:A_TAIL, :] = jnp.zeros((A_TAIL, W_A), F32)
        carry[...] = jnp.zeros((1, W_A), F32)

    def normalise(x_rows_ref, r0, n, hn_ref, r_dst):
        hn_ref[r_dst:r_dst + n, :] = _rms(x_rows_ref[r0:r0 + n, :], g_ref[...]).astype(BF16)

    def project_and_mix(hn_cur, z_next, hn_nxt, x_nxt_ref, x_nxt_r0, z_cur, outs, r_out):
        def project(c0):
            z_next[:, c0:c0 + PROJ_COLS] = jnp.dot(
                hn_cur[...], w_ref[:, c0:c0 + PROJ_COLS], preferred_element_type=F32)

        pieces = _front_pieces(z_cur, outs, r_out, prm, scr, tl)
        for k, r in enumerate(range(0, tl, tl // 2)):
            pieces.insert((k + 1) * len(pieces) // 3, (
                1200, functools.partial(normalise, x_nxt_ref, x_nxt_r0 + r, tl // 2, hn_nxt, r)))
        _interleave(pieces,
                    [(functools.partial(project, c0), c0 / D_IN) for c0 in range(0, D_IN, PROJ_COLS)])
        xa_buf[0:A_TAIL, :] = xa_buf[tl:tl + A_TAIL, :]

        def scan_row(t, h):
            h = a_s[pl.ds(t, 1), :] * h + u_s[pl.ds(t, 1), :]
            h_s[pl.ds(t, 1), :] = h
            return h

        carry[...] = lax.fori_loop(0, tl, scan_row, carry[...], unroll=8)
        outs[0][r_out:r_out + tl, :] = (h_s[...] * g_s[...]).astype(BF16)

    @pl.when(s == 0)
    def _():
        z_odd[...] = jnp.zeros(z_odd.shape, F32)
        held_a[...] = jnp.zeros(held_a.shape, BF16)
        held_c[...] = jnp.zeros(held_c.shape, BF16)
        held_u[...] = jnp.zeros(held_u.shape, F32)
        fresh_sequence()
        causal = (lax.broadcasted_iota(jnp.int32, (CHUNK, CHUNK), 0)
                  >= lax.broadcasted_iota(jnp.int32, (CHUNK, CHUNK), 1))
        for h in range(H_C):
            wm_s[h] = jnp.where(causal, sw_ref[h], 0.0).astype(BF16)
        sp_s[...] = _softplus(-lam_ref[...])
        zs_ref[...] = jnp.dot(_rms(xs_ref[...], g_ref[...]).astype(BF16), w_ref[...],
                              preferred_element_type=F32)
        normalise(x_ref, 0, tl, hn_even, 0)

    ya_ref[0:tl, :] = held_a[...]
    yc_ref[0:tl, :] = held_c[...]
    ub_ref[0:tl, :] = held_u[...]
    project_and_mix(hn_even, z_even, hn_odd, x_ref, tl, z_odd, (ya_ref, yc_ref, ub_ref), tl)
    ca_out_ref[...] = xa_buf[0:A_TAIL, :]
    h_out_ref[...] = carry[...]

    @pl.when(s % (chunks_per_seq // 2) == 0)
    def _():
        fresh_sequence()

    project_and_mix(hn_odd, z_odd, hn_even, x_next_ref, 0, z_even, (held_a, held_c, held_u), 0)


def _front(x, xs, g, w, p, layer, nb, seq, tl):
    ns = xs.shape[0]
    chunks_per_seq = seq // tl
    assert seq % tl == 0 and chunks_per_seq % 2 == 0 and tl % CHUNK == 0
    pairs = nb * chunks_per_seq // 2
    pairs_per_seq = chunks_per_seq // 2
    rows = nb * seq

    def per_layer(shape):
        return pl.BlockSpec((None,) + shape, lambda s: (layer,) + (0,) * len(shape))

    def lagged(width):
        return pl.BlockSpec((2 * tl, width), lambda s: (jnp.maximum(s - 1, 0), 0))

    def seq_of(s):
        return jnp.maximum(s - 1, 0) // pairs_per_seq

    scratch = [
        pltpu.VMEM((tl, D_IN), F32), pltpu.VMEM((tl, D_IN), F32),
        pltpu.VMEM((tl, W_A), BF16), pltpu.VMEM((tl, W_C), BF16), pltpu.VMEM((tl, W_B), F32),
        pltpu.VMEM((tl, D_MODEL), BF16), pltpu.VMEM((tl, D_MODEL), BF16),
        pltpu.VMEM((A_TAIL + tl, W_A), F32),
        pltpu.VMEM((tl, W_A), F32), pltpu.VMEM((tl, W_A), F32), pltpu.VMEM((tl, W_A), F32),
        pltpu.VMEM((tl, W_A), F32), pltpu.VMEM((1, W_A), F32),
        pltpu.VMEM((tl, W_C), BF16), pltpu.VMEM((H_C, CHUNK, CHUNK), BF16), pltpu.VMEM((1, W_A), F32),
    ]
    assert len(scratch) == 7 + N_FRONT_SCRATCH
    scratch_bytes = (2 * tl * D_IN * 4 + tl * (W_A + W_C) * 2 + tl * W_B * 4 + 2 * tl * D_MODEL * 2
                     + (A_TAIL + 5 * tl) * W_A * 4 + tl * W_C * 2 + H_C * CHUNK * CHUNK * 2)
    return pl.pallas_call(
        functools.partial(_front_kernel, tl=tl, chunks_per_seq=chunks_per_seq),
        grid=(pairs + 1,),
        in_specs=[
            pl.BlockSpec((2 * tl, D_MODEL), lambda s: (jnp.minimum(s, pairs - 1), 0)),
            pl.BlockSpec((tl, D_MODEL), lambda s: (jnp.minimum(2 * s + 2, 2 * pairs - 1), 0)),
            pl.BlockSpec((ns, D_MODEL), lambda s: (0, 0)),
            pl.BlockSpec((None, 1, D_MODEL), lambda s: (layer, 0, 0)),
            pl.BlockSpec((None, D_MODEL, D_IN), lambda s: (layer, 0, 0), pipeline_mode=pl.Buffered(1)),
            per_layer((K_A, W_A)), per_layer((1, W_A)), per_layer((H_A, HD_A, 2 * HD_A)),
            per_layer((1, W_A)), per_layer((1, W_A)), per_layer((1, W_A)),
            per_layer((1, W_C)), per_layer((1, W_C)), per_layer((H_C, CHUNK, CHUNK)),
            per_layer((CHUNK, H_C)),
        ],
        out_specs=[
            lagged(W_A), lagged(W_C), lagged(W_B),
            pl.BlockSpec((None, A_TAIL, W_A), lambda s: (seq_of(s), 0, 0)),
            pl.BlockSpec((None, 1, W_A), lambda s: (seq_of(s), 0, 0)),
            pl.BlockSpec((ns, D_IN), lambda s: (0, 0)),
        ],
        out_shape=[
            jax.ShapeDtypeStruct((rows, W_A), BF16),
            jax.ShapeDtypeStruct((rows, W_C), BF16),
            jax.ShapeDtypeStruct((rows, W_B), F32),
            jax.ShapeDtypeStruct((nb, A_TAIL, W_A), F32),
            jax.ShapeDtypeStruct((nb, 1, W_A), F32),
            jax.ShapeDtypeStruct((ns, D_IN), F32),
        ],
        scratch_shapes=scratch,
        compiler_params=pltpu.CompilerParams(
            dimension_semantics=("arbitrary",),
            vmem_limit_bytes=_vmem_limit(
                [3 * tl * D_MODEL * 4, 2 * tl * (W_A + W_C) * 2, 2 * tl * W_B * 4, 1024 * 1024,
                 ns * D_MODEL * 4, ns * D_IN * 4],
                D_MODEL * D_IN * 2 + scratch_bytes, tl * D_IN * 4 + 4 * tl * W_A * 4),
        ),
        name="front",
    )(x, x, xs, g, w, p["conv_a_w"], p["conv_a_b"], p["gate_w"], p["gate_r_b"], p["gate_i_b"],
      p["lru_lambda"], p["sgu_ln_g"], p["sgu_ln_b"], p["sgu_w"], p["sgu_b_t"])


CONV_B_ROWS = 64
NORM_B_ROWS = 32
BACK_ROWS = 256
HEAD_ROWS = B_TAIL + CONV_B_ROWS


def _conv_b_tile(w_ref, buf, r0, c0, rows, buf_row0=0):
    first = B_TAIL - (K_B - 1)
    cs = slice(c0, c0 + LANES)
    y = None
    for s in range(SUBLANES):
        n = rows if s == 0 else rows + SUBLANES
        part = None
        for kp in range(s, B_TAIL + 1, SUBLANES):
            if kp < first:
                continue
            start = r0 + kp - s - buf_row0
            term = w_ref[kp - first:kp - first + 1, cs] * buf[start:start + n, cs]
            part = term if part is None else part + term
        y = part if s == 0 else y + pltpu.roll(part, n - s, axis=0)[0:rows, :]
    return y


def _back_kernel(x_ref, ya_ref, yc_ref, ub_ref, ub_prev_ref, xs_ref, mixs_ref, w32_ref, cbw_ref, lbg_ref,
                 lbb_ref, o_ref, os_ref, head_buf, conv_s, yb_s, w_ref, *, tm, tiles_per_seq):
    i = pl.program_id(0)

    @pl.when(i == 0)
    def _():
        for c0 in range(0, D_MODEL, PROJ_COLS):
            w_ref[:, c0:c0 + PROJ_COLS] = w32_ref[:, c0:c0 + PROJ_COLS].astype(BF16)
        os_ref[...] = xs_ref[...] + jnp.dot(mixs_ref[...], w_ref[...], preferred_element_type=F32)

    head_buf[0:B_TAIL, :] = jnp.where(i % tiles_per_seq == 0, 0.0, ub_prev_ref[...])
    head_buf[B_TAIL:, :] = ub_ref[0:HEAD_ROWS - B_TAIL, :]

    def conv_rows(r):
        n = CONV_B_ROWS
        for c0 in range(0, W_B, LANES):
            if r == 0:
                conv_s[r:r + n, c0:c0 + LANES] = _conv_b_tile(cbw_ref, head_buf, r, c0, n)
            else:
                conv_s[r:r + n, c0:c0 + LANES] = _conv_b_tile(cbw_ref, ub_ref, r, c0, n, B_TAIL)
        for r1 in range(r, r + n, NORM_B_ROWS):
            y_b = _layer_norm(conv_s[r1:r1 + NORM_B_ROWS, :], lbg_ref[...], lbb_ref[...])
            yb_s[r1:r1 + NORM_B_ROWS, :] = jax.nn.silu(y_b).astype(BF16)

    def project(r, c0):
        rs, cs = slice(r, r + BACK_ROWS), slice(c0, c0 + PROJ_COLS)
        o_ref[rs, cs] = (
            x_ref[rs, cs]
            + jnp.dot(ya_ref[rs, :], w_ref[OFF_YA:OFF_YA + W_A, cs], preferred_element_type=F32)
            + jnp.dot(yb_s[rs, :], w_ref[OFF_YB:OFF_YB + W_B, cs], preferred_element_type=F32)
            + jnp.dot(yc_ref[rs, :], w_ref[OFF_YC:OFF_YC + W_C, cs], preferred_element_type=F32))

    fillers = [(functools.partial(project, r, c0), (r + BACK_ROWS) / tm)
               for r in range(0, tm, BACK_ROWS) for c0 in range(0, D_MODEL, PROJ_COLS)]
    _interleave([(CONV_B_ROWS, functools.partial(conv_rows, r)) for r in range(0, tm, CONV_B_ROWS)],
                fillers)


def _back(x, ya, yc, ub, xs, mix_s, w, p, layer, seq, tm):
    rows, ns = x.shape[0], xs.shape[0]
    assert seq % tm == 0 and tm % B_TAIL == 0 and tm % CONV_B_ROWS == 0 and tm % BACK_ROWS == 0
    tails_per_tile = tm // B_TAIL

    def per_layer(shape):
        return pl.BlockSpec((None,) + shape, lambda i: (layer,) + (0,) * len(shape))

    def tile(width):
        return pl.BlockSpec((tm, width), lambda i: (i, 0))

    def whole(width):
        return pl.BlockSpec((ns, width), lambda i: (0, 0))

    return pl.pallas_call(
        functools.partial(_back_kernel, tm=tm, tiles_per_seq=seq // tm),
        grid=(rows // tm,),
        in_specs=[
            tile(D_MODEL), tile(W_A), tile(W_C), tile(W_B),
            pl.BlockSpec((B_TAIL, W_B), lambda i: (jnp.maximum(i * tails_per_tile - 1, 0), 0)),
            whole(D_MODEL), whole(D_MODEL),
            pl.BlockSpec((None, D_MODEL, D_MODEL), lambda i: (layer, 0, 0), pipeline_mode=pl.Buffered(1)),
            per_layer((K_B, W_B)), per_layer((1, W_B)), per_layer((1, W_B)),
        ],
        out_specs=[tile(D_MODEL), whole(D_MODEL)],
        out_shape=[jax.ShapeDtypeStruct((rows, D_MODEL), F32), jax.ShapeDtypeStruct((ns, D_MODEL), F32)],
        scratch_shapes=[pltpu.VMEM((HEAD_ROWS, W_B), F32), pltpu.VMEM((tm, W_B), F32),
                        pltpu.VMEM((tm, W_B), BF16), pltpu.VMEM((D_MODEL, D_MODEL), BF16)],
        compiler_params=pltpu.CompilerParams(
            dimension_semantics=("arbitrary",),
            vmem_limit_bytes=_vmem_limit(
                [tm * D_MODEL * 4, tm * (W_A + W_C) * 2, tm * W_B * 4, tm * D_MODEL * 4, 256 * 1024,
                 ns * D_MODEL * 10],
                D_MODEL * D_MODEL * (4 + 2) + (HEAD_ROWS + tm) * W_B * 4 + tm * W_B * 2,
                2 * tm * PROJ_COLS * 4),
        ),
        name="back",
    )(x, ya, yc, ub, ub, xs, mix_s, w, p["conv_b_w"], p["ln_b_g"], p["ln_b_b"])


def _mixer_sample_kernel(
        z_ref, ca_ref, h0_ref, cb_ref, caw_ref, cab_ref, gw_ref, br_ref, bi_ref, lam_ref, cbw_ref,
        lbg_ref, lbb_ref, sg_ref, sb_ref, sw0_ref, sb0_ref,
        mix_ref, ca_out_ref, h_out_ref, cb_out_ref, vn_out_ref):
    xa = z_ref[:, OFF_XA:OFF_XA + W_A]
    for k in range(K_A - 2):
        ca_out_ref[k] = ca_ref[k + 1]
    ca_out_ref[K_A - 2] = xa
    sp = _softplus(-lam_ref[...])
    for h in range(H_A):
        cs = slice(h * HD_A, (h + 1) * HD_A)
        conv_h = cab_ref[:, cs] + caw_ref[K_A - 1:K_A, cs] * xa[:, cs]
        for k in range(K_A - 1):
            conv_h = conv_h + caw_ref[k:k + 1, cs] * ca_ref[k, :, cs]
        pre = jnp.dot(conv_h.astype(BF16), gw_ref[h], preferred_element_type=F32)
        a, u = _lru_gates(conv_h, pre, br_ref[:, cs], bi_ref[:, cs], sp[:, cs])
        h_new = a * h0_ref[:, cs] + u
        h_out_ref[:, cs] = h_new
        mix_ref[:, OFF_YA + h * HD_A:OFF_YA + (h + 1) * HD_A] = (
            h_new * jax.nn.gelu(z_ref[:, OFF_GA + h * HD_A:OFF_GA + (h + 1) * HD_A])).astype(BF16)

    ub = z_ref[:, OFF_XB:OFF_XB + W_B] * jax.nn.sigmoid(z_ref[:, OFF_GB:OFF_GB + W_B])
    acc = cbw_ref[K_B - 1:K_B, :] * ub
    for k in range(K_B - 1):
        acc = acc + cbw_ref[k:k + 1, :] * cb_ref[k]
    for k in range(K_B - 2):
        cb_out_ref[k] = cb_ref[k + 1]
    cb_out_ref[K_B - 2] = ub
    y_b = _layer_norm(acc, lbg_ref[...], lbb_ref[...])
    mix_ref[:, OFF_YB:OFF_YB + W_B] = jax.nn.silu(y_b).astype(BF16)

    g_c = jax.nn.gelu(z_ref[:, OFF_ZC:OFF_ZC + 2 * W_C])
    v_n = _layer_norm(g_c[:, W_C:], sg_ref[...], sb_ref[...])
    vn_out_ref[...] = v_n
    mix_ref[:, OFF_YC:OFF_YC + W_C] = (g_c[:, :W_C] * (sw0_ref[...] * v_n + sb0_ref[...])).astype(BF16)


def _mixer_sample(z, ca_t, h0, cb_t, p, layer, tb):
    nb = z.shape[0]

    def per_layer(shape):
        return pl.BlockSpec((None,) + shape, lambda b: (layer,) + (0,) * len(shape))

    return pl.pallas_call(
        _mixer_sample_kernel,
        grid=(nb // tb,),
        in_specs=[
            pl.BlockSpec((tb, D_IN), lambda b: (b, 0)),
            pl.BlockSpec((None, K_A - 1, tb, W_A), lambda b: (layer, 0, b, 0)),
            pl.BlockSpec((None, tb, W_A), lambda b: (layer, b, 0)),
            pl.BlockSpec((None, K_B - 1, tb, W_B), lambda b: (layer, 0, b, 0)),
            per_layer((K_A, W_A)), per_layer((1, W_A)), per_layer((H_A, HD_A, 2 * HD_A)),
            per_layer((1, W_A)), per_layer((1, W_A)), per_layer((1, W_A)),
            per_layer((K_B, W_B)), per_layer((1, W_B)), per_layer((1, W_B)),
            per_layer((1, W_C)), per_layer((1, W_C)), per_layer((1, W_C)), per_layer((1, W_C)),
        ],
        out_specs=[
            pl.BlockSpec((tb, D_MODEL), lambda b: (b, 0)),
            pl.BlockSpec((K_A - 1, tb, W_A), lambda b: (0, b, 0)),
            pl.BlockSpec((tb, W_A), lambda b: (b, 0)),
            pl.BlockSpec((K_B - 1, tb, W_B), lambda b: (0, b, 0)),
            pl.BlockSpec((tb, W_C), lambda b: (b, 0)),
        ],
        out_shape=[
            jax.ShapeDtypeStruct((nb, D_MODEL), BF16),
            jax.ShapeDtypeStruct((K_A - 1, nb, W_A), F32),
            jax.ShapeDtypeStruct((nb, W_A), F32),
            jax.ShapeDtypeStruct((K_B - 1, nb, W_B), F32),
            jax.ShapeDtypeStruct((nb, W_C), F32),
        ],
        compiler_params=pltpu.CompilerParams(
            dimension_semantics=("arbitrary",),
            vmem_limit_bytes=_vmem_limit(
                [tb * D_IN * 4, 2 * (K_A - 1) * tb * W_A * 4, 2 * tb * W_A * 4,
                 2 * (K_B - 1) * tb * W_B * 4, tb * D_MODEL * 2, tb * W_C * 4, 1024 * 1024],
                0, 8 * tb * W_A * 4),
        ),
        name="mixer_sample",
    )(z, ca_t, h0, cb_t, p["conv_a_w"], p["conv_a_b"], p["gate_w"], p["gate_r_b"], p["gate_i_b"],
      p["lru_lambda"], p["conv_b_w"], p["ln_b_g"], p["ln_b_b"], p["sgu_ln_g"], p["sgu_ln_b"],
      p["sgu_w00"], p["sgu_b0"])


def kernel(x_prompt, x_sample, state_conv_a, state_lru_h, state_conv_b, norm_mix, w_in, conv_a_w,
           conv_a_b, gate_r_w, gate_r_b, gate_i_w, gate_i_b, lru_lambda, conv_b_w, ln_b_g, ln_b_b,
           sgu_ln_g, sgu_ln_b, sgu_w, sgu_b, w_out, norm_ffn, w_ff1, w_ff2, norm_final):
    depth = w_in.shape[0]
    nb, seq, _ = x_prompt.shape
    ns = x_sample.shape[0]

    def row(v):
        return v[:, None, :]

    p = {
        "conv_a_w": conv_a_w, "conv_a_b": row(conv_a_b),
        "gate_w": jnp.concatenate([gate_r_w, gate_i_w], axis=-1).astype(BF16),
        "gate_r_b": row(gate_r_b), "gate_i_b": row(gate_i_b), "lru_lambda": row(lru_lambda),
        "conv_b_w": conv_b_w, "ln_b_g": row(ln_b_g), "ln_b_b": row(ln_b_b),
        "sgu_ln_g": row(sgu_ln_g), "sgu_ln_b": row(sgu_ln_b),
        "sgu_w": sgu_w, "sgu_b_t": jnp.swapaxes(sgu_b, 1, 2),
        "sgu_w00": row(jnp.repeat(sgu_w[:, :, 0, 0], HD_C, axis=-1)),
        "sgu_b0": row(jnp.repeat(sgu_b[:, :, 0], HD_C, axis=-1)),
    }
    g_mix, g_ffn, g_final = row(norm_mix), row(norm_ffn), norm_final[None, :]
    w_in_b = w_in.astype(BF16)

    ca_t = jnp.swapaxes(state_conv_a, 1, 2)
    cb_t = jnp.swapaxes(state_conv_b, 1, 2)

    xp = x_prompt.reshape(nb * seq, D_MODEL)
    xs = x_sample.reshape(ns, D_MODEL)
    ca_p, h_p, cb_p, ca_s, h_s, cb_s, v_s = [], [], [], [], [], [], []
    for l in range(depth):
        last = g_final if l == depth - 1 else None
        ya, yc, ub, ca_l, h_l, zs = _front(xp, xs, g_mix, w_in_b, p, l, nb, seq, tl=256)
        ca_p.append(ca_l[:, A_TAIL - (K_A - 1):, :])
        h_p.append(h_l[:, 0, :])
        cb_p.append(ub.reshape(nb, seq, W_B)[:, seq - (K_B - 1):, :])
        mix_s, cas_l, hs_l, cbs_l, v_l = _mixer_sample(zs, ca_t, state_lru_h, cb_t, p, l, tb=32)
        ca_s.append(jnp.swapaxes(cas_l, 0, 1))
        h_s.append(hs_l)
        cb_s.append(jnp.swapaxes(cbs_l, 0, 1))
        v_s.append(v_l[:, None, :])
        xp, xs = _back(xp, ya, yc, ub, xs, mix_s, w_out, p, l, seq, tm=512)
        xp, xs = _ffn(xp, xs, g_ffn, w_ff1, w_ff2, l, tm=1024, tf=512, g_final=last)

    return (xp.reshape(nb, seq, D_MODEL), xs.reshape(ns, 1, D_MODEL),
            jnp.stack(ca_p), jnp.stack(h_p), jnp.stack(cb_p),
            jnp.stack(ca_s), jnp.stack(h_s), jnp.stack(cb_s), jnp.stack(v_s))
```

```python
import functools

import jax
import jax.numpy as jnp
from jax import lax
from jax.experimental import pallas as pl
from jax.experimental.pallas import tpu as pltpu

F32 = jnp.float32
BF16 = jnp.bfloat16

D_MODEL = 2048
W_A = 1024
H_A = 8
HD_A = W_A // H_A
K_A = 4
LRU_C = 8.0
W_B = 512
K_B = 31
W_C = 512
H_C = 4
HD_C = W_C // H_C
CHUNK = 128
D_IN = 2 * (W_A + W_B + W_C)
D_FF = 4 * D_MODEL
EPS = 1e-6

OFF_XA, OFF_GA, OFF_XB, OFF_GB, OFF_ZC = 0, W_A, 2 * W_A, 2 * W_A + W_B, 2 * W_A + 2 * W_B
OFF_YA, OFF_YB, OFF_YC = 0, W_A, W_A + W_B

SUBLANES = 8
LANES = 128
A_TAIL = SUBLANES
B_TAIL = 32
V7X_VMEM_BYTES = 64 * 1024 * 1024


def _vmem_limit(block_bytes, scratch_bytes, temp_bytes):
    need = 2 * sum(block_bytes) + scratch_bytes + temp_bytes
    return int(min(need + need // 8, V7X_VMEM_BYTES - 4 * 1024 * 1024))


def _rms(x, g):
    return (x * lax.rsqrt(jnp.mean(x * x, axis=-1, keepdims=True) + EPS)) * g


def _layer_norm(x, g, b):
    xc = x - jnp.mean(x, axis=-1, keepdims=True)
    y = xc * lax.rsqrt(jnp.mean(xc * xc, axis=-1, keepdims=True) + EPS)
    return y * g + b


def _softplus(x):
    return jnp.maximum(x, 0.0) + jnp.log1p(jnp.exp(-jnp.abs(x)))


def _lru_gates(conv_h, pre, b_r, b_i, sp):
    r = jax.nn.sigmoid(pre[:, :HD_A] + b_r)
    i = jax.nn.sigmoid(pre[:, HD_A:] + b_i)
    log_a = (-LRU_C * r) * sp
    a = jnp.exp(log_a)
    t = jnp.tanh(log_a)
    u = jnp.sqrt(-2.0 * t / (1.0 - t)) * (i * conv_h)
    return a, u


def _interleave(pieces, fillers):
    total = sum(cost for cost, _ in pieces)
    done, issued = 0, 0
    for cost, piece in pieces:
        while issued < len(fillers) and fillers[issued][1] * total <= done:
            fillers[issued][0]()
            issued += 1
        piece()
        done += cost
    for filler, _ in fillers[issued:]:
        filler()


def _ffn_kernel(*refs, final_norm):
    if final_norm:
        xp_ref, xs_ref, g_ref, w1_ref, w2_ref, gf_ref, op_ref, os_ref, hf_ref = refs
    else:
        xp_ref, xs_ref, g_ref, w1_ref, w2_ref, op_ref, os_ref, hf_ref = refs
    i, j = pl.program_id(0), pl.program_id(1)
    tm = xp_ref.shape[0]
    last = pl.num_programs(1) - 1

    @pl.when(j == 0)
    def _():
        x = xp_ref[...]
        hf_ref[0:tm, :] = _rms(x, g_ref[...]).astype(BF16)
        op_ref[...] = x

    @pl.when((j == 0) & (i == 0))
    def _():
        x = xs_ref[...]
        hf_ref[tm:, :] = _rms(x, g_ref[...]).astype(BF16)
        os_ref[...] = x

    def mlp_block(rows, outs):
        h = jnp.dot(hf_ref[rows, :], w1_ref[...].astype(BF16), preferred_element_type=F32)
        h = jnp.square(jnp.maximum(h, 0.0)).astype(BF16)
        for c0 in range(0, D_MODEL, PROJ_COLS):
            cs = slice(c0, c0 + PROJ_COLS)
            acc = jnp.dot(h, w2_ref[:, cs].astype(BF16), preferred_element_type=F32)
            for o_ref, rs in outs:
                o_ref[:, cs] += acc[rs, :]

    @pl.when(i == 0)
    def _():
        mlp_block(slice(None), [(op_ref, slice(0, tm)), (os_ref, slice(tm, None))])

    @pl.when(i != 0)
    def _():
        mlp_block(slice(0, tm), [(op_ref, slice(None))])

    if final_norm:
        @pl.when(j == last)
        def _():
            op_ref[...] = _rms(op_ref[...], gf_ref[...])

        @pl.when((j == last) & (i == 0))
        def _():
            os_ref[...] = _rms(os_ref[...], gf_ref[...])


def _ffn(xp, xs, g, w1, w2, layer, tm, tf, g_final=None):
    m, ns = xp.shape[0], xs.shape[0]
    final_norm = g_final is not None
    in_specs = [
        pl.BlockSpec((tm, D_MODEL), lambda i, j: (i, 0), pipeline_mode=pl.Buffered(1)),
        pl.BlockSpec((ns, D_MODEL), lambda i, j: (0, 0)),
        pl.BlockSpec((None, 1, D_MODEL), lambda i, j: (layer, 0, 0)),
        pl.BlockSpec((None, D_MODEL, tf), lambda i, j: (layer, 0, j)),
        pl.BlockSpec((None, tf, D_MODEL), lambda i, j: (layer, j, 0)),
    ]
    args = [xp, xs, g, w1, w2]
    if final_norm:
        in_specs.append(pl.BlockSpec((1, D_MODEL), lambda i, j: (0, 0)))
        args.append(g_final)
    return pl.pallas_call(
        functools.partial(_ffn_kernel, final_norm=final_norm),
        grid=(m // tm, D_FF // tf),
        in_specs=in_specs,
        out_specs=[pl.BlockSpec((tm, D_MODEL), lambda i, j: (i, 0)),
                   pl.BlockSpec((ns, D_MODEL), lambda i, j: (0, 0))],
        out_shape=[jax.ShapeDtypeStruct((m, D_MODEL), F32), jax.ShapeDtypeStruct((ns, D_MODEL), F32)],
        scratch_shapes=[pltpu.VMEM((tm + ns, D_MODEL), BF16)],
        compiler_params=pltpu.CompilerParams(
            dimension_semantics=("arbitrary", "arbitrary"),
            vmem_limit_bytes=_vmem_limit(
                [D_MODEL * 4, D_MODEL * tf * 4, tf * D_MODEL * 4, tm * D_MODEL * 4, 2 * ns * D_MODEL * 4],
                tm * D_MODEL * 4 + (tm + ns) * D_MODEL * 2,
                tm * tf * 6 + tm * PROJ_COLS * 4 + 2 * D_MODEL * tf * 2),
        ),
        name="ffn",
    )(*args)


N_FRONT_PARAMS = 10
N_FRONT_SCRATCH = 9
PROJ_COLS = 512


def _front_pieces(z_ref, outs, r_out, prm, scr, tl):
    caw_ref, cab_ref, gw_ref, br_ref, bi_ref, _, sg_ref, sb_ref, _, sbt_ref = prm
    xa_buf, a_s, u_s, _, g_s, _, vn_s, wm_s, sp_s = scr
    _, yc_ref, ub_ref = outs
    pieces = []

    def c_norm(r):
        v = jax.nn.gelu(z_ref[r:r + CHUNK, OFF_ZC + W_C:OFF_ZC + 2 * W_C])
        vn_s[r:r + CHUNK, :] = _layer_norm(v, sg_ref[...], sb_ref[...]).astype(BF16)

    def c_head(r, h):
        cs = slice(h * HD_C, (h + 1) * HD_C)
        mixed = jnp.dot(wm_s[h], vn_s[r:r + CHUNK, cs], preferred_element_type=F32)
        mixed = mixed + sbt_ref[:, h:h + 1]
        u_c = jax.nn.gelu(z_ref[r:r + CHUNK, OFF_ZC + h * HD_C:OFF_ZC + (h + 1) * HD_C])
        yc_ref[r_out + r:r_out + r + CHUNK, cs] = (u_c * mixed).astype(BF16)

    def glu(r):
        ub_ref[r_out + r:r_out + r + CHUNK, :] = (
            z_ref[r:r + CHUNK, OFF_XB:OFF_XB + W_B]
            * jax.nn.sigmoid(z_ref[r:r + CHUNK, OFF_GB:OFF_GB + W_B]))

    def a_head(h):
        cs = slice(h * HD_A, (h + 1) * HD_A)
        xa_buf[A_TAIL:A_TAIL + tl, cs] = z_ref[:, OFF_XA + h * HD_A:OFF_XA + (h + 1) * HD_A]
        xa_full = xa_buf[0:A_TAIL + tl, cs]
        conv_h = cab_ref[:, cs] + caw_ref[K_A - 1:K_A, cs] * xa_full[A_TAIL:, :]
        for k in range(K_A - 1):
            shifted = pltpu.roll(xa_full, K_A - 1 - k, axis=0)[A_TAIL:, :]
            conv_h = conv_h + caw_ref[k:k + 1, cs] * shifted
        pre = jnp.dot(conv_h.astype(BF16), gw_ref[h], preferred_element_type=F32)
        a, u = _lru_gates(conv_h, pre, br_ref[:, cs], bi_ref[:, cs], sp_s[:, cs])
        a_s[:, cs] = a
        u_s[:, cs] = u
        g_s[:, cs] = jax.nn.gelu(z_ref[:, OFF_GA + h * HD_A:OFF_GA + (h + 1) * HD_A])

    for c, r in enumerate(range(0, tl, CHUNK)):
        pieces.append((1500, functools.partial(c_norm, r)))
        for h in range(H_C):
            pieces.append((500, functools.partial(c_head, r, h)))
        pieces.append((600, functools.partial(glu, r)))
    for h in range(H_A):
        pieces.append((2000 * tl // 256, functools.partial(a_head, h)))
    return pieces


def _front_kernel(x_ref, x_next_ref, xs_ref, g_ref, w_ref, *rest, tl, chunks_per_seq):
    prm = rest[:N_FRONT_PARAMS]
    lam_ref, sw_ref = prm[5], prm[8]
    ya_ref, yc_ref, ub_ref, ca_out_ref, h_out_ref, zs_ref = rest[N_FRONT_PARAMS:N_FRONT_PARAMS + 6]
    z_even, z_odd, hn_s = rest[N_FRONT_PARAMS + 6:N_FRONT_PARAMS + 9]
    scr = rest[N_FRONT_PARAMS + 9:]
    xa_buf, a_s, u_s, h_s, g_s, carry, _, wm_s, sp_s = scr
    outs = (ya_ref, yc_ref, ub_ref)
    s = pl.program_id(0)

    def normalise(x_rows_ref, r0):
        hn_s[...] = _rms(x_rows_ref[r0:r0 + tl, :], g_ref[...]).astype(BF16)

    def project_and_mix(x_rows_ref, x_r0, z_next, z_cur, r_out):
        normalise(x_rows_ref, x_r0)

        def project(c0):
            z_next[:, c0:c0 + PROJ_COLS] = jnp.dot(
                hn_s[...], w_ref[:, c0:c0 + PROJ_COLS], preferred_element_type=F32)

        _interleave(_front_pieces(z_cur, outs, r_out, prm, scr, tl),
                    [(functools.partial(project, c0), c0 / D_IN) for c0 in range(0, D_IN, PROJ_COLS)])
        xa_buf[0:A_TAIL, :] = xa_buf[tl:tl + A_TAIL, :]

        def scan_row(t, h):
            h = a_s[pl.ds(t, 1), :] * h + u_s[pl.ds(t, 1), :]
            h_s[pl.ds(t, 1), :] = h
            return h

        carry[...] = lax.fori_loop(0, tl, scan_row, carry[...], unroll=8)
        ya_ref[r_out:r_out + tl, :] = (h_s[...] * g_s[...]).astype(BF16)

    @pl.when(s == 0)
    def _():
        causal = (lax.broadcasted_iota(jnp.int32, (CHUNK, CHUNK), 0)
                  >= lax.broadcasted_iota(jnp.int32, (CHUNK, CHUNK), 1))
        for h in range(H_C):
            wm_s[h] = jnp.where(causal, sw_ref[h], 0.0).astype(BF16)
        sp_s[...] = _softplus(-lam_ref[...])
        zs_ref[...] = jnp.dot(_rms(xs_ref[...], g_ref[...]).astype(BF16), w_ref[...],
                              preferred_element_type=F32)
        normalise(x_ref, 0)
        z_even[...] = jnp.dot(hn_s[...], w_ref[...], preferred_element_type=F32)

    @pl.when(s % (chunks_per_seq // 2) == 0)
    def _():
        xa_buf[0:A_TAIL, :] = jnp.zeros((A_TAIL, W_A), F32)
        carry[...] = jnp.zeros((1, W_A), F32)

    project_and_mix(x_ref, tl, z_odd, z_even, 0)
    project_and_mix(x_next_ref, 0, z_even, z_odd, tl)
    ca_out_ref[...] = xa_buf[0:A_TAIL, :]
    h_out_ref[...] = carry[...]


def _front(x, xs, g, w, p, layer, nb, seq, tl):
    ns = xs.shape[0]
    chunks_per_seq = seq // tl
    assert seq % tl == 0 and chunks_per_seq % 2 == 0 and tl % CHUNK == 0
    pairs = nb * chunks_per_seq // 2
    pairs_per_seq = chunks_per_seq // 2
    rows = nb * seq

    def per_layer(shape):
        return pl.BlockSpec((None,) + shape, lambda s: (layer,) + (0,) * len(shape))

    def pair(width):
        return pl.BlockSpec((2 * tl, width), lambda s: (s, 0))

    scratch = [
        pltpu.VMEM((tl, D_IN), F32), pltpu.VMEM((tl, D_IN), F32), pltpu.VMEM((tl, D_MODEL), BF16),
        pltpu.VMEM((A_TAIL + tl, W_A), F32),
        pltpu.VMEM((tl, W_A), F32), pltpu.VMEM((tl, W_A), F32), pltpu.VMEM((tl, W_A), F32),
        pltpu.VMEM((tl, W_A), F32), pltpu.VMEM((1, W_A), F32),
        pltpu.VMEM((tl, W_C), BF16), pltpu.VMEM((H_C, CHUNK, CHUNK), BF16), pltpu.VMEM((1, W_A), F32),
    ]
    assert len(scratch) == 3 + N_FRONT_SCRATCH
    scratch_bytes = (2 * tl * D_IN * 4 + tl * D_MODEL * 2
                     + (A_TAIL + 5 * tl) * W_A * 4 + tl * W_C * 2 + H_C * CHUNK * CHUNK * 2)
    return pl.pallas_call(
        functools.partial(_front_kernel, tl=tl, chunks_per_seq=chunks_per_seq),
        grid=(pairs,),
        in_specs=[
            pair(D_MODEL),
            pl.BlockSpec((tl, D_MODEL), lambda s: (jnp.minimum(2 * s + 2, 2 * pairs - 1), 0)),
            pl.BlockSpec((ns, D_MODEL), lambda s: (0, 0)),
            pl.BlockSpec((None, 1, D_MODEL), lambda s: (layer, 0, 0)),
            pl.BlockSpec((None, D_MODEL, D_IN), lambda s: (layer, 0, 0), pipeline_mode=pl.Buffered(1)),
            per_layer((K_A, W_A)), per_layer((1, W_A)), per_layer((H_A, HD_A, 2 * HD_A)),
            per_layer((1, W_A)), per_layer((1, W_A)), per_layer((1, W_A)),
            per_layer((1, W_C)), per_layer((1, W_C)), per_layer((H_C, CHUNK, CHUNK)),
            per_layer((CHUNK, H_C)),
        ],
        out_specs=[
            pair(W_A), pair(W_C), pair(W_B),
            pl.BlockSpec((None, A_TAIL, W_A), lambda s: (s // pairs_per_seq, 0, 0)),
            pl.BlockSpec((None, 1, W_A), lambda s: (s // pairs_per_seq, 0, 0)),
            pl.BlockSpec((ns, D_IN), lambda s: (0, 0)),
        ],
        out_shape=[
            jax.ShapeDtypeStruct((rows, W_A), BF16),
            jax.ShapeDtypeStruct((rows, W_C), BF16),
            jax.ShapeDtypeStruct((rows, W_B), F32),
            jax.ShapeDtypeStruct((nb, A_TAIL, W_A), F32),
            jax.ShapeDtypeStruct((nb, 1, W_A), F32),
            jax.ShapeDtypeStruct((ns, D_IN), F32),
        ],
        scratch_shapes=scratch,
        compiler_params=pltpu.CompilerParams(
            dimension_semantics=("arbitrary",),
            vmem_limit_bytes=_vmem_limit(
                [3 * tl * D_MODEL * 4, 2 * tl * (W_A + W_C) * 2, 2 * tl * W_B * 4, 1024 * 1024,
                 ns * D_MODEL * 4, ns * D_IN * 4],
                D_MODEL * D_IN * 2 + scratch_bytes, tl * D_IN * 4 + 4 * tl * W_A * 4),
        ),
        name="front",
    )(x, x, xs, g, w, p["conv_a_w"], p["conv_a_b"], p["gate_w"], p["gate_r_b"], p["gate_i_b"],
      p["lru_lambda"], p["sgu_ln_g"], p["sgu_ln_b"], p["sgu_w"], p["sgu_b_t"])


CONV_B_ROWS = 64
NORM_B_ROWS = 32
BACK_ROWS = 256


def _conv_b_tile(w_ref, ub_slab, conv_slab, r0, c, rows):
    first = B_TAIL - (K_B - 1)
    half = rows // 2
    cs = slice(c * LANES, (c + 1) * LANES)
    even = odd = None
    for kp in range(first, first + K_B + 1):
        x = ub_slab[c, pl.ds(r0 + kp, half, stride=2), :]
        if kp < first + K_B:
            term = w_ref[kp - first:kp - first + 1, cs] * x
            even = term if even is None else even + term
        if kp > first:
            term = w_ref[kp - first - 1:kp - first, cs] * x
            odd = term if odd is None else odd + term
    conv_slab[c, pl.ds(r0, half, stride=2), :] = even
    conv_slab[c, pl.ds(r0 + 1, half, stride=2), :] = odd


def _back_kernel(x_ref, ya_ref, yc_ref, ub_ref, ub_prev_ref, xs_ref, mixs_ref, w32_ref, cbw_ref, lbg_ref,
                 lbb_ref, o_ref, os_ref, ub_slab, conv_slab, yb_s, w_ref, *, tm, tiles_per_seq):
    i = pl.program_id(0)

    @pl.when(i == 0)
    def _():
        for c0 in range(0, D_MODEL, PROJ_COLS):
            w_ref[:, c0:c0 + PROJ_COLS] = w32_ref[:, c0:c0 + PROJ_COLS].astype(BF16)
        os_ref[...] = xs_ref[...] + jnp.dot(mixs_ref[...], w_ref[...], preferred_element_type=F32)

    n_slabs = W_B // LANES
    prev = jnp.where(i % tiles_per_seq == 0, 0.0, ub_prev_ref[...])
    for c in range(n_slabs):
        ub_slab[c, 0:B_TAIL, :] = prev[:, c * LANES:(c + 1) * LANES]
        ub_slab[c, B_TAIL:, :] = ub_ref[:, c * LANES:(c + 1) * LANES]

    def conv_rows(r):
        n = CONV_B_ROWS
        for c in range(n_slabs):
            _conv_b_tile(cbw_ref, ub_slab, conv_slab, r, c, n)
        for r1 in range(r, r + n, NORM_B_ROWS):
            conv = jnp.concatenate([conv_slab[c, r1:r1 + NORM_B_ROWS, :] for c in range(n_slabs)], axis=-1)
            y_b = _layer_norm(conv, lbg_ref[...], lbb_ref[...])
            yb_s[r1:r1 + NORM_B_ROWS, :] = jax.nn.silu(y_b).astype(BF16)

    def project(r, c0):
        rs, cs = slice(r, r + BACK_ROWS), slice(c0, c0 + PROJ_COLS)
        o_ref[rs, cs] = (
            x_ref[rs, cs]
            + jnp.dot(ya_ref[rs, :], w_ref[OFF_YA:OFF_YA + W_A, cs], preferred_element_type=F32)
            + jnp.dot(yb_s[rs, :], w_ref[OFF_YB:OFF_YB + W_B, cs], preferred_element_type=F32)
            + jnp.dot(yc_ref[rs, :], w_ref[OFF_YC:OFF_YC + W_C, cs], preferred_element_type=F32))

    fillers = [(functools.partial(project, r, c0), (r + BACK_ROWS) / tm)
               for r in range(0, tm, BACK_ROWS) for c0 in range(0, D_MODEL, PROJ_COLS)]
    _interleave([(CONV_B_ROWS, functools.partial(conv_rows, r)) for r in range(0, tm, CONV_B_ROWS)],
                fillers)


def _back(x, ya, yc, ub, xs, mix_s, w, p, layer, seq, tm):
    rows, ns = x.shape[0], xs.shape[0]
    assert seq % tm == 0 and tm % B_TAIL == 0 and tm % CONV_B_ROWS == 0 and tm % BACK_ROWS == 0
    tails_per_tile = tm // B_TAIL

    def per_layer(shape):
        return pl.BlockSpec((None,) + shape, lambda i: (layer,) + (0,) * len(shape))

    def tile(width):
        return pl.BlockSpec((tm, width), lambda i: (i, 0))

    def whole(width):
        return pl.BlockSpec((ns, width), lambda i: (0, 0))

    return pl.pallas_call(
        functools.partial(_back_kernel, tm=tm, tiles_per_seq=seq // tm),
        grid=(rows // tm,),
        in_specs=[
            tile(D_MODEL), tile(W_A), tile(W_C), tile(W_B),
            pl.BlockSpec((B_TAIL, W_B), lambda i: (jnp.maximum(i * tails_per_tile - 1, 0), 0)),
            whole(D_MODEL), whole(D_MODEL),
            pl.BlockSpec((None, D_MODEL, D_MODEL), lambda i: (layer, 0, 0), pipeline_mode=pl.Buffered(1)),
            per_layer((K_B, W_B)), per_layer((1, W_B)), per_layer((1, W_B)),
        ],
        out_specs=[tile(D_MODEL), whole(D_MODEL)],
        out_shape=[jax.ShapeDtypeStruct((rows, D_MODEL), F32), jax.ShapeDtypeStruct((ns, D_MODEL), F32)],
        scratch_shapes=[pltpu.VMEM((W_B // LANES, B_TAIL + tm, LANES), F32),
                        pltpu.VMEM((W_B // LANES, tm, LANES), F32),
                        pltpu.VMEM((tm, W_B), BF16), pltpu.VMEM((D_MODEL, D_MODEL), BF16)],
        compiler_params=pltpu.CompilerParams(
            dimension_semantics=("arbitrary",),
            vmem_limit_bytes=_vmem_limit(
                [tm * D_MODEL * 4, tm * (W_A + W_C) * 2, tm * W_B * 4, tm * D_MODEL * 4, 256 * 1024,
                 ns * D_MODEL * 10],
                D_MODEL * D_MODEL * (4 + 2) + (B_TAIL + 2 * tm) * W_B * 4 + tm * W_B * 2,
                2 * tm * PROJ_COLS * 4),
        ),
        name="back",
    )(x, ya, yc, ub, ub, xs, mix_s, w, p["conv_b_w"], p["ln_b_g"], p["ln_b_b"])


def _mixer_sample_kernel(
        z_ref, ca_ref, h0_ref, cb_ref, caw_ref, cab_ref, gw_ref, br_ref, bi_ref, lam_ref, cbw_ref,
        lbg_ref, lbb_ref, sg_ref, sb_ref, sw0_ref, sb0_ref,
        mix_ref, ca_out_ref, h_out_ref, cb_out_ref, vn_out_ref):
    xa = z_ref[:, OFF_XA:OFF_XA + W_A]
    for k in range(K_A - 2):
        ca_out_ref[k] = ca_ref[k + 1]
    ca_out_ref[K_A - 2] = xa
    sp = _softplus(-lam_ref[...])
    for h in range(H_A):
        cs = slice(h * HD_A, (h + 1) * HD_A)
        conv_h = cab_ref[:, cs] + caw_ref[K_A - 1:K_A, cs] * xa[:, cs]
        for k in range(K_A - 1):
            conv_h = conv_h + caw_ref[k:k + 1, cs] * ca_ref[k, :, cs]
        pre = jnp.dot(conv_h.astype(BF16), gw_ref[h], preferred_element_type=F32)
        a, u = _lru_gates(conv_h, pre, br_ref[:, cs], bi_ref[:, cs], sp[:, cs])
        h_new = a * h0_ref[:, cs] + u
        h_out_ref[:, cs] = h_new
        mix_ref[:, OFF_YA + h * HD_A:OFF_YA + (h + 1) * HD_A] = (
            h_new * jax.nn.gelu(z_ref[:, OFF_GA + h * HD_A:OFF_GA + (h + 1) * HD_A])).astype(BF16)

    ub = z_ref[:, OFF_XB:OFF_XB + W_B] * jax.nn.sigmoid(z_ref[:, OFF_GB:OFF_GB + W_B])
    acc = cbw_ref[K_B - 1:K_B, :] * ub
    for k in range(K_B - 1):
        acc = acc + cbw_ref[k:k + 1, :] * cb_ref[k]
    for k in range(K_B - 2):
        cb_out_ref[k] = cb_ref[k + 1]
    cb_out_ref[K_B - 2] = ub
    y_b = _layer_norm(acc, lbg_ref[...], lbb_ref[...])
    mix_ref[:, OFF_YB:OFF_YB + W_B] = jax.nn.silu(y_b).astype(BF16)

    g_c = jax.nn.gelu(z_ref[:, OFF_ZC:OFF_ZC + 2 * W_C])
    v_n = _layer_norm(g_c[:, W_C:], sg_ref[...], sb_ref[...])
    vn_out_ref[...] = v_n
    mix_ref[:, OFF_YC:OFF_YC + W_C] = (g_c[:, :W_C] * (sw0_ref[...] * v_n + sb0_ref[...])).astype(BF16)


def _mixer_sample(z, ca_t, h0, cb_t, p, layer, tb):
    nb = z.shape[0]

    def per_layer(shape):
        return pl.BlockSpec((None,) + shape, lambda b: (layer,) + (0,) * len(shape))

    return pl.pallas_call(
        _mixer_sample_kernel,
        grid=(nb // tb,),
        in_specs=[
            pl.BlockSpec((tb, D_IN), lambda b: (b, 0)),
            pl.BlockSpec((None, K_A - 1, tb, W_A), lambda b: (layer, 0, b, 0)),
            pl.BlockSpec((None, tb, W_A), lambda b: (layer, b, 0)),
            pl.BlockSpec((None, K_B - 1, tb, W_B), lambda b: (layer, 0, b, 0)),
            per_layer((K_A, W_A)), per_layer((1, W_A)), per_layer((H_A, HD_A, 2 * HD_A)),
            per_layer((1, W_A)), per_layer((1, W_A)), per_layer((1, W_A)),
            per_layer((K_B, W_B)), per_layer((1, W_B)), per_layer((1, W_B)),
            per_layer((1, W_C)), per_layer((1, W_C)), per_layer((1, W_C)), per_layer((1, W_C)),
        ],
        out_specs=[
            pl.BlockSpec((tb, D_MODEL), lambda b: (b, 0)),
            pl.BlockSpec((K_A - 1, tb, W_A), lambda b: (0, b, 0)),
            pl.BlockSpec((tb, W_A), lambda b: (b, 0)),
            pl.BlockSpec((K_B - 1, tb, W_B), lambda b: (0, b, 0)),
            pl.BlockSpec((tb, W_C), lambda b: (b, 0)),
        ],
        out_shape=[
            jax.ShapeDtypeStruct((nb, D_MODEL), BF16),
            jax.ShapeDtypeStruct((K_A - 1, nb, W_A), F32),
            jax.ShapeDtypeStruct((nb, W_A), F32),
            jax.ShapeDtypeStruct((K_B - 1, nb, W_B), F32),
            jax.ShapeDtypeStruct((nb, W_C), F32),
        ],
        compiler_params=pltpu.CompilerParams(
            dimension_semantics=("arbitrary",),
            vmem_limit_bytes=_vmem_limit(
                [tb * D_IN * 4, 2 * (K_A - 1) * tb * W_A * 4, 2 * tb * W_A * 4,
                 2 * (K_B - 1) * tb * W_B * 4, tb * D_MODEL * 2, tb * W_C * 4, 1024 * 1024],
                0, 8 * tb * W_A * 4),
        ),
        name="mixer_sample",
    )(z, ca_t, h0, cb_t, p["conv_a_w"], p["conv_a_b"], p["gate_w"], p["gate_r_b"], p["gate_i_b"],
      p["lru_lambda"], p["conv_b_w"], p["ln_b_g"], p["ln_b_b"], p["sgu_ln_g"], p["sgu_ln_b"],
      p["sgu_w00"], p["sgu_b0"])


def kernel(x_prompt, x_sample, state_conv_a, state_lru_h, state_conv_b, norm_mix, w_in, conv_a_w,
           conv_a_b, gate_r_w, gate_r_b, gate_i_w, gate_i_b, lru_lambda, conv_b_w, ln_b_g, ln_b_b,
           sgu_ln_g, sgu_ln_b, sgu_w, sgu_b, w_out, norm_ffn, w_ff1, w_ff2, norm_final):
    depth = w_in.shape[0]
    nb, seq, _ = x_prompt.shape
    ns = x_sample.shape[0]

    def row(v):
        return v[:, None, :]

    p = {
        "conv_a_w": conv_a_w, "conv_a_b": row(conv_a_b),
        "gate_w": jnp.concatenate([gate_r_w, gate_i_w], axis=-1).astype(BF16),
        "gate_r_b": row(gate_r_b), "gate_i_b": row(gate_i_b), "lru_lambda": row(lru_lambda),
        "conv_b_w": conv_b_w, "ln_b_g": row(ln_b_g), "ln_b_b": row(ln_b_b),
        "sgu_ln_g": row(sgu_ln_g), "sgu_ln_b": row(sgu_ln_b),
        "sgu_w": sgu_w, "sgu_b_t": jnp.swapaxes(sgu_b, 1, 2),
        "sgu_w00": row(jnp.repeat(sgu_w[:, :, 0, 0], HD_C, axis=-1)),
        "sgu_b0": row(jnp.repeat(sgu_b[:, :, 0], HD_C, axis=-1)),
    }
    g_mix, g_ffn, g_final = row(norm_mix), row(norm_ffn), norm_final[None, :]
    w_in_b = w_in.astype(BF16)

    ca_t = jnp.swapaxes(state_conv_a, 1, 2)
    cb_t = jnp.swapaxes(state_conv_b, 1, 2)

    xp = x_prompt.reshape(nb * seq, D_MODEL)
    xs = x_sample.reshape(ns, D_MODEL)
    ca_p, h_p, cb_p, ca_s, h_s, cb_s, v_s = [], [], [], [], [], [], []
    for l in range(depth):
        last = g_final if l == depth - 1 else None
        ya, yc, ub, ca_l, h_l, zs = _front(xp, xs, g_mix, w_in_b, p, l, nb, seq, tl=256)
        ca_p.append(ca_l[:, A_TAIL - (K_A - 1):, :])
        h_p.append(h_l[:, 0, :])
        cb_p.append(ub.reshape(nb, seq, W_B)[:, seq - (K_B - 1):, :])
        mix_s, cas_l, hs_l, cbs_l, v_l = _mixer_sample(zs, ca_t, state_lru_h, cb_t, p, l, tb=32)
        ca_s.append(jnp.swapaxes(cas_l, 0, 1))
        h_s.append(hs_l)
        cb_s.append(jnp.swapaxes(cbs_l, 0, 1))
        v_s.append(v_l[:, None, :])
        xp, xs = _back(xp, ya, yc, ub, xs, mix_s, w_out, p, l, seq, tm=512)
        xp, xs = _ffn(xp, xs, g_ffn, w_ff1, w_ff2, l, tm=1024, tf=512, g_final=last)

    return (xp.reshape(nb, seq, D_MODEL), xs.reshape(ns, 1, D_MODEL),
            jnp.stack(ca_p), jnp.stack(h_p), jnp.stack(cb_p),
            jnp.stack(ca_s), jnp.stack(h_s), jnp.stack(cb_s), jnp.stack(v_s))
```

```python
import functools

import jax
import jax.numpy as jnp
from jax import lax
from jax.experimental import pallas as pl
from jax.experimental.pallas import tpu as pltpu

F32 = jnp.float32
BF16 = jnp.bfloat16

D_MODEL = 2048
W_A = 1024
H_A = 8
HD_A = W_A // H_A
K_A = 4
LRU_C = 8.0
W_B = 512
K_B = 31
W_C = 512
H_C = 4
HD_C = W_C // H_C
CHUNK = 128
D_IN = 2 * (W_A + W_B + W_C)
D_FF = 4 * D_MODEL
EPS = 1e-6

OFF_XA, OFF_GA, OFF_XB, OFF_GB, OFF_ZC = 0, W_A, 2 * W_A, 2 * W_A + W_B, 2 * W_A + 2 * W_B
OFF_YA, OFF_YB, OFF_YC = 0, W_A, W_A + W_B

SUBLANES = 8
LANES = 128
A_TAIL = SUBLANES
B_TAIL = 32
V7X_VMEM_BYTES = 64 * 1024 * 1024


def _vmem_limit(block_bytes, scratch_bytes, temp_bytes):
    need = 2 * sum(block_bytes) + scratch_bytes + temp_bytes
    return int(min(need + need // 8, V7X_VMEM_BYTES - 4 * 1024 * 1024))


def _rms(x, g):
    return (x * lax.rsqrt(jnp.mean(x * x, axis=-1, keepdims=True) + EPS)) * g


def _layer_norm(x, g, b):
    xc = x - jnp.mean(x, axis=-1, keepdims=True)
    y = xc * lax.rsqrt(jnp.mean(xc * xc, axis=-1, keepdims=True) + EPS)
    return y * g + b


def _softplus(x):
    return jnp.maximum(x, 0.0) + jnp.log1p(jnp.exp(-jnp.abs(x)))


def _lru_gates(conv_h, pre, b_r, b_i, sp):
    r = jax.nn.sigmoid(pre[:, :HD_A] + b_r)
    i = jax.nn.sigmoid(pre[:, HD_A:] + b_i)
    log_a = (-LRU_C * r) * sp
    a = jnp.exp(log_a)
    t = jnp.tanh(log_a)
    u = jnp.sqrt(-2.0 * t / (1.0 - t)) * (i * conv_h)
    return a, u


def _interleave(pieces, fillers):
    total = sum(cost for cost, _ in pieces)
    done, issued = 0, 0
    for cost, piece in pieces:
        while issued < len(fillers) and fillers[issued][1] * total <= done:
            fillers[issued][0]()
            issued += 1
        piece()
        done += cost
    for filler, _ in fillers[issued:]:
        filler()


def _ffn_kernel(*refs, final_norm, with_sample):
    refs = list(refs)
    hf_ref = refs.pop()
    if with_sample:
        xp_ref, xs_ref = refs[0], refs[1]
        os_ref = refs.pop()
    else:
        xp_ref = refs[1]
    op_ref = refs.pop()
    gf_ref = refs.pop() if final_norm else None
    g_ref, w1_ref, w2_ref = refs[2:5]
    j = pl.program_id(1)
    tm = xp_ref.shape[0]
    outs = [(op_ref, slice(0, tm))] + ([(os_ref, slice(tm, None))] if with_sample else [])
    srcs = [xp_ref] + ([xs_ref] if with_sample else [])

    @pl.when(j == 0)
    def _():
        for x_ref, (o_ref, rs) in zip(srcs, outs):
            x = x_ref[...]
            hf_ref[rs, :] = _rms(x, g_ref[...]).astype(BF16)
            o_ref[...] = x

    h = jnp.dot(hf_ref[...], w1_ref[...].astype(BF16), preferred_element_type=F32)
    h = jnp.square(jnp.maximum(h, 0.0)).astype(BF16)
    for c0 in range(0, D_MODEL, PROJ_COLS):
        cs = slice(c0, c0 + PROJ_COLS)
        acc = jnp.dot(h, w2_ref[:, cs].astype(BF16), preferred_element_type=F32)
        for o_ref, rs in outs:
            o_ref[:, cs] += acc[rs, :]

    if final_norm:
        @pl.when(j == pl.num_programs(1) - 1)
        def _():
            for o_ref, _ in outs:
                o_ref[...] = _rms(o_ref[...], gf_ref[...])


def _ffn(xp, xs, g, w1, w2, layer, tm, tf, g_final=None):
    m, ns = xp.shape[0], xs.shape[0]
    final_norm = g_final is not None

    def call(with_sample, n_tiles, lead_args, lead_specs, out_specs, out_shape, aliases):
        rows = tm + (ns if with_sample else 0)
        in_specs = lead_specs + [
            pl.BlockSpec((None, 1, D_MODEL), lambda i, j: (layer, 0, 0)),
            pl.BlockSpec((None, D_MODEL, tf), lambda i, j: (layer, 0, j)),
            pl.BlockSpec((None, tf, D_MODEL), lambda i, j: (layer, j, 0)),
        ]
        args = lead_args + [g, w1, w2]
        if final_norm:
            in_specs.append(pl.BlockSpec((1, D_MODEL), lambda i, j: (0, 0)))
            args.append(g_final)
        return pl.pallas_call(
            functools.partial(_ffn_kernel, final_norm=final_norm, with_sample=with_sample),
            grid=(n_tiles, D_FF // tf),
            in_specs=in_specs,
            out_specs=out_specs,
            out_shape=out_shape,
            input_output_aliases=aliases,
            scratch_shapes=[pltpu.VMEM((rows, D_MODEL), BF16)],
            compiler_params=pltpu.CompilerParams(
                dimension_semantics=("arbitrary", "arbitrary"),
                vmem_limit_bytes=_vmem_limit(
                    [D_MODEL * 4, D_MODEL * tf * 4, tf * D_MODEL * 4, tm * D_MODEL * 4,
                     2 * ns * D_MODEL * 4],
                    tm * D_MODEL * 4 + rows * D_MODEL * 2,
                    rows * tf * 6 + rows * PROJ_COLS * 4 + 2 * D_MODEL * tf * 2),
            ),
            name="ffn_head" if with_sample else "ffn_tail",
        )(*args)

    def x_tile(first_tile):
        return pl.BlockSpec((tm, D_MODEL), lambda i, j: (i + first_tile, 0), pipeline_mode=pl.Buffered(1))

    def o_tile(first_tile):
        return pl.BlockSpec((tm, D_MODEL), lambda i, j: (i + first_tile, 0))

    whole = pl.BlockSpec((ns, D_MODEL), lambda i, j: (0, 0))
    full = jax.ShapeDtypeStruct((m, D_MODEL), F32)
    op, os = call(True, 1, [xp, xs], [x_tile(0), whole], [o_tile(0), whole],
                  [full, jax.ShapeDtypeStruct((ns, D_MODEL), F32)], {})
    op = call(False, m // tm - 1, [op, xp], [pl.BlockSpec(memory_space=pl.ANY), x_tile(1)],
              o_tile(1), full, {0: 0})
    return op, os


N_FRONT_PARAMS = 10
N_FRONT_SCRATCH = 9
PROJ_COLS = 512


def _front_pieces(z_ref, outs, r_out, prm, scr, tl):
    caw_ref, cab_ref, gw_ref, br_ref, bi_ref, _, sg_ref, sb_ref, _, sbt_ref = prm
    xa_buf, a_s, u_s, _, g_s, _, vn_s, wm_s, sp_s = scr
    _, yc_ref, ub_ref = outs
    pieces = []

    def c_norm(r):
        v = jax.nn.gelu(z_ref[r:r + CHUNK, OFF_ZC + W_C:OFF_ZC + 2 * W_C])
        vn_s[r:r + CHUNK, :] = _layer_norm(v, sg_ref[...], sb_ref[...]).astype(BF16)

    def c_head(r, h):
        cs = slice(h * HD_C, (h + 1) * HD_C)
        mixed = jnp.dot(wm_s[h], vn_s[r:r + CHUNK, cs], preferred_element_type=F32)
        mixed = mixed + sbt_ref[:, h:h + 1]
        u_c = jax.nn.gelu(z_ref[r:r + CHUNK, OFF_ZC + h * HD_C:OFF_ZC + (h + 1) * HD_C])
        yc_ref[r_out + r:r_out + r + CHUNK, cs] = (u_c * mixed).astype(BF16)

    def glu(r):
        ub_ref[r_out + r:r_out + r + CHUNK, :] = (
            z_ref[r:r + CHUNK, OFF_XB:OFF_XB + W_B]
            * jax.nn.sigmoid(z_ref[r:r + CHUNK, OFF_GB:OFF_GB + W_B]))

    def a_head(h):
        cs = slice(h * HD_A, (h + 1) * HD_A)
        xa_buf[A_TAIL:A_TAIL + tl, cs] = z_ref[:, OFF_XA + h * HD_A:OFF_XA + (h + 1) * HD_A]
        xa_full = xa_buf[0:A_TAIL + tl, cs]
        conv_h = cab_ref[:, cs] + caw_ref[K_A - 1:K_A, cs] * xa_full[A_TAIL:, :]
        for k in range(K_A - 1):
            shifted = pltpu.roll(xa_full, K_A - 1 - k, axis=0)[A_TAIL:, :]
            conv_h = conv_h + caw_ref[k:k + 1, cs] * shifted
        pre = jnp.dot(conv_h.astype(BF16), gw_ref[h], preferred_element_type=F32)
        a, u = _lru_gates(conv_h, pre, br_ref[:, cs], bi_ref[:, cs], sp_s[:, cs])
        a_s[:, cs] = a
        u_s[:, cs] = u
        g_s[:, cs] = jax.nn.gelu(z_ref[:, OFF_GA + h * HD_A:OFF_GA + (h + 1) * HD_A])

    for c, r in enumerate(range(0, tl, CHUNK)):
        pieces.append((1500, functools.partial(c_norm, r)))
        for h in range(H_C):
            pieces.append((500, functools.partial(c_head, r, h)))
        pieces.append((600, functools.partial(glu, r)))
    for h in range(H_A):
        pieces.append((2000 * tl // 256, functools.partial(a_head, h)))
    return pieces


def _front_kernel(x_ref, x_next_ref, xs_ref, g_ref, w_ref, *rest, tl, chunks_per_seq):
    prm = rest[:N_FRONT_PARAMS]
    lam_ref, sw_ref = prm[5], prm[8]
    ya_ref, yc_ref, ub_ref, ca_out_ref, h_out_ref, zs_ref = rest[N_FRONT_PARAMS:N_FRONT_PARAMS + 6]
    z_even, z_odd, hn_s = rest[N_FRONT_PARAMS + 6:N_FRONT_PARAMS + 9]
    scr = rest[N_FRONT_PARAMS + 9:]
    xa_buf, a_s, u_s, h_s, g_s, carry, _, wm_s, sp_s = scr
    outs = (ya_ref, yc_ref, ub_ref)
    s = pl.program_id(0)

    def normalise(x_rows_ref, r0):
        hn_s[...] = _rms(x_rows_ref[r0:r0 + tl, :], g_ref[...]).astype(BF16)

    def project_and_mix(x_rows_ref, x_r0, z_next, z_cur, r_out):
        normalise(x_rows_ref, x_r0)

        def project(c0):
            z_next[:, c0:c0 + PROJ_COLS] = jnp.dot(
                hn_s[...], w_ref[:, c0:c0 + PROJ_COLS], preferred_element_type=F32)

        _interleave(_front_pieces(z_cur, outs, r_out, prm, scr, tl),
                    [(functools.partial(project, c0), c0 / D_IN) for c0 in range(0, D_IN, PROJ_COLS)])
        xa_buf[0:A_TAIL, :] = xa_buf[tl:tl + A_TAIL, :]

        def scan_row(t, h):
            h = a_s[pl.ds(t, 1), :] * h + u_s[pl.ds(t, 1), :]
            h_s[pl.ds(t, 1), :] = h
            return h

        carry[...] = lax.fori_loop(0, tl, scan_row, carry[...], unroll=8)
        ya_ref[r_out:r_out + tl, :] = (h_s[...] * g_s[...]).astype(BF16)

    @pl.when(s == 0)
    def _():
        causal = (lax.broadcasted_iota(jnp.int32, (CHUNK, CHUNK), 0)
                  >= lax.broadcasted_iota(jnp.int32, (CHUNK, CHUNK), 1))
        for h in range(H_C):
            wm_s[h] = jnp.where(causal, sw_ref[h], 0.0).astype(BF16)
        sp_s[...] = _softplus(-lam_ref[...])
        zs_ref[...] = jnp.dot(_rms(xs_ref[...], g_ref[...]).astype(BF16), w_ref[...],
                              preferred_element_type=F32)
        normalise(x_ref, 0)
        z_even[...] = jnp.dot(hn_s[...], w_ref[...], preferred_element_type=F32)

    @pl.when(s % (chunks_per_seq // 2) == 0)
    def _():
        xa_buf[0:A_TAIL, :] = jnp.zeros((A_TAIL, W_A), F32)
        carry[...] = jnp.zeros((1, W_A), F32)

    project_and_mix(x_ref, tl, z_odd, z_even, 0)
    project_and_mix(x_next_ref, 0, z_even, z_odd, tl)
    ca_out_ref[...] = xa_buf[0:A_TAIL, :]
    h_out_ref[...] = carry[...]


def _front(x, xs, g, w, p, layer, nb, seq, tl):
    ns = xs.shape[0]
    chunks_per_seq = seq // tl
    assert seq % tl == 0 and chunks_per_seq % 2 == 0 and tl % CHUNK == 0
    pairs = nb * chunks_per_seq // 2
    pairs_per_seq = chunks_per_seq // 2
    rows = nb * seq

    def per_layer(shape):
        return pl.BlockSpec((None,) + shape, lambda s: (layer,) + (0,) * len(shape))

    def pair(width):
        return pl.BlockSpec((2 * tl, width), lambda s: (s, 0))

    scratch = [
        pltpu.VMEM((tl, D_IN), F32), pltpu.VMEM((tl, D_IN), F32), pltpu.VMEM((tl, D_MODEL), BF16),
        pltpu.VMEM((A_TAIL + tl, W_A), F32),
        pltpu.VMEM((tl, W_A), F32), pltpu.VMEM((tl, W_A), F32), pltpu.VMEM((tl, W_A), F32),
        pltpu.VMEM((tl, W_A), F32), pltpu.VMEM((1, W_A), F32),
        pltpu.VMEM((tl, W_C), BF16), pltpu.VMEM((H_C, CHUNK, CHUNK), BF16), pltpu.VMEM((1, W_A), F32),
    ]
    assert len(scratch) == 3 + N_FRONT_SCRATCH
    scratch_bytes = (2 * tl * D_IN * 4 + tl * D_MODEL * 2
                     + (A_TAIL + 5 * tl) * W_A * 4 + tl * W_C * 2 + H_C * CHUNK * CHUNK * 2)
    return pl.pallas_call(
        functools.partial(_front_kernel, tl=tl, chunks_per_seq=chunks_per_seq),
        grid=(pairs,),
        in_specs=[
            pair(D_MODEL),
            pl.BlockSpec((tl, D_MODEL), lambda s: (jnp.minimum(2 * s + 2, 2 * pairs - 1), 0)),
            pl.BlockSpec((ns, D_MODEL), lambda s: (0, 0)),
            pl.BlockSpec((None, 1, D_MODEL), lambda s: (layer, 0, 0)),
            pl.BlockSpec((None, D_MODEL, D_IN), lambda s: (layer, 0, 0), pipeline_mode=pl.Buffered(1)),
            per_layer((K_A, W_A)), per_layer((1, W_A)), per_layer((H_A, HD_A, 2 * HD_A)),
            per_layer((1, W_A)), per_layer((1, W_A)), per_layer((1, W_A)),
            per_layer((1, W_C)), per_layer((1, W_C)), per_layer((H_C, CHUNK, CHUNK)),
            per_layer((CHUNK, H_C)),
        ],
        out_specs=[
            pair(W_A), pair(W_C), pair(W_B),
            pl.BlockSpec((None, A_TAIL, W_A), lambda s: (s // pairs_per_seq, 0, 0)),
            pl.BlockSpec((None, 1, W_A), lambda s: (s // pairs_per_seq, 0, 0)),
            pl.BlockSpec((ns, D_IN), lambda s: (0, 0)),
        ],
        out_shape=[
            jax.ShapeDtypeStruct((rows, W_A), BF16),
            jax.ShapeDtypeStruct((rows, W_C), BF16),
            jax.ShapeDtypeStruct((rows, W_B), F32),
            jax.ShapeDtypeStruct((nb, A_TAIL, W_A), F32),
            jax.ShapeDtypeStruct((nb, 1, W_A), F32),
            jax.ShapeDtypeStruct((ns, D_IN), F32),
        ],
        scratch_shapes=scratch,
        compiler_params=pltpu.CompilerParams(
            dimension_semantics=("arbitrary",),
            vmem_limit_bytes=_vmem_limit(
                [3 * tl * D_MODEL * 4, 2 * tl * (W_A + W_C) * 2, 2 * tl * W_B * 4, 1024 * 1024,
                 ns * D_MODEL * 4, ns * D_IN * 4],
                D_MODEL * D_IN * 2 + scratch_bytes, tl * D_IN * 4 + 4 * tl * W_A * 4),
        ),
        name="front",
    )(x, x, xs, g, w, p["conv_a_w"], p["conv_a_b"], p["gate_w"], p["gate_r_b"], p["gate_i_b"],
      p["lru_lambda"], p["sgu_ln_g"], p["sgu_ln_b"], p["sgu_w"], p["sgu_b_t"])


CONV_B_ROWS = 64
NORM_B_ROWS = 32
BACK_ROWS = 256


def _conv_b_tile(w_ref, ub_slab, conv_slab, r0, c, rows):
    first = B_TAIL - (K_B - 1)
    half = rows // 2
    cs = slice(c * LANES, (c + 1) * LANES)
    even = odd = None
    for kp in range(first, first + K_B + 1):
        x = ub_slab[c, pl.ds(r0 + kp, half, stride=2), :]
        if kp < first + K_B:
            term = w_ref[kp - first:kp - first + 1, cs] * x
            even = term if even is None else even + term
        if kp > first:
            term = w_ref[kp - first - 1:kp - first, cs] * x
            odd = term if odd is None else odd + term
    conv_slab[c, pl.ds(r0, half, stride=2), :] = even
    conv_slab[c, pl.ds(r0 + 1, half, stride=2), :] = odd


def _back_kernel(x_ref, ya_ref, yc_ref, ub_ref, ub_prev_ref, xs_ref, mixs_ref, w32_ref, cbw_ref, lbg_ref,
                 lbb_ref, o_ref, os_ref, ub_slab, conv_slab, yb_s, w_ref, *, tm, tiles_per_seq):
    i = pl.program_id(0)

    @pl.when(i == 0)
    def _():
        for c0 in range(0, D_MODEL, PROJ_COLS):
            w_ref[:, c0:c0 + PROJ_COLS] = w32_ref[:, c0:c0 + PROJ_COLS].astype(BF16)
        os_ref[...] = xs_ref[...] + jnp.dot(mixs_ref[...], w_ref[...], preferred_element_type=F32)

    n_slabs = W_B // LANES
    prev = jnp.where(i % tiles_per_seq == 0, 0.0, ub_prev_ref[...])
    for c in range(n_slabs):
        ub_slab[c, 0:B_TAIL, :] = prev[:, c * LANES:(c + 1) * LANES]
        ub_slab[c, B_TAIL:, :] = ub_ref[:, c * LANES:(c + 1) * LANES]

    def conv_rows(r):
        n = CONV_B_ROWS
        for c in range(n_slabs):
            _conv_b_tile(cbw_ref, ub_slab, conv_slab, r, c, n)
        for r1 in range(r, r + n, NORM_B_ROWS):
            conv = jnp.concatenate([conv_slab[c, r1:r1 + NORM_B_ROWS, :] for c in range(n_slabs)], axis=-1)
            y_b = _layer_norm(conv, lbg_ref[...], lbb_ref[...])
            yb_s[r1:r1 + NORM_B_ROWS, :] = jax.nn.silu(y_b).astype(BF16)

    def project(r, c0):
        rs, cs = slice(r, r + BACK_ROWS), slice(c0, c0 + PROJ_COLS)
        o_ref[rs, cs] = (
            x_ref[rs, cs]
            + jnp.dot(ya_ref[rs, :], w_ref[OFF_YA:OFF_YA + W_A, cs], preferred_element_type=F32)
            + jnp.dot(yb_s[rs, :], w_ref[OFF_YB:OFF_YB + W_B, cs], preferred_element_type=F32)
            + jnp.dot(yc_ref[rs, :], w_ref[OFF_YC:OFF_YC + W_C, cs], preferred_element_type=F32))

    fillers = [(functools.partial(project, r, c0), (r + BACK_ROWS) / tm)
               for r in range(0, tm, BACK_ROWS) for c0 in range(0, D_MODEL, PROJ_COLS)]
    _interleave([(CONV_B_ROWS, functools.partial(conv_rows, r)) for r in range(0, tm, CONV_B_ROWS)],
                fillers)


def _back(x, ya, yc, ub, xs, mix_s, w, p, layer, seq, tm):
    rows, ns = x.shape[0], xs.shape[0]
    assert seq % tm == 0 and tm % B_TAIL == 0 and tm % CONV_B_ROWS == 0 and tm % BACK_ROWS == 0
    tails_per_tile = tm // B_TAIL

    def per_layer(shape):
        return pl.BlockSpec((None,) + shape, lambda i: (layer,) + (0,) * len(shape))

    def tile(width):
        return pl.BlockSpec((tm, width), lambda i: (i, 0))

    def whole(width):
        return pl.BlockSpec((ns, width), lambda i: (0, 0))

    return pl.pallas_call(
        functools.partial(_back_kernel, tm=tm, tiles_per_seq=seq // tm),
        grid=(rows // tm,),
        in_specs=[
            tile(D_MODEL), tile(W_A), tile(W_C), tile(W_B),
            pl.BlockSpec((B_TAIL, W_B), lambda i: (jnp.maximum(i * tails_per_tile - 1, 0), 0)),
            whole(D_MODEL), whole(D_MODEL),
            pl.BlockSpec((None, D_MODEL, D_MODEL), lambda i: (layer, 0, 0), pipeline_mode=pl.Buffered(1)),
            per_layer((K_B, W_B)), per_layer((1, W_B)), per_layer((1, W_B)),
        ],
        out_specs=[tile(D_MODEL), whole(D_MODEL)],
        out_shape=[jax.ShapeDtypeStruct((rows, D_MODEL), F32), jax.ShapeDtypeStruct((ns, D_MODEL), F32)],
        scratch_shapes=[pltpu.VMEM((W_B // LANES, B_TAIL + tm, LANES), F32),
                        pltpu.VMEM((W_B // LANES, tm, LANES), F32),
                        pltpu.VMEM((tm, W_B), BF16), pltpu.VMEM((D_MODEL, D_MODEL), BF16)],
        compiler_params=pltpu.CompilerParams(
            dimension_semantics=("arbitrary",),
            vmem_limit_bytes=_vmem_limit(
                [tm * D_MODEL * 4, tm * (W_A + W_C) * 2, tm * W_B * 4, tm * D_MODEL * 4, 256 * 1024,
                 ns * D_MODEL * 10],
                D_MODEL * D_MODEL * (4 + 2) + (B_TAIL + 2 * tm) * W_B * 4 + tm * W_B * 2,
                2 * tm * PROJ_COLS * 4),
        ),
        name="back",
    )(x, ya, yc, ub, ub, xs, mix_s, w, p["conv_b_w"], p["ln_b_g"], p["ln_b_b"])


def _mixer_sample_kernel(
        z_ref, ca_ref, h0_ref, cb_ref, caw_ref, cab_ref, gw_ref, br_ref, bi_ref, lam_ref, cbw_ref,
        lbg_ref, lbb_ref, sg_ref, sb_ref, sw0_ref, sb0_ref,
        mix_ref, ca_out_ref, h_out_ref, cb_out_ref, vn_out_ref):
    xa = z_ref[:, OFF_XA:OFF_XA + W_A]
    for k in range(K_A - 2):
        ca_out_ref[k] = ca_ref[k + 1]
    ca_out_ref[K_A - 2] = xa
    sp = _softplus(-lam_ref[...])
    for h in range(H_A):
        cs = slice(h * HD_A, (h + 1) * HD_A)
        conv_h = cab_ref[:, cs] + caw_ref[K_A - 1:K_A, cs] * xa[:, cs]
        for k in range(K_A - 1):
            conv_h = conv_h + caw_ref[k:k + 1, cs] * ca_ref[k, :, cs]
        pre = jnp.dot(conv_h.astype(BF16), gw_ref[h], preferred_element_type=F32)
        a, u = _lru_gates(conv_h, pre, br_ref[:, cs], bi_ref[:, cs], sp[:, cs])
        h_new = a * h0_ref[:, cs] + u
        h_out_ref[:, cs] = h_new
        mix_ref[:, OFF_YA + h * HD_A:OFF_YA + (h + 1) * HD_A] = (
            h_new * jax.nn.gelu(z_ref[:, OFF_GA + h * HD_A:OFF_GA + (h + 1) * HD_A])).astype(BF16)

    ub = z_ref[:, OFF_XB:OFF_XB + W_B] * jax.nn.sigmoid(z_ref[:, OFF_GB:OFF_GB + W_B])
    acc = cbw_ref[K_B - 1:K_B, :] * ub
    for k in range(K_B - 1):
        acc = acc + cbw_ref[k:k + 1, :] * cb_ref[k]
    for k in range(K_B - 2):
        cb_out_ref[k] = cb_ref[k + 1]
    cb_out_ref[K_B - 2] = ub
    y_b = _layer_norm(acc, lbg_ref[...], lbb_ref[...])
    mix_ref[:, OFF_YB:OFF_YB + W_B] = jax.nn.silu(y_b).astype(BF16)

    g_c = jax.nn.gelu(z_ref[:, OFF_ZC:OFF_ZC + 2 * W_C])
    v_n = _layer_norm(g_c[:, W_C:], sg_ref[...], sb_ref[...])
    vn_out_ref[...] = v_n
    mix_ref[:, OFF_YC:OFF_YC + W_C] = (g_c[:, :W_C] * (sw0_ref[...] * v_n + sb0_ref[...])).astype(BF16)


def _mixer_sample(z, ca_t, h0, cb_t, p, layer, tb):
    nb = z.shape[0]

    def per_layer(shape):
        return pl.BlockSpec((None,) + shape, lambda b: (layer,) + (0,) * len(shape))

    return pl.pallas_call(
        _mixer_sample_kernel,
        grid=(nb // tb,),
        in_specs=[
            pl.BlockSpec((tb, D_IN), lambda b: (b, 0)),
            pl.BlockSpec((None, K_A - 1, tb, W_A), lambda b: (layer, 0, b, 0)),
            pl.BlockSpec((None, tb, W_A), lambda b: (layer, b, 0)),
            pl.BlockSpec((None, K_B - 1, tb, W_B), lambda b: (layer, 0, b, 0)),
            per_layer((K_A, W_A)), per_layer((1, W_A)), per_layer((H_A, HD_A, 2 * HD_A)),
            per_layer((1, W_A)), per_layer((1, W_A)), per_layer((1, W_A)),
            per_layer((K_B, W_B)), per_layer((1, W_B)), per_layer((1, W_B)),
            per_layer((1, W_C)), per_layer((1, W_C)), per_layer((1, W_C)), per_layer((1, W_C)),
        ],
        out_specs=[
            pl.BlockSpec((tb, D_MODEL), lambda b: (b, 0)),
            pl.BlockSpec((K_A - 1, tb, W_A), lambda b: (0, b, 0)),
            pl.BlockSpec((tb, W_A), lambda b: (b, 0)),
            pl.BlockSpec((K_B - 1, tb, W_B), lambda b: (0, b, 0)),
            pl.BlockSpec((tb, W_C), lambda b: (b, 0)),
        ],
        out_shape=[
            jax.ShapeDtypeStruct((nb, D_MODEL), BF16),
            jax.ShapeDtypeStruct((K_A - 1, nb, W_A), F32),
            jax.ShapeDtypeStruct((nb, W_A), F32),
            jax.ShapeDtypeStruct((K_B - 1, nb, W_B), F32),
            jax.ShapeDtypeStruct((nb, W_C), F32),
        ],
        compiler_params=pltpu.CompilerParams(
            dimension_semantics=("arbitrary",),
            vmem_limit_bytes=_vmem_limit(
                [tb * D_IN * 4, 2 * (K_A - 1) * tb * W_A * 4, 2 * tb * W_A * 4,
                 2 * (K_B - 1) * tb * W_B * 4, tb * D_MODEL * 2, tb * W_C * 4, 1024 * 1024],
                0, 8 * tb * W_A * 4),
        ),
        name="mixer_sample",
    )(z, ca_t, h0, cb_t, p["conv_a_w"], p["conv_a_b"], p["gate_w"], p["gate_r_b"], p["gate_i_b"],
      p["lru_lambda"], p["conv_b_w"], p["ln_b_g"], p["ln_b_b"], p["sgu_ln_g"], p["sgu_ln_b"],
      p["sgu_w00"], p["sgu_b0"])


def kernel(x_prompt, x_sample, state_conv_a, state_lru_h, state_conv_b, norm_mix, w_in, conv_a_w,
           conv_a_b, gate_r_w, gate_r_b, gate_i_w, gate_i_b, lru_lambda, conv_b_w, ln_b_g, ln_b_b,
           sgu_ln_g, sgu_ln_b, sgu_w, sgu_b, w_out, norm_ffn, w_ff1, w_ff2, norm_final):
    depth = w_in.shape[0]
    nb, seq, _ = x_prompt.shape
    ns = x_sample.shape[0]

    def row(v):
        return v[:, None, :]

    p = {
        "conv_a_w": conv_a_w, "conv_a_b": row(conv_a_b),
        "gate_w": jnp.concatenate([gate_r_w, gate_i_w], axis=-1).astype(BF16),
        "gate_r_b": row(gate_r_b), "gate_i_b": row(gate_i_b), "lru_lambda": row(lru_lambda),
        "conv_b_w": conv_b_w, "ln_b_g": row(ln_b_g), "ln_b_b": row(ln_b_b),
        "sgu_ln_g": row(sgu_ln_g), "sgu_ln_b": row(sgu_ln_b),
        "sgu_w": sgu_w, "sgu_b_t": jnp.swapaxes(sgu_b, 1, 2),
        "sgu_w00": row(jnp.repeat(sgu_w[:, :, 0, 0], HD_C, axis=-1)),
        "sgu_b0": row(jnp.repeat(sgu_b[:, :, 0], HD_C, axis=-1)),
    }
    g_mix, g_ffn, g_final = row(norm_mix), row(norm_ffn), norm_final[None, :]
    w_in_b = w_in.astype(BF16)

    ca_t = jnp.swapaxes(state_conv_a, 1, 2)
    cb_t = jnp.swapaxes(state_conv_b, 1, 2)

    xp = x_prompt.reshape(nb * seq, D_MODEL)
    xs = x_sample.reshape(ns, D_MODEL)
    ca_p, h_p, cb_p, ca_s, h_s, cb_s, v_s = [], [], [], [], [], [], []
    for l in range(depth):
        last = g_final if l == depth - 1 else None
        ya, yc, ub, ca_l, h_l, zs = _front(xp, xs, g_mix, w_in_b, p, l, nb, seq, tl=256)
        ca_p.append(ca_l[:, A_TAIL - (K_A - 1):, :])
        h_p.append(h_l[:, 0, :])
        cb_p.append(ub.reshape(nb, seq, W_B)[:, seq - (K_B - 1):, :])
        mix_s, cas_l, hs_l, cbs_l, v_l = _mixer_sample(zs, ca_t, state_lru_h, cb_t, p, l, tb=32)
        ca_s.append(jnp.swapaxes(cas_l, 0, 1))
        h_s.append(hs_l)
        cb_s.append(jnp.swapaxes(cbs_l, 0, 1))
        v_s.append(v_l[:, None, :])
        xp, xs = _back(xp, ya, yc, ub, xs, mix_s, w_out, p, l, seq, tm=512)
        xp, xs = _ffn(xp, xs, g_ffn, w_ff1, w_ff2, l, tm=1024, tf=512, g_final=last)

    return (xp.reshape(nb, seq, D_MODEL), xs.reshape(ns, 1, D_MODEL),
            jnp.stack(ca_p), jnp.stack(h_p), jnp.stack(cb_p),
            jnp.stack(ca_s), jnp.stack(h_s), jnp.stack(cb_s), jnp.stack(v_s))
```

```python
import functools

import jax
import jax.numpy as jnp
from jax import lax
from jax.experimental import pallas as pl
from jax.experimental.pallas import tpu as pltpu

F32 = jnp.float32
BF16 = jnp.bfloat16

D_MODEL = 2048
W_A = 1024
H_A = 8
HD_A = W_A // H_A
K_A = 4
LRU_C = 8.0
W_B = 512
K_B = 31
W_C = 512
H_C = 4
HD_C = W_C // H_C
CHUNK = 128
D_IN = 2 * (W_A + W_B + W_C)
D_FF = 4 * D_MODEL
EPS = 1e-6

OFF_XA, OFF_GA, OFF_XB, OFF_GB, OFF_ZC = 0, W_A, 2 * W_A, 2 * W_A + W_B, 2 * W_A + 2 * W_B
OFF_YA, OFF_YB, OFF_YC = 0, W_A, W_A + W_B

SUBLANES = 8
LANES = 128
A_TAIL = SUBLANES
B_TAIL = 32
V7X_VMEM_BYTES = 64 * 1024 * 1024


def _vmem_limit(block_bytes, scratch_bytes, temp_bytes):
    need = 2 * sum(block_bytes) + scratch_bytes + temp_bytes
    return int(min(need + need // 8, V7X_VMEM_BYTES - 4 * 1024 * 1024))


def _rms(x, g):
    return (x * lax.rsqrt(jnp.mean(x * x, axis=-1, keepdims=True) + EPS)) * g


def _layer_norm(x, g, b):
    xc = x - jnp.mean(x, axis=-1, keepdims=True)
    y = xc * lax.rsqrt(jnp.mean(xc * xc, axis=-1, keepdims=True) + EPS)
    return y * g + b


def _softplus(x):
    return jnp.maximum(x, 0.0) + jnp.log1p(jnp.exp(-jnp.abs(x)))


def _lru_gates(conv_h, pre, b_r, b_i, sp):
    r = jax.nn.sigmoid(pre[:, :HD_A] + b_r)
    i = jax.nn.sigmoid(pre[:, HD_A:] + b_i)
    log_a = (-LRU_C * r) * sp
    a = jnp.exp(log_a)
    t = jnp.tanh(log_a)
    u = jnp.sqrt(-2.0 * t / (1.0 - t)) * (i * conv_h)
    return a, u


def _interleave(pieces, fillers):
    total = sum(cost for cost, _ in pieces)
    done, issued = 0, 0
    for cost, piece in pieces:
        while issued < len(fillers) and fillers[issued][1] * total <= done:
            fillers[issued][0]()
            issued += 1
        piece()
        done += cost
    for filler, _ in fillers[issued:]:
        filler()


def _ffn_kernel(*refs, final_norm, with_sample):
    refs = list(refs)
    hf_ref = refs.pop()
    xp_ref = refs[0]
    n_x = 1
    if with_sample:
        xs_ref = refs[1]
        os_ref = refs.pop()
        n_x = 2
    op_ref = refs.pop()
    gf_ref = refs.pop() if final_norm else None
    g_ref, w1_ref, w2_ref = refs[n_x:n_x + 3]
    j = pl.program_id(1)
    tm = xp_ref.shape[0]
    outs = [(op_ref, slice(0, tm))] + ([(os_ref, slice(tm, None))] if with_sample else [])
    srcs = [xp_ref] + ([xs_ref] if with_sample else [])

    @pl.when(j == 0)
    def _():
        for x_ref, (o_ref, rs) in zip(srcs, outs):
            x = x_ref[...]
            hf_ref[rs, :] = _rms(x, g_ref[...]).astype(BF16)
            o_ref[...] = x

    h = jnp.dot(hf_ref[...], w1_ref[...].astype(BF16), preferred_element_type=F32)
    h = jnp.square(jnp.maximum(h, 0.0)).astype(BF16)
    for c0 in range(0, D_MODEL, PROJ_COLS):
        cs = slice(c0, c0 + PROJ_COLS)
        acc = jnp.dot(h, w2_ref[:, cs].astype(BF16), preferred_element_type=F32)
        for o_ref, rs in outs:
            o_ref[:, cs] += acc[rs, :]

    if final_norm:
        @pl.when(j == pl.num_programs(1) - 1)
        def _():
            for o_ref, _ in outs:
                o_ref[...] = _rms(o_ref[...], gf_ref[...])


def _ffn(xp, xs, g, w1, w2, layer, tm, tf, g_final=None):
    m, ns = xp.shape[0], xs.shape[0]
    final_norm = g_final is not None

    def call(with_sample, n_tiles, lead_args, lead_specs, out_specs, out_shape, aliases):
        rows = tm + (ns if with_sample else 0)
        in_specs = lead_specs + [
            pl.BlockSpec((None, 1, D_MODEL), lambda i, j: (layer, 0, 0)),
            pl.BlockSpec((None, D_MODEL, tf), lambda i, j: (layer, 0, j)),
            pl.BlockSpec((None, tf, D_MODEL), lambda i, j: (layer, j, 0)),
        ]
        args = lead_args + [g, w1, w2]
        if final_norm:
            in_specs.append(pl.BlockSpec((1, D_MODEL), lambda i, j: (0, 0)))
            args.append(g_final)
        return pl.pallas_call(
            functools.partial(_ffn_kernel, final_norm=final_norm, with_sample=with_sample),
            grid=(n_tiles, D_FF // tf),
            in_specs=in_specs,
            out_specs=out_specs,
            out_shape=out_shape,
            input_output_aliases=aliases,
            scratch_shapes=[pltpu.VMEM((rows, D_MODEL), BF16)],
            compiler_params=pltpu.CompilerParams(
                dimension_semantics=("arbitrary", "arbitrary"),
                vmem_limit_bytes=_vmem_limit(
                    [D_MODEL * 4, D_MODEL * tf * 4, tf * D_MODEL * 4, tm * D_MODEL * 4,
                     2 * ns * D_MODEL * 4],
                    tm * D_MODEL * 4 + rows * D_MODEL * 2,
                    rows * tf * 6 + rows * PROJ_COLS * 4 + 2 * D_MODEL * tf * 2),
            ),
            name="ffn_head" if with_sample else "ffn_tail",
        )(*args)

    def x_tile(first_tile):
        return pl.BlockSpec((tm, D_MODEL), lambda i, j: (i + first_tile, 0), pipeline_mode=pl.Buffered(1))

    def o_tile(first_tile):
        return pl.BlockSpec((tm, D_MODEL), lambda i, j: (i + first_tile, 0))

    whole = pl.BlockSpec((ns, D_MODEL), lambda i, j: (0, 0))
    full = jax.ShapeDtypeStruct((m, D_MODEL), F32)
    op, os = call(True, 1, [xp, xs], [x_tile(0), whole], [o_tile(0), whole],
                  [full, jax.ShapeDtypeStruct((ns, D_MODEL), F32)], {0: 0})
    op = call(False, m // tm - 1, [op], [x_tile(1)], o_tile(1), full, {0: 0})
    return op, os


N_FRONT_PARAMS = 10
N_FRONT_SCRATCH = 9
PROJ_COLS = 512


def _front_pieces(z_ref, outs, r_out, prm, scr, tl):
    caw_ref, cab_ref, gw_ref, br_ref, bi_ref, _, sg_ref, sb_ref, _, sbt_ref = prm
    xa_buf, a_s, u_s, _, g_s, _, vn_s, wm_s, sp_s = scr
    _, yc_ref, ub_ref = outs
    pieces = []

    def c_norm(r):
        v = jax.nn.gelu(z_ref[r:r + CHUNK, OFF_ZC + W_C:OFF_ZC + 2 * W_C])
        vn_s[r:r + CHUNK, :] = _layer_norm(v, sg_ref[...], sb_ref[...]).astype(BF16)

    def c_head(r, h):
        cs = slice(h * HD_C, (h + 1) * HD_C)
        mixed = jnp.dot(wm_s[h], vn_s[r:r + CHUNK, cs], preferred_element_type=F32)
        mixed = mixed + sbt_ref[:, h:h + 1]
        u_c = jax.nn.gelu(z_ref[r:r + CHUNK, OFF_ZC + h * HD_C:OFF_ZC + (h + 1) * HD_C])
        yc_ref[r_out + r:r_out + r + CHUNK, cs] = (u_c * mixed).astype(BF16)

    def glu(r):
        ub_ref[r_out + r:r_out + r + CHUNK, :] = (
            z_ref[r:r + CHUNK, OFF_XB:OFF_XB + W_B]
            * jax.nn.sigmoid(z_ref[r:r + CHUNK, OFF_GB:OFF_GB + W_B]))

    def a_head(h):
        cs = slice(h * HD_A, (h + 1) * HD_A)
        xa_buf[A_TAIL:A_TAIL + tl, cs] = z_ref[:, OFF_XA + h * HD_A:OFF_XA + (h + 1) * HD_A]
        xa_full = xa_buf[0:A_TAIL + tl, cs]
        conv_h = cab_ref[:, cs] + caw_ref[K_A - 1:K_A, cs] * xa_full[A_TAIL:, :]
        for k in range(K_A - 1):
            shifted = pltpu.roll(xa_full, K_A - 1 - k, axis=0)[A_TAIL:, :]
            conv_h = conv_h + caw_ref[k:k + 1, cs] * shifted
        pre = jnp.dot(conv_h.astype(BF16), gw_ref[h], preferred_element_type=F32)
        a, u = _lru_gates(conv_h, pre, br_ref[:, cs], bi_ref[:, cs], sp_s[:, cs])
        a_s[:, cs] = a
        u_s[:, cs] = u
        g_s[:, cs] = jax.nn.gelu(z_ref[:, OFF_GA + h * HD_A:OFF_GA + (h + 1) * HD_A])

    for c, r in enumerate(range(0, tl, CHUNK)):
        pieces.append((1500, functools.partial(c_norm, r)))
        for h in range(H_C):
            pieces.append((500, functools.partial(c_head, r, h)))
        pieces.append((600, functools.partial(glu, r)))
    for h in range(H_A):
        pieces.append((2000 * tl // 256, functools.partial(a_head, h)))
    return pieces


def _front_kernel(x_ref, x_next_ref, xs_ref, g_ref, w_ref, *rest, tl, chunks_per_seq):
    prm = rest[:N_FRONT_PARAMS]
    lam_ref, sw_ref = prm[5], prm[8]
    ya_ref, yc_ref, ub_ref, ca_out_ref, h_out_ref, zs_ref = rest[N_FRONT_PARAMS:N_FRONT_PARAMS + 6]
    z_even, z_odd, hn_s = rest[N_FRONT_PARAMS + 6:N_FRONT_PARAMS + 9]
    scr = rest[N_FRONT_PARAMS + 9:]
    xa_buf, a_s, u_s, h_s, g_s, carry, _, wm_s, sp_s = scr
    outs = (ya_ref, yc_ref, ub_ref)
    s = pl.program_id(0)

    def normalise(x_rows_ref, r0):
        hn_s[...] = _rms(x_rows_ref[r0:r0 + tl, :], g_ref[...]).astype(BF16)

    def project_and_mix(x_rows_ref, x_r0, z_next, z_cur, r_out):
        normalise(x_rows_ref, x_r0)

        def project(c0):
            z_next[:, c0:c0 + PROJ_COLS] = jnp.dot(
                hn_s[...], w_ref[:, c0:c0 + PROJ_COLS], preferred_element_type=F32)

        _interleave(_front_pieces(z_cur, outs, r_out, prm, scr, tl),
                    [(functools.partial(project, c0), c0 / D_IN) for c0 in range(0, D_IN, PROJ_COLS)])
        xa_buf[0:A_TAIL, :] = xa_buf[tl:tl + A_TAIL, :]

        def scan_row(t, h):
            h = a_s[pl.ds(t, 1), :] * h + u_s[pl.ds(t, 1), :]
            h_s[pl.ds(t, 1), :] = h
            return h

        carry[...] = lax.fori_loop(0, tl, scan_row, carry[...], unroll=8)
        ya_ref[r_out:r_out + tl, :] = (h_s[...] * g_s[...]).astype(BF16)

    @pl.when(s == 0)
    def _():
        causal = (lax.broadcasted_iota(jnp.int32, (CHUNK, CHUNK), 0)
                  >= lax.broadcasted_iota(jnp.int32, (CHUNK, CHUNK), 1))
        for h in range(H_C):
            wm_s[h] = jnp.where(causal, sw_ref[h], 0.0).astype(BF16)
        sp_s[...] = _softplus(-lam_ref[...])
        zs_ref[...] = jnp.dot(_rms(xs_ref[...], g_ref[...]).astype(BF16), w_ref[...],
                              preferred_element_type=F32)
        normalise(x_ref, 0)
        z_even[...] = jnp.dot(hn_s[...], w_ref[...], preferred_element_type=F32)

    @pl.when(s % (chunks_per_seq // 2) == 0)
    def _():
        xa_buf[0:A_TAIL, :] = jnp.zeros((A_TAIL, W_A), F32)
        carry[...] = jnp.zeros((1, W_A), F32)

    project_and_mix(x_ref, tl, z_odd, z_even, 0)
    project_and_mix(x_next_ref, 0, z_even, z_odd, tl)
    ca_out_ref[...] = xa_buf[0:A_TAIL, :]
    h_out_ref[...] = carry[...]


def _front(x, xs, g, w, p, layer, nb, seq, tl):
    ns = xs.shape[0]
    chunks_per_seq = seq // tl
    assert seq % tl == 0 and chunks_per_seq % 2 == 0 and tl % CHUNK == 0
    pairs = nb * chunks_per_seq // 2
    pairs_per_seq = chunks_per_seq // 2
    rows = nb * seq

    def per_layer(shape):
        return pl.BlockSpec((None,) + shape, lambda s: (layer,) + (0,) * len(shape))

    def pair(width):
        return pl.BlockSpec((2 * tl, width), lambda s: (s, 0))

    scratch = [
        pltpu.VMEM((tl, D_IN), F32), pltpu.VMEM((tl, D_IN), F32), pltpu.VMEM((tl, D_MODEL), BF16),
        pltpu.VMEM((A_TAIL + tl, W_A), F32),
        pltpu.VMEM((tl, W_A), F32), pltpu.VMEM((tl, W_A), F32), pltpu.VMEM((tl, W_A), F32),
        pltpu.VMEM((tl, W_A), F32), pltpu.VMEM((1, W_A), F32),
        pltpu.VMEM((tl, W_C), BF16), pltpu.VMEM((H_C, CHUNK, CHUNK), BF16), pltpu.VMEM((1, W_A), F32),
    ]
    assert len(scratch) == 3 + N_FRONT_SCRATCH
    scratch_bytes = (2 * tl * D_IN * 4 + tl * D_MODEL * 2
                     + (A_TAIL + 5 * tl) * W_A * 4 + tl * W_C * 2 + H_C * CHUNK * CHUNK * 2)
    return pl.pallas_call(
        functools.partial(_front_kernel, tl=tl, chunks_per_seq=chunks_per_seq),
        grid=(pairs,),
        in_specs=[
            pair(D_MODEL),
            pl.BlockSpec((tl, D_MODEL), lambda s: (jnp.minimum(2 * s + 2, 2 * pairs - 1), 0)),
            pl.BlockSpec((ns, D_MODEL), lambda s: (0, 0)),
            pl.BlockSpec((None, 1, D_MODEL), lambda s: (layer, 0, 0)),
            pl.BlockSpec((None, D_MODEL, D_IN), lambda s: (layer, 0, 0), pipeline_mode=pl.Buffered(1)),
            per_layer((K_A, W_A)), per_layer((1, W_A)), per_layer((H_A, HD_A, 2 * HD_A)),
            per_layer((1, W_A)), per_layer((1, W_A)), per_layer((1, W_A)),
            per_layer((1, W_C)), per_layer((1, W_C)), per_layer((H_C, CHUNK, CHUNK)),
            per_layer((CHUNK, H_C)),
        ],
        out_specs=[
            pair(W_A), pair(W_C), pair(W_B),
            pl.BlockSpec((None, A_TAIL, W_A), lambda s: (s // pairs_per_seq, 0, 0)),
            pl.BlockSpec((None, 1, W_A), lambda s: (s // pairs_per_seq, 0, 0)),
            pl.BlockSpec((ns, D_IN), lambda s: (0, 0)),
        ],
        out_shape=[
            jax.ShapeDtypeStruct((rows, W_A), BF16),
            jax.ShapeDtypeStruct((rows, W_C), BF16),
            jax.ShapeDtypeStruct((rows, W_B), F32),
            jax.ShapeDtypeStruct((nb, A_TAIL, W_A), F32),
            jax.ShapeDtypeStruct((nb, 1, W_A), F32),
            jax.ShapeDtypeStruct((ns, D_IN), F32),
        ],
        scratch_shapes=scratch,
        compiler_params=pltpu.CompilerParams(
            dimension_semantics=("arbitrary",),
            vmem_limit_bytes=_vmem_limit(
                [3 * tl * D_MODEL * 4, 2 * tl * (W_A + W_C) * 2, 2 * tl * W_B * 4, 1024 * 1024,
                 ns * D_MODEL * 4, ns * D_IN * 4],
                D_MODEL * D_IN * 2 + scratch_bytes, tl * D_IN * 4 + 4 * tl * W_A * 4),
        ),
        name="front",
    )(x, x, xs, g, w, p["conv_a_w"], p["conv_a_b"], p["gate_w"], p["gate_r_b"], p["gate_i_b"],
      p["lru_lambda"], p["sgu_ln_g"], p["sgu_ln_b"], p["sgu_w"], p["sgu_b_t"])


CONV_B_ROWS = 64
NORM_B_ROWS = 32
BACK_ROWS = 256


def _conv_b_tile(w_ref, ub_slab, conv_slab, r0, c, rows):
    first = B_TAIL - (K_B - 1)
    half = rows // 2
    cs = slice(c * LANES, (c + 1) * LANES)
    even = odd = None
    for kp in range(first, first + K_B + 1):
        x = ub_slab[c, pl.ds(r0 + kp, half, stride=2), :]
        if kp < first + K_B:
            term = w_ref[kp - first:kp - first + 1, cs] * x
            even = term if even is None else even + term
        if kp > first:
            term = w_ref[kp - first - 1:kp - first, cs] * x
            odd = term if odd is None else odd + term
    conv_slab[c, pl.ds(r0, half, stride=2), :] = even
    conv_slab[c, pl.ds(r0 + 1, half, stride=2), :] = odd


def _back_kernel(x_ref, ya_ref, yc_ref, ub_ref, ub_prev_ref, xs_ref, mixs_ref, w32_ref, cbw_ref, lbg_ref,
                 lbb_ref, o_ref, os_ref, ub_slab, conv_slab, yb_s, w_ref, *, tm, tiles_per_seq):
    i = pl.program_id(0)

    @pl.when(i == 0)
    def _():
        for c0 in range(0, D_MODEL, PROJ_COLS):
            w_ref[:, c0:c0 + PROJ_COLS] = w32_ref[:, c0:c0 + PROJ_COLS].astype(BF16)
        os_ref[...] = xs_ref[...] + jnp.dot(mixs_ref[...], w_ref[...], preferred_element_type=F32)

    n_slabs = W_B // LANES
    prev = jnp.where(i % tiles_per_seq == 0, 0.0, ub_prev_ref[...])
    for c in range(n_slabs):
        ub_slab[c, 0:B_TAIL, :] = prev[:, c * LANES:(c + 1) * LANES]
        ub_slab[c, B_TAIL:, :] = ub_ref[:, c * LANES:(c + 1) * LANES]

    def conv_rows(r):
        n = CONV_B_ROWS
        for c in range(n_slabs):
            _conv_b_tile(cbw_ref, ub_slab, conv_slab, r, c, n)
        for r1 in range(r, r + n, NORM_B_ROWS):
            conv = jnp.concatenate([conv_slab[c, r1:r1 + NORM_B_ROWS, :] for c in range(n_slabs)], axis=-1)
            y_b = _layer_norm(conv, lbg_ref[...], lbb_ref[...])
            yb_s[r1:r1 + NORM_B_ROWS, :] = jax.nn.silu(y_b).astype(BF16)

    def project(r, c0):
        rs, cs = slice(r, r + BACK_ROWS), slice(c0, c0 + PROJ_COLS)
        o_ref[rs, cs] = (
            x_ref[rs, cs]
            + jnp.dot(ya_ref[rs, :], w_ref[OFF_YA:OFF_YA + W_A, cs], preferred_element_type=F32)
            + jnp.dot(yb_s[rs, :], w_ref[OFF_YB:OFF_YB + W_B, cs], preferred_element_type=F32)
            + jnp.dot(yc_ref[rs, :], w_ref[OFF_YC:OFF_YC + W_C, cs], preferred_element_type=F32))

    fillers = [(functools.partial(project, r, c0), (r + BACK_ROWS) / tm)
               for r in range(0, tm, BACK_ROWS) for c0 in range(0, D_MODEL, PROJ_COLS)]
    _interleave([(CONV_B_ROWS, functools.partial(conv_rows, r)) for r in range(0, tm, CONV_B_ROWS)],
                fillers)


def _back(x, ya, yc, ub, xs, mix_s, w, p, layer, seq, tm):
    rows, ns = x.shape[0], xs.shape[0]
    assert seq % tm == 0 and tm % B_TAIL == 0 and tm % CONV_B_ROWS == 0 and tm % BACK_ROWS == 0
    tails_per_tile = tm // B_TAIL

    def per_layer(shape):
        return pl.BlockSpec((None,) + shape, lambda i: (layer,) + (0,) * len(shape))

    def tile(width):
        return pl.BlockSpec((tm, width), lambda i: (i, 0))

    def whole(width):
        return pl.BlockSpec((ns, width), lambda i: (0, 0))

    return pl.pallas_call(
        functools.partial(_back_kernel, tm=tm, tiles_per_seq=seq // tm),
        grid=(rows // tm,),
        in_specs=[
            tile(D_MODEL), tile(W_A), tile(W_C), tile(W_B),
            pl.BlockSpec((B_TAIL, W_B), lambda i: (jnp.maximum(i * tails_per_tile - 1, 0), 0)),
            whole(D_MODEL), whole(D_MODEL),
            pl.BlockSpec((None, D_MODEL, D_MODEL), lambda i: (layer, 0, 0), pipeline_mode=pl.Buffered(1)),
            per_layer((K_B, W_B)), per_layer((1, W_B)), per_layer((1, W_B)),
        ],
        out_specs=[tile(D_MODEL), whole(D_MODEL)],
        out_shape=[jax.ShapeDtypeStruct((rows, D_MODEL), F32), jax.ShapeDtypeStruct((ns, D_MODEL), F32)],
        scratch_shapes=[pltpu.VMEM((W_B // LANES, B_TAIL + tm, LANES), F32),
                        pltpu.VMEM((W_B // LANES, tm, LANES), F32),
                        pltpu.VMEM((tm, W_B), BF16), pltpu.VMEM((D_MODEL, D_MODEL), BF16)],
        compiler_params=pltpu.CompilerParams(
            dimension_semantics=("arbitrary",),
            vmem_limit_bytes=_vmem_limit(
                [tm * D_MODEL * 4, tm * (W_A + W_C) * 2, tm * W_B * 4, tm * D_MODEL * 4, 256 * 1024,
                 ns * D_MODEL * 10],
                D_MODEL * D_MODEL * (4 + 2) + (B_TAIL + 2 * tm) * W_B * 4 + tm * W_B * 2,
                2 * tm * PROJ_COLS * 4),
        ),
        name="back",
    )(x, ya, yc, ub, ub, xs, mix_s, w, p["conv_b_w"], p["ln_b_g"], p["ln_b_b"])


def _mixer_sample_kernel(
        z_ref, ca_ref, h0_ref, cb_ref, caw_ref, cab_ref, gw_ref, br_ref, bi_ref, lam_ref, cbw_ref,
        lbg_ref, lbb_ref, sg_ref, sb_ref, sw0_ref, sb0_ref,
        mix_ref, ca_out_ref, h_out_ref, cb_out_ref, vn_out_ref):
    xa = z_ref[:, OFF_XA:OFF_XA + W_A]
    for k in range(K_A - 2):
        ca_out_ref[k] = ca_ref[k + 1]
    ca_out_ref[K_A - 2] = xa
    sp = _softplus(-lam_ref[...])
    for h in range(H_A):
        cs = slice(h * HD_A, (h + 1) * HD_A)
        conv_h = cab_ref[:, cs] + caw_ref[K_A - 1:K_A, cs] * xa[:, cs]
        for k in range(K_A - 1):
            conv_h = conv_h + caw_ref[k:k + 1, cs] * ca_ref[k, :, cs]
        pre = jnp.dot(conv_h.astype(BF16), gw_ref[h], preferred_element_type=F32)
        a, u = _lru_gates(conv_h, pre, br_ref[:, cs], bi_ref[:, cs], sp[:, cs])
        h_new = a * h0_ref[:, cs] + u
        h_out_ref[:, cs] = h_new
        mix_ref[:, OFF_YA + h * HD_A:OFF_YA + (h + 1) * HD_A] = (
            h_new * jax.nn.gelu(z_ref[:, OFF_GA + h * HD_A:OFF_GA + (h + 1) * HD_A])).astype(BF16)

    ub = z_ref[:, OFF_XB:OFF_XB + W_B] * jax.nn.sigmoid(z_ref[:, OFF_GB:OFF_GB + W_B])
    acc = cbw_ref[K_B - 1:K_B, :] * ub
    for k in range(K_B - 1):
        acc = acc + cbw_ref[k:k + 1, :] * cb_ref[k]
    for k in range(K_B - 2):
        cb_out_ref[k] = cb_ref[k + 1]
    cb_out_ref[K_B - 2] = ub
    y_b = _layer_norm(acc, lbg_ref[...], lbb_ref[...])
    mix_ref[:, OFF_YB:OFF_YB + W_B] = jax.nn.silu(y_b).astype(BF16)

    g_c = jax.nn.gelu(z_ref[:, OFF_ZC:OFF_ZC + 2 * W_C])
    v_n = _layer_norm(g_c[:, W_C:], sg_ref[...], sb_ref[...])
    vn_out_ref[...] = v_n
    mix_ref[:, OFF_YC:OFF_YC + W_C] = (g_c[:, :W_C] * (sw0_ref[...] * v_n + sb0_ref[...])).astype(BF16)


def _mixer_sample(z, ca_t, h0, cb_t, p, layer, tb):
    nb = z.shape[0]

    def per_layer(shape):
        return pl.BlockSpec((None,) + shape, lambda b: (layer,) + (0,) * len(shape))

    return pl.pallas_call(
        _mixer_sample_kernel,
        grid=(nb // tb,),
        in_specs=[
            pl.BlockSpec((tb, D_IN), lambda b: (b, 0)),
            pl.BlockSpec((None, K_A - 1, tb, W_A), lambda b: (layer, 0, b, 0)),
            pl.BlockSpec((None, tb, W_A), lambda b: (layer, b, 0)),
            pl.BlockSpec((None, K_B - 1, tb, W_B), lambda b: (layer, 0, b, 0)),
            per_layer((K_A, W_A)), per_layer((1, W_A)), per_layer((H_A, HD_A, 2 * HD_A)),
            per_layer((1, W_A)), per_layer((1, W_A)), per_layer((1, W_A)),
            per_layer((K_B, W_B)), per_layer((1, W_B)), per_layer((1, W_B)),
            per_layer((1, W_C)), per_layer((1, W_C)), per_layer((1, W_C)), per_layer((1, W_C)),
        ],
        out_specs=[
            pl.BlockSpec((tb, D_MODEL), lambda b: (b, 0)),
            pl.BlockSpec((K_A - 1, tb, W_A), lambda b: (0, b, 0)),
            pl.BlockSpec((tb, W_A), lambda b: (b, 0)),
            pl.BlockSpec((K_B - 1, tb, W_B), lambda b: (0, b, 0)),
            pl.BlockSpec((tb, W_C), lambda b: (b, 0)),
        ],
        out_shape=[
            jax.ShapeDtypeStruct((nb, D_MODEL), BF16),
            jax.ShapeDtypeStruct((K_A - 1, nb, W_A), F32),
            jax.ShapeDtypeStruct((nb, W_A), F32),
            jax.ShapeDtypeStruct((K_B - 1, nb, W_B), F32),
            jax.ShapeDtypeStruct((nb, W_C), F32),
        ],
        compiler_params=pltpu.CompilerParams(
            dimension_semantics=("arbitrary",),
            vmem_limit_bytes=_vmem_limit(
                [tb * D_IN * 4, 2 * (K_A - 1) * tb * W_A * 4, 2 * tb * W_A * 4,
                 2 * (K_B - 1) * tb * W_B * 4, tb * D_MODEL * 2, tb * W_C * 4, 1024 * 1024],
                0, 8 * tb * W_A * 4),
        ),
        name="mixer_sample",
    )(z, ca_t, h0, cb_t, p["conv_a_w"], p["conv_a_b"], p["gate_w"], p["gate_r_b"], p["gate_i_b"],
      p["lru_lambda"], p["conv_b_w"], p["ln_b_g"], p["ln_b_b"], p["sgu_ln_g"], p["sgu_ln_b"],
      p["sgu_w00"], p["sgu_b0"])


def kernel(x_prompt, x_sample, state_conv_a, state_lru_h, state_conv_b, norm_mix, w_in, conv_a_w,
           conv_a_b, gate_r_w, gate_r_b, gate_i_w, gate_i_b, lru_lambda, conv_b_w, ln_b_g, ln_b_b,
           sgu_ln_g, sgu_ln_b, sgu_w, sgu_b, w_out, norm_ffn, w_ff1, w_ff2, norm_final):
    depth = w_in.shape[0]
    nb, seq, _ = x_prompt.shape
    ns = x_sample.shape[0]

    def row(v):
        return v[:, None, :]

    p = {
        "conv_a_w": conv_a_w, "conv_a_b": row(conv_a_b),
        "gate_w": jnp.concatenate([gate_r_w, gate_i_w], axis=-1).astype(BF16),
        "gate_r_b": row(gate_r_b), "gate_i_b": row(gate_i_b), "lru_lambda": row(lru_lambda),
        "conv_b_w": conv_b_w, "ln_b_g": row(ln_b_g), "ln_b_b": row(ln_b_b),
        "sgu_ln_g": row(sgu_ln_g), "sgu_ln_b": row(sgu_ln_b),
        "sgu_w": sgu_w, "sgu_b_t": jnp.swapaxes(sgu_b, 1, 2),
        "sgu_w00": row(jnp.repeat(sgu_w[:, :, 0, 0], HD_C, axis=-1)),
        "sgu_b0": row(jnp.repeat(sgu_b[:, :, 0], HD_C, axis=-1)),
    }
    g_mix, g_ffn, g_final = row(norm_mix), row(norm_ffn), norm_final[None, :]
    w_in_b = w_in.astype(BF16)

    ca_t = jnp.swapaxes(state_conv_a, 1, 2)
    cb_t = jnp.swapaxes(state_conv_b, 1, 2)

    xp = x_prompt.reshape(nb * seq, D_MODEL)
    xs = x_sample.reshape(ns, D_MODEL)
    ca_p, h_p, cb_p, ca_s, h_s, cb_s, v_s = [], [], [], [], [], [], []
    for l in range(depth):
        last = g_final if l == depth - 1 else None
        ya, yc, ub, ca_l, h_l, zs = _front(xp, xs, g_mix, w_in_b, p, l, nb, seq, tl=256)
        ca_p.append(ca_l[:, A_TAIL - (K_A - 1):, :])
        h_p.append(h_l[:, 0, :])
        cb_p.append(ub.reshape(nb, seq, W_B)[:, seq - (K_B - 1):, :])
        mix_s, cas_l, hs_l, cbs_l, v_l = _mixer_sample(zs, ca_t, state_lru_h, cb_t, p, l, tb=32)
        ca_s.append(jnp.swapaxes(cas_l, 0, 1))
        h_s.append(hs_l)
        cb_s.append(jnp.swapaxes(cbs_l, 0, 1))
        v_s.append(v_l[:, None, :])
        xp, xs = _back(xp, ya, yc, ub, xs, mix_s, w_out, p, l, seq, tm=512)
        xp, xs = _ffn(xp, xs, g_ffn, w_ff1, w_ff2, l, tm=1024, tf=512, g_final=last)

    return (xp.reshape(nb, seq, D_MODEL), xs.reshape(ns, 1, D_MODEL),
            jnp.stack(ca_p), jnp.stack(h_p), jnp.stack(cb_p),
            jnp.stack(ca_s), jnp.stack(h_s), jnp.stack(cb_s), jnp.stack(v_s))
```

```python
import functools

import jax
import jax.numpy as jnp
from jax import lax
from jax.experimental import pallas as pl
from jax.experimental.pallas import tpu as pltpu

F32 = jnp.float32
BF16 = jnp.bfloat16

D_MODEL = 2048
W_A = 1024
H_A = 8
HD_A = W_A // H_A
K_A = 4
LRU_C = 8.0
W_B = 512
K_B = 31
W_C = 512
H_C = 4
HD_C = W_C // H_C
CHUNK = 128
D_IN = 2 * (W_A + W_B + W_C)
D_FF = 4 * D_MODEL
EPS = 1e-6

OFF_XA, OFF_GA, OFF_XB, OFF_GB, OFF_ZC = 0, W_A, 2 * W_A, 2 * W_A + W_B, 2 * W_A + 2 * W_B
OFF_YA, OFF_YB, OFF_YC = 0, W_A, W_A + W_B

SUBLANES = 8
LANES = 128
A_TAIL = SUBLANES
B_TAIL = 32
V7X_VMEM_BYTES = 64 * 1024 * 1024


def _vmem_limit(block_bytes, scratch_bytes, temp_bytes):
    need = 2 * sum(block_bytes) + scratch_bytes + temp_bytes
    return int(min(need + need // 8, V7X_VMEM_BYTES - 4 * 1024 * 1024))


def _rms(x, g):
    return (x * lax.rsqrt(jnp.mean(x * x, axis=-1, keepdims=True) + EPS)) * g


def _layer_norm(x, g, b):
    xc = x - jnp.mean(x, axis=-1, keepdims=True)
    y = xc * lax.rsqrt(jnp.mean(xc * xc, axis=-1, keepdims=True) + EPS)
    return y * g + b


def _softplus(x):
    return jnp.maximum(x, 0.0) + jnp.log1p(jnp.exp(-jnp.abs(x)))


def _lru_gates(conv_h, pre, b_r, b_i, sp):
    r = jax.nn.sigmoid(pre[:, :HD_A] + b_r)
    i = jax.nn.sigmoid(pre[:, HD_A:] + b_i)
    log_a = (-LRU_C * r) * sp
    a = jnp.exp(log_a)
    t = jnp.tanh(log_a)
    u = jnp.sqrt(-2.0 * t / (1.0 - t)) * (i * conv_h)
    return a, u


def _interleave(pieces, fillers):
    total = sum(cost for cost, _ in pieces)
    done, issued = 0, 0
    for cost, piece in pieces:
        while issued < len(fillers) and fillers[issued][1] * total <= done:
            fillers[issued][0]()
            issued += 1
        piece()
        done += cost
    for filler, _ in fillers[issued:]:
        filler()


def _ffn_kernel(*refs, final_norm, with_sample):
    refs = list(refs)
    hf_ref = refs.pop()
    xp_ref = refs[0]
    n_x = 1
    if with_sample:
        xs_ref = refs[1]
        os_ref = refs.pop()
        n_x = 2
    op_ref = refs.pop()
    gf_ref = refs.pop() if final_norm else None
    g_ref, w1_ref, w2_ref = refs[n_x:n_x + 3]
    j = pl.program_id(1)
    tm = xp_ref.shape[0]
    outs = [(op_ref, slice(0, tm))] + ([(os_ref, slice(tm, None))] if with_sample else [])
    srcs = [xp_ref] + ([xs_ref] if with_sample else [])

    @pl.when(j == 0)
    def _():
        for x_ref, (o_ref, rs) in zip(srcs, outs):
            x = x_ref[...]
            hf_ref[rs, :] = _rms(x, g_ref[...]).astype(BF16)
            o_ref[...] = x

    h = jnp.dot(hf_ref[...], w1_ref[...].astype(BF16), preferred_element_type=F32)
    h = jnp.square(jnp.maximum(h, 0.0)).astype(BF16)
    for c0 in range(0, D_MODEL, PROJ_COLS):
        cs = slice(c0, c0 + PROJ_COLS)
        acc = jnp.dot(h, w2_ref[:, cs].astype(BF16), preferred_element_type=F32)
        for o_ref, rs in outs:
            o_ref[:, cs] += acc[rs, :]

    if final_norm:
        @pl.when(j == pl.num_programs(1) - 1)
        def _():
            for o_ref, _ in outs:
                o_ref[...] = _rms(o_ref[...], gf_ref[...])


def _ffn(xp, xs, g, w1, w2, layer, tm, tf, g_final=None):
    m, ns = xp.shape[0], xs.shape[0]
    final_norm = g_final is not None

    def call(with_sample, n_tiles, lead_args, lead_specs, out_specs, out_shape, aliases):
        rows = tm + (ns if with_sample else 0)
        in_specs = lead_specs + [
            pl.BlockSpec((None, 1, D_MODEL), lambda i, j: (layer, 0, 0)),
            pl.BlockSpec((None, D_MODEL, tf), lambda i, j: (layer, 0, j)),
            pl.BlockSpec((None, tf, D_MODEL), lambda i, j: (layer, j, 0)),
        ]
        args = lead_args + [g, w1, w2]
        if final_norm:
            in_specs.append(pl.BlockSpec((1, D_MODEL), lambda i, j: (0, 0)))
            args.append(g_final)
        return pl.pallas_call(
            functools.partial(_ffn_kernel, final_norm=final_norm, with_sample=with_sample),
            grid=(n_tiles, D_FF // tf),
            in_specs=in_specs,
            out_specs=out_specs,
            out_shape=out_shape,
            input_output_aliases=aliases,
            scratch_shapes=[pltpu.VMEM((rows, D_MODEL), BF16)],
            compiler_params=pltpu.CompilerParams(
                dimension_semantics=("arbitrary", "arbitrary"),
                vmem_limit_bytes=_vmem_limit(
                    [D_MODEL * 4, D_MODEL * tf * 4, tf * D_MODEL * 4, tm * D_MODEL * 4,
                     2 * ns * D_MODEL * 4],
                    tm * D_MODEL * 4 + rows * D_MODEL * 2,
                    rows * tf * 6 + rows * PROJ_COLS * 4 + 2 * D_MODEL * tf * 2),
            ),
            name="ffn_head" if with_sample else "ffn_tail",
        )(*args)

    def x_tile(first_tile):
        return pl.BlockSpec((tm, D_MODEL), lambda i, j: (i + first_tile, 0), pipeline_mode=pl.Buffered(1))

    def o_tile(first_tile):
        return pl.BlockSpec((tm, D_MODEL), lambda i, j: (i + first_tile, 0))

    whole = pl.BlockSpec((ns, D_MODEL), lambda i, j: (0, 0))
    full = jax.ShapeDtypeStruct((m, D_MODEL), F32)
    op, os = call(True, 1, [xp, xs], [x_tile(0), whole], [o_tile(0), whole],
                  [full, jax.ShapeDtypeStruct((ns, D_MODEL), F32)], {0: 0})
    op = call(False, m // tm - 1, [op], [x_tile(1)], o_tile(1), full, {0: 0})
    return op, os


N_FRONT_PARAMS = 11
N_FRONT_SCRATCH = 10
PROJ_COLS = 512


def _pack_gate_weights(gwr_ref, gwi_ref, gw_ref):
    for h in range(H_A):
        gw_ref[h, :, 0:HD_A] = gwr_ref[h].astype(BF16)
        gw_ref[h, :, HD_A:] = gwi_ref[h].astype(BF16)


def _front_pieces(z_ref, outs, r_out, prm, scr, tl):
    caw_ref, cab_ref, _, _, br_ref, bi_ref, _, sg_ref, sb_ref, _, sbt_ref = prm
    xa_buf, a_s, u_s, _, g_s, _, vn_s, wm_s, sp_s, gw_ref = scr
    _, yc_ref, ub_ref = outs
    pieces = []

    def c_norm(r):
        v = jax.nn.gelu(z_ref[r:r + CHUNK, OFF_ZC + W_C:OFF_ZC + 2 * W_C])
        vn_s[r:r + CHUNK, :] = _layer_norm(v, sg_ref[...], sb_ref[...]).astype(BF16)

    def c_head(r, h):
        cs = slice(h * HD_C, (h + 1) * HD_C)
        mixed = jnp.dot(wm_s[h], vn_s[r:r + CHUNK, cs], preferred_element_type=F32)
        mixed = mixed + sbt_ref[:, h:h + 1]
        u_c = jax.nn.gelu(z_ref[r:r + CHUNK, OFF_ZC + h * HD_C:OFF_ZC + (h + 1) * HD_C])
        yc_ref[r_out + r:r_out + r + CHUNK, cs] = (u_c * mixed).astype(BF16)

    def glu(r):
        ub_ref[r_out + r:r_out + r + CHUNK, :] = (
            z_ref[r:r + CHUNK, OFF_XB:OFF_XB + W_B]
            * jax.nn.sigmoid(z_ref[r:r + CHUNK, OFF_GB:OFF_GB + W_B]))

    def a_head(h):
        cs = slice(h * HD_A, (h + 1) * HD_A)
        xa_buf[A_TAIL:A_TAIL + tl, cs] = z_ref[:, OFF_XA + h * HD_A:OFF_XA + (h + 1) * HD_A]
        xa_full = xa_buf[0:A_TAIL + tl, cs]
        conv_h = cab_ref[:, cs] + caw_ref[K_A - 1:K_A, cs] * xa_full[A_TAIL:, :]
        for k in range(K_A - 1):
            shifted = pltpu.roll(xa_full, K_A - 1 - k, axis=0)[A_TAIL:, :]
            conv_h = conv_h + caw_ref[k:k + 1, cs] * shifted
        pre = jnp.dot(conv_h.astype(BF16), gw_ref[h], preferred_element_type=F32)
        a, u = _lru_gates(conv_h, pre, br_ref[:, cs], bi_ref[:, cs], sp_s[:, cs])
        a_s[:, cs] = a
        u_s[:, cs] = u
        g_s[:, cs] = jax.nn.gelu(z_ref[:, OFF_GA + h * HD_A:OFF_GA + (h + 1) * HD_A])

    for c, r in enumerate(range(0, tl, CHUNK)):
        pieces.append((1500, functools.partial(c_norm, r)))
        for h in range(H_C):
            pieces.append((500, functools.partial(c_head, r, h)))
        pieces.append((600, functools.partial(glu, r)))
    for h in range(H_A):
        pieces.append((2000 * tl // 256, functools.partial(a_head, h)))
    return pieces


def _front_kernel(x_ref, x_next_ref, xs_ref, g_ref, w_ref, *rest, tl, chunks_per_seq):
    prm = rest[:N_FRONT_PARAMS]
    gwr_ref, gwi_ref, lam_ref, sw_ref = prm[2], prm[3], prm[6], prm[9]
    ya_ref, yc_ref, ub_ref, ca_out_ref, h_out_ref, zs_ref = rest[N_FRONT_PARAMS:N_FRONT_PARAMS + 6]
    z_even, z_odd, hn_s = rest[N_FRONT_PARAMS + 6:N_FRONT_PARAMS + 9]
    scr = rest[N_FRONT_PARAMS + 9:]
    xa_buf, a_s, u_s, h_s, g_s, carry, _, wm_s, sp_s, gw_s = scr
    outs = (ya_ref, yc_ref, ub_ref)
    s = pl.program_id(0)

    def normalise(x_rows_ref, r0):
        hn_s[...] = _rms(x_rows_ref[r0:r0 + tl, :], g_ref[...]).astype(BF16)

    def project_and_mix(x_rows_ref, x_r0, z_next, z_cur, r_out):
        normalise(x_rows_ref, x_r0)

        def project(c0):
            z_next[:, c0:c0 + PROJ_COLS] = jnp.dot(
                hn_s[...], w_ref[:, c0:c0 + PROJ_COLS], preferred_element_type=F32)

        _interleave(_front_pieces(z_cur, outs, r_out, prm, scr, tl),
                    [(functools.partial(project, c0), c0 / D_IN) for c0 in range(0, D_IN, PROJ_COLS)])
        xa_buf[0:A_TAIL, :] = xa_buf[tl:tl + A_TAIL, :]

        def scan_row(t, h):
            h = a_s[pl.ds(t, 1), :] * h + u_s[pl.ds(t, 1), :]
            h_s[pl.ds(t, 1), :] = h
            return h

        carry[...] = lax.fori_loop(0, tl, scan_row, carry[...], unroll=8)
        ya_ref[r_out:r_out + tl, :] = (h_s[...] * g_s[...]).astype(BF16)

    @pl.when(s == 0)
    def _():
        causal = (lax.broadcasted_iota(jnp.int32, (CHUNK, CHUNK), 0)
                  >= lax.broadcasted_iota(jnp.int32, (CHUNK, CHUNK), 1))
        for h in range(H_C):
            wm_s[h] = jnp.where(causal, sw_ref[h], 0.0).astype(BF16)
        sp_s[...] = _softplus(-lam_ref[...])
        _pack_gate_weights(gwr_ref, gwi_ref, gw_s)
        zs_ref[...] = jnp.dot(_rms(xs_ref[...], g_ref[...]).astype(BF16), w_ref[...],
                              preferred_element_type=F32)
        normalise(x_ref, 0)
        z_even[...] = jnp.dot(hn_s[...], w_ref[...], preferred_element_type=F32)

    @pl.when(s % (chunks_per_seq // 2) == 0)
    def _():
        xa_buf[0:A_TAIL, :] = jnp.zeros((A_TAIL, W_A), F32)
        carry[...] = jnp.zeros((1, W_A), F32)

    project_and_mix(x_ref, tl, z_odd, z_even, 0)
    project_and_mix(x_next_ref, 0, z_even, z_odd, tl)
    ca_out_ref[...] = xa_buf[0:A_TAIL, :]
    h_out_ref[...] = carry[...]


def _front(x, xs, g, w, p, layer, nb, seq, tl):
    ns = xs.shape[0]
    chunks_per_seq = seq // tl
    assert seq % tl == 0 and chunks_per_seq % 2 == 0 and tl % CHUNK == 0
    pairs = nb * chunks_per_seq // 2
    pairs_per_seq = chunks_per_seq // 2
    rows = nb * seq

    def per_layer(shape):
        return pl.BlockSpec((None,) + shape, lambda s: (layer,) + (0,) * len(shape))

    def pair(width):
        return pl.BlockSpec((2 * tl, width), lambda s: (s, 0))

    scratch = [
        pltpu.VMEM((tl, D_IN), F32), pltpu.VMEM((tl, D_IN), F32), pltpu.VMEM((tl, D_MODEL), BF16),
        pltpu.VMEM((A_TAIL + tl, W_A), F32),
        pltpu.VMEM((tl, W_A), F32), pltpu.VMEM((tl, W_A), F32), pltpu.VMEM((tl, W_A), F32),
        pltpu.VMEM((tl, W_A), F32), pltpu.VMEM((1, W_A), F32),
        pltpu.VMEM((tl, W_C), BF16), pltpu.VMEM((H_C, CHUNK, CHUNK), BF16), pltpu.VMEM((1, W_A), F32),
        pltpu.VMEM((H_A, HD_A, 2 * HD_A), BF16),
    ]
    assert len(scratch) == 3 + N_FRONT_SCRATCH
    scratch_bytes = (2 * tl * D_IN * 4 + tl * D_MODEL * 2
                     + (A_TAIL + 5 * tl) * W_A * 4 + tl * W_C * 2 + H_C * CHUNK * CHUNK * 2)
    return pl.pallas_call(
        functools.partial(_front_kernel, tl=tl, chunks_per_seq=chunks_per_seq),
        grid=(pairs,),
        in_specs=[
            pair(D_MODEL),
            pl.BlockSpec((tl, D_MODEL), lambda s: (jnp.minimum(2 * s + 2, 2 * pairs - 1), 0)),
            pl.BlockSpec((ns, D_MODEL), lambda s: (0, 0)),
            pl.BlockSpec((None, 1, D_MODEL), lambda s: (layer, 0, 0)),
            pl.BlockSpec((None, D_MODEL, D_IN), lambda s: (layer, 0, 0), pipeline_mode=pl.Buffered(1)),
            per_layer((K_A, W_A)), per_layer((1, W_A)),
            per_layer((H_A, HD_A, HD_A)), per_layer((H_A, HD_A, HD_A)),
            per_layer((1, W_A)), per_layer((1, W_A)), per_layer((1, W_A)),
            per_layer((1, W_C)), per_layer((1, W_C)), per_layer((H_C, CHUNK, CHUNK)),
            per_layer((CHUNK, H_C)),
        ],
        out_specs=[
            pair(W_A), pair(W_C), pair(W_B),
            pl.BlockSpec((None, A_TAIL, W_A), lambda s: (s // pairs_per_seq, 0, 0)),
            pl.BlockSpec((None, 1, W_A), lambda s: (s // pairs_per_seq, 0, 0)),
            pl.BlockSpec((ns, D_IN), lambda s: (0, 0)),
        ],
        out_shape=[
            jax.ShapeDtypeStruct((rows, W_A), BF16),
            jax.ShapeDtypeStruct((rows, W_C), BF16),
            jax.ShapeDtypeStruct((rows, W_B), F32),
            jax.ShapeDtypeStruct((nb, A_TAIL, W_A), F32),
            jax.ShapeDtypeStruct((nb, 1, W_A), F32),
            jax.ShapeDtypeStruct((ns, D_IN), F32),
        ],
        scratch_shapes=scratch,
        compiler_params=pltpu.CompilerParams(
            dimension_semantics=("arbitrary",),
            vmem_limit_bytes=_vmem_limit(
                [3 * tl * D_MODEL * 4, 2 * tl * (W_A + W_C) * 2, 2 * tl * W_B * 4, 1024 * 1024,
                 ns * D_MODEL * 4, ns * D_IN * 4],
                D_MODEL * D_IN * 2 + scratch_bytes, tl * D_IN * 4 + 4 * tl * W_A * 4),
        ),
        name="front",
    )(x, x, xs, g, w, p["conv_a_w"], p["conv_a_b"], p["gate_r_w"], p["gate_i_w"], p["gate_r_b"],
      p["gate_i_b"], p["lru_lambda"], p["sgu_ln_g"], p["sgu_ln_b"], p["sgu_w"], p["sgu_b_t"])


CONV_B_ROWS = 64
NORM_B_ROWS = 32
BACK_ROWS = 256


def _conv_b_tile(w_ref, ub_slab, conv_slab, r0, c, rows):
    first = B_TAIL - (K_B - 1)
    half = rows // 2
    cs = slice(c * LANES, (c + 1) * LANES)
    even = odd = None
    for kp in range(first, first + K_B + 1):
        x = ub_slab[c, pl.ds(r0 + kp, half, stride=2), :]
        if kp < first + K_B:
            term = w_ref[kp - first:kp - first + 1, cs] * x
            even = term if even is None else even + term
        if kp > first:
            term = w_ref[kp - first - 1:kp - first, cs] * x
            odd = term if odd is None else odd + term
    conv_slab[c, pl.ds(r0, half, stride=2), :] = even
    conv_slab[c, pl.ds(r0 + 1, half, stride=2), :] = odd


def _back_kernel(x_ref, ya_ref, yc_ref, ub_ref, ub_prev_ref, xs_ref, mixs_ref, w32_ref, cbw_ref, lbg_ref,
                 lbb_ref, o_ref, os_ref, ub_slab, conv_slab, yb_s, w_ref, *, tm, tiles_per_seq):
    i = pl.program_id(0)

    @pl.when(i == 0)
    def _():
        for c0 in range(0, D_MODEL, PROJ_COLS):
            w_ref[:, c0:c0 + PROJ_COLS] = w32_ref[:, c0:c0 + PROJ_COLS].astype(BF16)
        os_ref[...] = xs_ref[...] + jnp.dot(mixs_ref[...], w_ref[...], preferred_element_type=F32)

    n_slabs = W_B // LANES
    prev = jnp.where(i % tiles_per_seq == 0, 0.0, ub_prev_ref[...])
    for c in range(n_slabs):
        ub_slab[c, 0:B_TAIL, :] = prev[:, c * LANES:(c + 1) * LANES]
        ub_slab[c, B_TAIL:, :] = ub_ref[:, c * LANES:(c + 1) * LANES]

    def conv_rows(r):
        n = CONV_B_ROWS
        for c in range(n_slabs):
            _conv_b_tile(cbw_ref, ub_slab, conv_slab, r, c, n)
        for r1 in range(r, r + n, NORM_B_ROWS):
            conv = jnp.concatenate([conv_slab[c, r1:r1 + NORM_B_ROWS, :] for c in range(n_slabs)], axis=-1)
            y_b = _layer_norm(conv, lbg_ref[...], lbb_ref[...])
            yb_s[r1:r1 + NORM_B_ROWS, :] = jax.nn.silu(y_b).astype(BF16)

    def project(r, c0):
        rs, cs = slice(r, r + BACK_ROWS), slice(c0, c0 + PROJ_COLS)
        o_ref[rs, cs] = (
            x_ref[rs, cs]
            + jnp.dot(ya_ref[rs, :], w_ref[OFF_YA:OFF_YA + W_A, cs], preferred_element_type=F32)
            + jnp.dot(yb_s[rs, :], w_ref[OFF_YB:OFF_YB + W_B, cs], preferred_element_type=F32)
            + jnp.dot(yc_ref[rs, :], w_ref[OFF_YC:OFF_YC + W_C, cs], preferred_element_type=F32))

    fillers = [(functools.partial(project, r, c0), (r + BACK_ROWS) / tm)
               for r in range(0, tm, BACK_ROWS) for c0 in range(0, D_MODEL, PROJ_COLS)]
    _interleave([(CONV_B_ROWS, functools.partial(conv_rows, r)) for r in range(0, tm, CONV_B_ROWS)],
                fillers)


def _back(x, ya, yc, ub, xs, mix_s, w, p, layer, seq, tm):
    rows, ns = x.shape[0], xs.shape[0]
    assert seq % tm == 0 and tm % B_TAIL == 0 and tm % CONV_B_ROWS == 0 and tm % BACK_ROWS == 0
    tails_per_tile = tm // B_TAIL

    def per_layer(shape):
        return pl.BlockSpec((None,) + shape, lambda i: (layer,) + (0,) * len(shape))

    def tile(width):
        return pl.BlockSpec((tm, width), lambda i: (i, 0))

    def whole(width):
        return pl.BlockSpec((ns, width), lambda i: (0, 0))

    return pl.pallas_call(
        functools.partial(_back_kernel, tm=tm, tiles_per_seq=seq // tm),
        grid=(rows // tm,),
        in_specs=[
            tile(D_MODEL), tile(W_A), tile(W_C), tile(W_B),
            pl.BlockSpec((B_TAIL, W_B), lambda i: (jnp.maximum(i * tails_per_tile - 1, 0), 0)),
            whole(D_MODEL), whole(D_MODEL),
            pl.BlockSpec((None, D_MODEL, D_MODEL), lambda i: (layer, 0, 0), pipeline_mode=pl.Buffered(1)),
            per_layer((K_B, W_B)), per_layer((1, W_B)), per_layer((1, W_B)),
        ],
        out_specs=[tile(D_MODEL), whole(D_MODEL)],
        out_shape=[jax.ShapeDtypeStruct((rows, D_MODEL), F32), jax.ShapeDtypeStruct((ns, D_MODEL), F32)],
        scratch_shapes=[pltpu.VMEM((W_B // LANES, B_TAIL + tm, LANES), F32),
                        pltpu.VMEM((W_B // LANES, tm, LANES), F32),
                        pltpu.VMEM((tm, W_B), BF16), pltpu.VMEM((D_MODEL, D_MODEL), BF16)],
        compiler_params=pltpu.CompilerParams(
            dimension_semantics=("arbitrary",),
            vmem_limit_bytes=_vmem_limit(
                [tm * D_MODEL * 4, tm * (W_A + W_C) * 2, tm * W_B * 4, tm * D_MODEL * 4, 256 * 1024,
                 ns * D_MODEL * 10],
                D_MODEL * D_MODEL * (4 + 2) + (B_TAIL + 2 * tm) * W_B * 4 + tm * W_B * 2,
                2 * tm * PROJ_COLS * 4),
        ),
        name="back",
    )(x, ya, yc, ub, ub, xs, mix_s, w, p["conv_b_w"], p["ln_b_g"], p["ln_b_b"])


def _mixer_sample_kernel(
        z_ref, ca_ref, h0_ref, cb_ref, caw_ref, cab_ref, gwr_ref, gwi_ref, br_ref, bi_ref, lam_ref,
        cbw_ref, lbg_ref, lbb_ref, sg_ref, sb_ref, sw0_ref, sb0_ref,
        mix_ref, ca_out_ref, h_out_ref, cb_out_ref, vn_out_ref):
    xa = z_ref[:, OFF_XA:OFF_XA + W_A]
    for k in range(K_A - 2):
        ca_out_ref[k] = ca_ref[k + 1]
    ca_out_ref[K_A - 2] = xa
    sp = _softplus(-lam_ref[...])
    for h in range(H_A):
        cs = slice(h * HD_A, (h + 1) * HD_A)
        conv_h = cab_ref[:, cs] + caw_ref[K_A - 1:K_A, cs] * xa[:, cs]
        for k in range(K_A - 1):
            conv_h = conv_h + caw_ref[k:k + 1, cs] * ca_ref[k, :, cs]
        conv_b = conv_h.astype(BF16)
        pre = jnp.concatenate(
            [jnp.dot(conv_b, gwr_ref[h].astype(BF16), preferred_element_type=F32),
             jnp.dot(conv_b, gwi_ref[h].astype(BF16), preferred_element_type=F32)], axis=-1)
        a, u = _lru_gates(conv_h, pre, br_ref[:, cs], bi_ref[:, cs], sp[:, cs])
        h_new = a * h0_ref[:, cs] + u
        h_out_ref[:, cs] = h_new
        mix_ref[:, OFF_YA + h * HD_A:OFF_YA + (h + 1) * HD_A] = (
            h_new * jax.nn.gelu(z_ref[:, OFF_GA + h * HD_A:OFF_GA + (h + 1) * HD_A])).astype(BF16)

    ub = z_ref[:, OFF_XB:OFF_XB + W_B] * jax.nn.sigmoid(z_ref[:, OFF_GB:OFF_GB + W_B])
    acc = cbw_ref[K_B - 1:K_B, :] * ub
    for k in range(K_B - 1):
        acc = acc + cbw_ref[k:k + 1, :] * cb_ref[k]
    for k in range(K_B - 2):
        cb_out_ref[k] = cb_ref[k + 1]
    cb_out_ref[K_B - 2] = ub
    y_b = _layer_norm(acc, lbg_ref[...], lbb_ref[...])
    mix_ref[:, OFF_YB:OFF_YB + W_B] = jax.nn.silu(y_b).astype(BF16)

    g_c = jax.nn.gelu(z_ref[:, OFF_ZC:OFF_ZC + 2 * W_C])
    v_n = _layer_norm(g_c[:, W_C:], sg_ref[...], sb_ref[...])
    vn_out_ref[...] = v_n
    mix_ref[:, OFF_YC:OFF_YC + W_C] = (g_c[:, :W_C] * (sw0_ref[...] * v_n + sb0_ref[...])).astype(BF16)


def _mixer_sample(z, ca_t, h0, cb_t, p, layer, tb):
    nb = z.shape[0]

    def per_layer(shape):
        return pl.BlockSpec((None,) + shape, lambda b: (layer,) + (0,) * len(shape))

    return pl.pallas_call(
        _mixer_sample_kernel,
        grid=(nb // tb,),
        in_specs=[
            pl.BlockSpec((tb, D_IN), lambda b: (b, 0)),
            pl.BlockSpec((None, K_A - 1, tb, W_A), lambda b: (layer, 0, b, 0)),
            pl.BlockSpec((None, tb, W_A), lambda b: (layer, b, 0)),
            pl.BlockSpec((None, K_B - 1, tb, W_B), lambda b: (layer, 0, b, 0)),
            per_layer((K_A, W_A)), per_layer((1, W_A)),
            per_layer((H_A, HD_A, HD_A)), per_layer((H_A, HD_A, HD_A)),
            per_layer((1, W_A)), per_layer((1, W_A)), per_layer((1, W_A)),
            per_layer((K_B, W_B)), per_layer((1, W_B)), per_layer((1, W_B)),
            per_layer((1, W_C)), per_layer((1, W_C)), per_layer((1, W_C)), per_layer((1, W_C)),
        ],
        out_specs=[
            pl.BlockSpec((tb, D_MODEL), lambda b: (b, 0)),
            pl.BlockSpec((K_A - 1, tb, W_A), lambda b: (0, b, 0)),
            pl.BlockSpec((tb, W_A), lambda b: (b, 0)),
            pl.BlockSpec((K_B - 1, tb, W_B), lambda b: (0, b, 0)),
            pl.BlockSpec((tb, W_C), lambda b: (b, 0)),
        ],
        out_shape=[
            jax.ShapeDtypeStruct((nb, D_MODEL), BF16),
            jax.ShapeDtypeStruct((K_A - 1, nb, W_A), F32),
            jax.ShapeDtypeStruct((nb, W_A), F32),
            jax.ShapeDtypeStruct((K_B - 1, nb, W_B), F32),
            jax.ShapeDtypeStruct((nb, W_C), F32),
        ],
        compiler_params=pltpu.CompilerParams(
            dimension_semantics=("arbitrary",),
            vmem_limit_bytes=_vmem_limit(
                [tb * D_IN * 4, 2 * (K_A - 1) * tb * W_A * 4, 2 * tb * W_A * 4,
                 2 * (K_B - 1) * tb * W_B * 4, tb * D_MODEL * 2, tb * W_C * 4, 1024 * 1024],
                0, 8 * tb * W_A * 4),
        ),
        name="mixer_sample",
    )(z, ca_t, h0, cb_t, p["conv_a_w"], p["conv_a_b"], p["gate_r_w"], p["gate_i_w"], p["gate_r_b"],
      p["gate_i_b"], p["lru_lambda"], p["conv_b_w"], p["ln_b_g"], p["ln_b_b"], p["sgu_ln_g"], p["sgu_ln_b"],
      p["sgu_w00"], p["sgu_b0"])


def kernel(x_prompt, x_sample, state_conv_a, state_lru_h, state_conv_b, norm_mix, w_in, conv_a_w,
           conv_a_b, gate_r_w, gate_r_b, gate_i_w, gate_i_b, lru_lambda, conv_b_w, ln_b_g, ln_b_b,
           sgu_ln_g, sgu_ln_b, sgu_w, sgu_b, w_out, norm_ffn, w_ff1, w_ff2, norm_final):
    depth = w_in.shape[0]
    nb, seq, _ = x_prompt.shape
    ns = x_sample.shape[0]

    def row(v):
        return v[:, None, :]

    p = {
        "conv_a_w": conv_a_w, "conv_a_b": row(conv_a_b),
        "gate_r_w": gate_r_w, "gate_i_w": gate_i_w,
        "gate_r_b": row(gate_r_b), "gate_i_b": row(gate_i_b), "lru_lambda": row(lru_lambda),
        "conv_b_w": conv_b_w, "ln_b_g": row(ln_b_g), "ln_b_b": row(ln_b_b),
        "sgu_ln_g": row(sgu_ln_g), "sgu_ln_b": row(sgu_ln_b),
        "sgu_w": sgu_w, "sgu_b_t": jnp.swapaxes(sgu_b, 1, 2),
        "sgu_w00": row(jnp.repeat(sgu_w[:, :, 0, 0], HD_C, axis=-1)),
        "sgu_b0": row(jnp.repeat(sgu_b[:, :, 0], HD_C, axis=-1)),
    }
    g_mix, g_ffn, g_final = row(norm_mix), row(norm_ffn), norm_final[None, :]
    w_in_b = w_in.astype(BF16)

    ca_t = jnp.swapaxes(state_conv_a, 1, 2)
    cb_t = jnp.swapaxes(state_conv_b, 1, 2)

    xp = x_prompt.reshape(nb * seq, D_MODEL)
    xs = x_sample.reshape(ns, D_MODEL)
    ca_p, h_p, cb_p, ca_s, h_s, cb_s, v_s = [], [], [], [], [], [], []
    for l in range(depth):
        last = g_final if l == depth - 1 else None
        ya, yc, ub, ca_l, h_l, zs = _front(xp, xs, g_mix, w_in_b, p, l, nb, seq, tl=256)
        ca_p.append(ca_l[:, A_TAIL - (K_A - 1):, :])
        h_p.append(h_l[:, 0, :])
        cb_p.append(ub.reshape(nb, seq, W_B)[:, seq - (K_B - 1):, :])
        mix_s, cas_l, hs_l, cbs_l, v_l = _mixer_sample(zs, ca_t, state_lru_h, cb_t, p, l, tb=32)
        ca_s.append(jnp.swapaxes(cas_l, 0, 1))
        h_s.append(hs_l)
        cb_s.append(jnp.swapaxes(cbs_l, 0, 1))
        v_s.append(v_l[:, None, :])
        xp, xs = _back(xp, ya, yc, ub, xs, mix_s, w_out, p, l, seq, tm=512)
        xp, xs = _ffn(xp, xs, g_ffn, w_ff1, w_ff2, l, tm=1024, tf=512, g_final=last)

    return (xp.reshape(nb, seq, D_MODEL), xs.reshape(ns, 1, D_MODEL),
            jnp.stack(ca_p), jnp.stack(h_p), jnp.stack(cb_p),
            jnp.stack(ca_s), jnp.stack(h_s), jnp.stack(cb_s), jnp.stack(v_s))
```

```python
import functools

import jax
import jax.numpy as jnp
from jax import lax
from jax.experimental import pallas as pl
from jax.experimental.pallas import tpu as pltpu

F32 = jnp.float32
BF16 = jnp.bfloat16

D_MODEL = 2048
W_A = 1024
H_A = 8
HD_A = W_A // H_A
K_A = 4
LRU_C = 8.0
W_B = 512
K_B = 31
W_C = 512
H_C = 4
HD_C = W_C // H_C
CHUNK = 128
D_IN = 2 * (W_A + W_B + W_C)
D_FF = 4 * D_MODEL
EPS = 1e-6

OFF_XA, OFF_GA, OFF_XB, OFF_GB, OFF_ZC = 0, W_A, 2 * W_A, 2 * W_A + W_B, 2 * W_A + 2 * W_B
OFF_YA, OFF_YB, OFF_YC = 0, W_A, W_A + W_B

SUBLANES = 8
LANES = 128
A_TAIL = SUBLANES
B_TAIL = 32
V7X_VMEM_BYTES = 64 * 1024 * 1024


def _vmem_limit(block_bytes, scratch_bytes, temp_bytes):
    need = 2 * sum(block_bytes) + scratch_bytes + temp_bytes
    return int(min(need + need // 8, V7X_VMEM_BYTES - 4 * 1024 * 1024))


def _rms(x, g):
    return (x * lax.rsqrt(jnp.mean(x * x, axis=-1, keepdims=True) + EPS)) * g


def _layer_norm(x, g, b):
    xc = x - jnp.mean(x, axis=-1, keepdims=True)
    y = xc * lax.rsqrt(jnp.mean(xc * xc, axis=-1, keepdims=True) + EPS)
    return y * g + b


def _softplus(x):
    return jnp.maximum(x, 0.0) + jnp.log1p(jnp.exp(-jnp.abs(x)))


def _lru_gates(conv_h, pre, b_r, b_i, sp):
    r = jax.nn.sigmoid(pre[:, :HD_A] + b_r)
    i = jax.nn.sigmoid(pre[:, HD_A:] + b_i)
    log_a = (-LRU_C * r) * sp
    a = jnp.exp(log_a)
    t = jnp.tanh(log_a)
    u = jnp.sqrt(-2.0 * t / (1.0 - t)) * (i * conv_h)
    return a, u


def _interleave(pieces, fillers):
    total = sum(cost for cost, _ in pieces)
    done, issued = 0, 0
    for cost, piece in pieces:
        while issued < len(fillers) and fillers[issued][1] * total <= done:
            fillers[issued][0]()
            issued += 1
        piece()
        done += cost
    for filler, _ in fillers[issued:]:
        filler()


FFN_COLS = 512


def _ffn_kernel(*refs, final_norm, with_sample):
    refs = list(refs)
    hf_ref = refs.pop()
    xp_ref = refs[0]
    n_x = 1
    if with_sample:
        xs_ref = refs[1]
        os_ref = refs.pop()
        n_x = 2
    op_ref = refs.pop()
    gf_ref = refs.pop() if final_norm else None
    g_ref, w1_ref, w2_ref = refs[n_x:n_x + 3]
    j = pl.program_id(1)
    tm = xp_ref.shape[0]
    outs = [(op_ref, slice(0, tm))] + ([(os_ref, slice(tm, None))] if with_sample else [])
    srcs = [xp_ref] + ([xs_ref] if with_sample else [])

    @pl.when(j == 0)
    def _():
        for x_ref, (o_ref, rs) in zip(srcs, outs):
            x = x_ref[...]
            hf_ref[rs, :] = _rms(x, g_ref[...]).astype(BF16)
            o_ref[...] = x

    h = jnp.dot(hf_ref[...], w1_ref[...].astype(BF16), preferred_element_type=F32)
    h = jnp.square(jnp.maximum(h, 0.0)).astype(BF16)
    for c0 in range(0, D_MODEL, FFN_COLS):
        cs = slice(c0, c0 + FFN_COLS)
        acc = jnp.dot(h, w2_ref[:, cs].astype(BF16), preferred_element_type=F32)
        for o_ref, rs in outs:
            o_ref[:, cs] += acc[rs, :]

    if final_norm:
        @pl.when(j == pl.num_programs(1) - 1)
        def _():
            for o_ref, _ in outs:
                o_ref[...] = _rms(o_ref[...], gf_ref[...])


def _ffn(xp, xs, g, w1, w2, layer, tm, tf, g_final=None):
    m, ns = xp.shape[0], xs.shape[0]
    final_norm = g_final is not None

    def call(with_sample, n_tiles, lead_args, lead_specs, out_specs, out_shape, aliases):
        rows = tm + (ns if with_sample else 0)
        in_specs = lead_specs + [
            pl.BlockSpec((None, 1, D_MODEL), lambda i, j: (layer, 0, 0)),
            pl.BlockSpec((None, D_MODEL, tf), lambda i, j: (layer, 0, j)),
            pl.BlockSpec((None, tf, D_MODEL), lambda i, j: (layer, j, 0)),
        ]
        args = lead_args + [g, w1, w2]
        if final_norm:
            in_specs.append(pl.BlockSpec((1, D_MODEL), lambda i, j: (0, 0)))
            args.append(g_final)
        return pl.pallas_call(
            functools.partial(_ffn_kernel, final_norm=final_norm, with_sample=with_sample),
            grid=(n_tiles, D_FF // tf),
            in_specs=in_specs,
            out_specs=out_specs,
            out_shape=out_shape,
            input_output_aliases=aliases,
            scratch_shapes=[pltpu.VMEM((rows, D_MODEL), BF16)],
            compiler_params=pltpu.CompilerParams(
                dimension_semantics=("arbitrary", "arbitrary"),
                vmem_limit_bytes=_vmem_limit(
                    [D_MODEL * 4, D_MODEL * tf * 4, tf * D_MODEL * 4, tm * D_MODEL * 4,
                     2 * ns * D_MODEL * 4],
                    tm * D_MODEL * 4 + rows * D_MODEL * 2,
                    rows * tf * 6 + rows * FFN_COLS * 4 + 2 * D_MODEL * tf * 2),
            ),
            name="ffn_head" if with_sample else "ffn_tail",
        )(*args)

    def x_tile(first_tile):
        return pl.BlockSpec((tm, D_MODEL), lambda i, j: (i + first_tile, 0), pipeline_mode=pl.Buffered(1))

    def o_tile(first_tile):
        return pl.BlockSpec((tm, D_MODEL), lambda i, j: (i + first_tile, 0))

    whole = pl.BlockSpec((ns, D_MODEL), lambda i, j: (0, 0))
    full = jax.ShapeDtypeStruct((m, D_MODEL), F32)
    op, os = call(True, 1, [xp, xs], [x_tile(0), whole], [o_tile(0), whole],
                  [full, jax.ShapeDtypeStruct((ns, D_MODEL), F32)], {0: 0})
    op = call(False, m // tm - 1, [op], [x_tile(1)], o_tile(1), full, {0: 0})
    return op, os


N_FRONT_PARAMS = 11
N_FRONT_SCRATCH = 10
PROJ_COLS = 512


def _pack_gate_weights(gwr_ref, gwi_ref, gw_ref):
    for h in range(H_A):
        gw_ref[h, :, 0:HD_A] = gwr_ref[h].astype(BF16)
        gw_ref[h, :, HD_A:] = gwi_ref[h].astype(BF16)


def _front_pieces(z_ref, outs, r_out, prm, scr, tl):
    caw_ref, cab_ref, _, _, br_ref, bi_ref, _, sg_ref, sb_ref, _, sbt_ref = prm
    xa_buf, a_s, u_s, _, g_s, _, vn_s, wm_s, sp_s, gw_ref = scr
    _, yc_ref, ub_ref = outs
    pieces = []

    def c_norm(r):
        v = jax.nn.gelu(z_ref[r:r + CHUNK, OFF_ZC + W_C:OFF_ZC + 2 * W_C])
        vn_s[r:r + CHUNK, :] = _layer_norm(v, sg_ref[...], sb_ref[...]).astype(BF16)

    def c_head(r, h):
        cs = slice(h * HD_C, (h + 1) * HD_C)
        mixed = jnp.dot(wm_s[h], vn_s[r:r + CHUNK, cs], preferred_element_type=F32)
        mixed = mixed + sbt_ref[:, h:h + 1]
        u_c = jax.nn.gelu(z_ref[r:r + CHUNK, OFF_ZC + h * HD_C:OFF_ZC + (h + 1) * HD_C])
        yc_ref[r_out + r:r_out + r + CHUNK, cs] = (u_c * mixed).astype(BF16)

    def glu(r):
        ub_ref[r_out + r:r_out + r + CHUNK, :] = (
            z_ref[r:r + CHUNK, OFF_XB:OFF_XB + W_B]
            * jax.nn.sigmoid(z_ref[r:r + CHUNK, OFF_GB:OFF_GB + W_B]))

    def a_head(h):
        cs = slice(h * HD_A, (h + 1) * HD_A)
        xa_buf[A_TAIL:A_TAIL + tl, cs] = z_ref[:, OFF_XA + h * HD_A:OFF_XA + (h + 1) * HD_A]
        xa_full = xa_buf[0:A_TAIL + tl, cs]
        conv_h = cab_ref[:, cs] + caw_ref[K_A - 1:K_A, cs] * xa_full[A_TAIL:, :]
        for k in range(K_A - 1):
            shifted = pltpu.roll(xa_full, K_A - 1 - k, axis=0)[A_TAIL:, :]
            conv_h = conv_h + caw_ref[k:k + 1, cs] * shifted
        pre = jnp.dot(conv_h.astype(BF16), gw_ref[h], preferred_element_type=F32)
        a, u = _lru_gates(conv_h, pre, br_ref[:, cs], bi_ref[:, cs], sp_s[:, cs])
        a_s[:, cs] = a
        u_s[:, cs] = u
        g_s[:, cs] = jax.nn.gelu(z_ref[:, OFF_GA + h * HD_A:OFF_GA + (h + 1) * HD_A])

    other = []
    for r in range(0, tl, CHUNK):
        other.append((1500, functools.partial(c_norm, r)))
        for h in range(H_C):
            other.append((500, functools.partial(c_head, r, h)))
        other.append((600, functools.partial(glu, r)))
    per_gate = -(-len(other) // H_A)
    for h in range(H_A):
        pieces.append((2000 * tl // 256, functools.partial(a_head, h)))
        pieces.extend(other[h * per_gate:(h + 1) * per_gate])
    return pieces


def _front_kernel(x_ref, x_next_ref, xs_ref, g_ref, w_ref, *rest, tl, chunks_per_seq):
    prm = rest[:N_FRONT_PARAMS]
    gwr_ref, gwi_ref, lam_ref, sw_ref = prm[2], prm[3], prm[6], prm[9]
    ya_ref, yc_ref, ub_ref, ca_out_ref, h_out_ref, zs_ref = rest[N_FRONT_PARAMS:N_FRONT_PARAMS + 6]
    z_even, z_odd, hn_s = rest[N_FRONT_PARAMS + 6:N_FRONT_PARAMS + 9]
    scr = rest[N_FRONT_PARAMS + 9:]
    xa_buf, a_s, u_s, h_s, g_s, carry, _, wm_s, sp_s, gw_s = scr
    outs = (ya_ref, yc_ref, ub_ref)
    s = pl.program_id(0)

    def normalise(x_rows_ref, r0):
        hn_s[...] = _rms(x_rows_ref[r0:r0 + tl, :], g_ref[...]).astype(BF16)

    def project_and_mix(x_rows_ref, x_r0, z_next, z_cur, r_out):
        normalise(x_rows_ref, x_r0)

        def project(c0):
            z_next[:, c0:c0 + PROJ_COLS] = jnp.dot(
                hn_s[...], w_ref[:, c0:c0 + PROJ_COLS], preferred_element_type=F32)

        _interleave(_front_pieces(z_cur, outs, r_out, prm, scr, tl),
                    [(functools.partial(project, c0), c0 / D_IN) for c0 in range(0, D_IN, PROJ_COLS)])
        xa_buf[0:A_TAIL, :] = xa_buf[tl:tl + A_TAIL, :]

        def scan_row(t, h):
            h = a_s[pl.ds(t, 1), :] * h + u_s[pl.ds(t, 1), :]
            h_s[pl.ds(t, 1), :] = h
            return h

        carry[...] = lax.fori_loop(0, tl, scan_row, carry[...], unroll=8)
        ya_ref[r_out:r_out + tl, :] = (h_s[...] * g_s[...]).astype(BF16)

    @pl.when(s == 0)
    def _():
        causal = (lax.broadcasted_iota(jnp.int32, (CHUNK, CHUNK), 0)
                  >= lax.broadcasted_iota(jnp.int32, (CHUNK, CHUNK), 1))
        for h in range(H_C):
            wm_s[h] = jnp.where(causal, sw_ref[h], 0.0).astype(BF16)
        sp_s[...] = _softplus(-lam_ref[...])
        _pack_gate_weights(gwr_ref, gwi_ref, gw_s)
        zs_ref[...] = jnp.dot(_rms(xs_ref[...], g_ref[...]).astype(BF16), w_ref[...],
                              preferred_element_type=F32)
        normalise(x_ref, 0)
        z_even[...] = jnp.dot(hn_s[...], w_ref[...], preferred_element_type=F32)

    @pl.when(s % (chunks_per_seq // 2) == 0)
    def _():
        xa_buf[0:A_TAIL, :] = jnp.zeros((A_TAIL, W_A), F32)
        carry[...] = jnp.zeros((1, W_A), F32)

    project_and_mix(x_ref, tl, z_odd, z_even, 0)
    project_and_mix(x_next_ref, 0, z_even, z_odd, tl)
    ca_out_ref[...] = xa_buf[0:A_TAIL, :]
    h_out_ref[...] = carry[...]


def _front(x, xs, g, w, p, layer, nb, seq, tl):
    ns = xs.shape[0]
    chunks_per_seq = seq // tl
    assert seq % tl == 0 and chunks_per_seq % 2 == 0 and tl % CHUNK == 0
    pairs = nb * chunks_per_seq // 2
    pairs_per_seq = chunks_per_seq // 2
    rows = nb * seq

    def per_layer(shape):
        return pl.BlockSpec((None,) + shape, lambda s: (layer,) + (0,) * len(shape))

    def pair(width):
        return pl.BlockSpec((2 * tl, width), lambda s: (s, 0))

    scratch = [
        pltpu.VMEM((tl, D_IN), F32), pltpu.VMEM((tl, D_IN), F32), pltpu.VMEM((tl, D_MODEL), BF16),
        pltpu.VMEM((A_TAIL + tl, W_A), F32),
        pltpu.VMEM((tl, W_A), F32), pltpu.VMEM((tl, W_A), F32), pltpu.VMEM((tl, W_A), F32),
        pltpu.VMEM((tl, W_A), F32), pltpu.VMEM((1, W_A), F32),
        pltpu.VMEM((tl, W_C), BF16), pltpu.VMEM((H_C, CHUNK, CHUNK), BF16), pltpu.VMEM((1, W_A), F32),
        pltpu.VMEM((H_A, HD_A, 2 * HD_A), BF16),
    ]
    assert len(scratch) == 3 + N_FRONT_SCRATCH
    scratch_bytes = (2 * tl * D_IN * 4 + tl * D_MODEL * 2
                     + (A_TAIL + 5 * tl) * W_A * 4 + tl * W_C * 2 + H_C * CHUNK * CHUNK * 2)
    return pl.pallas_call(
        functools.partial(_front_kernel, tl=tl, chunks_per_seq=chunks_per_seq),
        grid=(pairs,),
        in_specs=[
            pair(D_MODEL),
            pl.BlockSpec((tl, D_MODEL), lambda s: (jnp.minimum(2 * s + 2, 2 * pairs - 1), 0)),
            pl.BlockSpec((ns, D_MODEL), lambda s: (0, 0)),
            pl.BlockSpec((None, 1, D_MODEL), lambda s: (layer, 0, 0)),
            pl.BlockSpec((None, D_MODEL, D_IN), lambda s: (layer, 0, 0), pipeline_mode=pl.Buffered(1)),
            per_layer((K_A, W_A)), per_layer((1, W_A)),
            per_layer((H_A, HD_A, HD_A)), per_layer((H_A, HD_A, HD_A)),
            per_layer((1, W_A)), per_layer((1, W_A)), per_layer((1, W_A)),
            per_layer((1, W_C)), per_layer((1, W_C)), per_layer((H_C, CHUNK, CHUNK)),
            per_layer((CHUNK, H_C)),
        ],
        out_specs=[
            pair(W_A), pair(W_C), pair(W_B),
            pl.BlockSpec((None, A_TAIL, W_A), lambda s: (s // pairs_per_seq, 0, 0)),
            pl.BlockSpec((None, 1, W_A), lambda s: (s // pairs_per_seq, 0, 0)),
            pl.BlockSpec((ns, D_IN), lambda s: (0, 0)),
        ],
        out_shape=[
            jax.ShapeDtypeStruct((rows, W_A), BF16),
            jax.ShapeDtypeStruct((rows, W_C), BF16),
            jax.ShapeDtypeStruct((rows, W_B), F32),
            jax.ShapeDtypeStruct((nb, A_TAIL, W_A), F32),
            jax.ShapeDtypeStruct((nb, 1, W_A), F32),
            jax.ShapeDtypeStruct((ns, D_IN), F32),
        ],
        scratch_shapes=scratch,
        compiler_params=pltpu.CompilerParams(
            dimension_semantics=("arbitrary",),
            vmem_limit_bytes=_vmem_limit(
                [3 * tl * D_MODEL * 4, 2 * tl * (W_A + W_C) * 2, 2 * tl * W_B * 4, 1024 * 1024,
                 ns * D_MODEL * 4, ns * D_IN * 4],
                D_MODEL * D_IN * 2 + scratch_bytes, tl * D_IN * 4 + 4 * tl * W_A * 4),
        ),
        name="front",
    )(x, x, xs, g, w, p["conv_a_w"], p["conv_a_b"], p["gate_r_w"], p["gate_i_w"], p["gate_r_b"],
      p["gate_i_b"], p["lru_lambda"], p["sgu_ln_g"], p["sgu_ln_b"], p["sgu_w"], p["sgu_b_t"])


CONV_B_ROWS = 64
NORM_B_ROWS = 32
BACK_ROWS = 256
BACK_COLS = 2048


def _conv_b_tile(w_ref, ub_slab, conv_slab, r0, c, rows):
    first = B_TAIL - (K_B - 1)
    half = rows // 2
    cs = slice(c * LANES, (c + 1) * LANES)
    even = odd = None
    for kp in range(first, first + K_B + 1):
        x = ub_slab[c, pl.ds(r0 + kp, half, stride=2), :]
        if kp < first + K_B:
            term = w_ref[kp - first:kp - first + 1, cs] * x
            even = term if even is None else even + term
        if kp > first:
            term = w_ref[kp - first - 1:kp - first, cs] * x
            odd = term if odd is None else odd + term
    conv_slab[c, pl.ds(r0, half, stride=2), :] = even
    conv_slab[c, pl.ds(r0 + 1, half, stride=2), :] = odd


def _back_kernel(x_ref, ya_ref, yc_ref, ub_ref, ub_prev_ref, xs_ref, mixs_ref, w32_ref, cbw_ref, lbg_ref,
                 lbb_ref, o_ref, os_ref, ub_slab, conv_slab, yb_s, w_ref, *, tm, tiles_per_seq):
    i = pl.program_id(0)

    @pl.when(i == 0)
    def _():
        for c0 in range(0, D_MODEL, PROJ_COLS):
            w_ref[:, c0:c0 + PROJ_COLS] = w32_ref[:, c0:c0 + PROJ_COLS].astype(BF16)
        os_ref[...] = xs_ref[...] + jnp.dot(mixs_ref[...], w_ref[...], preferred_element_type=F32)

    n_slabs = W_B // LANES
    prev = jnp.where(i % tiles_per_seq == 0, 0.0, ub_prev_ref[...])
    for c in range(n_slabs):
        ub_slab[c, 0:B_TAIL, :] = prev[:, c * LANES:(c + 1) * LANES]
        ub_slab[c, B_TAIL:, :] = ub_ref[:, c * LANES:(c + 1) * LANES]

    def conv_rows(r):
        n = CONV_B_ROWS
        for c in range(n_slabs):
            _conv_b_tile(cbw_ref, ub_slab, conv_slab, r, c, n)
        for r1 in range(r, r + n, NORM_B_ROWS):
            conv = jnp.concatenate([conv_slab[c, r1:r1 + NORM_B_ROWS, :] for c in range(n_slabs)], axis=-1)
            y_b = _layer_norm(conv, lbg_ref[...], lbb_ref[...])
            yb_s[r1:r1 + NORM_B_ROWS, :] = jax.nn.silu(y_b).astype(BF16)

    def project(r, c0):
        rs, cs = slice(r, r + BACK_ROWS), slice(c0, c0 + BACK_COLS)
        o_ref[rs, cs] = (
            x_ref[rs, cs]
            + jnp.dot(ya_ref[rs, :], w_ref[OFF_YA:OFF_YA + W_A, cs], preferred_element_type=F32)
            + jnp.dot(yb_s[rs, :], w_ref[OFF_YB:OFF_YB + W_B, cs], preferred_element_type=F32)
            + jnp.dot(yc_ref[rs, :], w_ref[OFF_YC:OFF_YC + W_C, cs], preferred_element_type=F32))

    fillers = [(functools.partial(project, r, c0), (r + BACK_ROWS) / tm)
               for r in range(0, tm, BACK_ROWS) for c0 in range(0, D_MODEL, BACK_COLS)]
    _interleave([(CONV_B_ROWS, functools.partial(conv_rows, r)) for r in range(0, tm, CONV_B_ROWS)],
                fillers)


def _back(x, ya, yc, ub, xs, mix_s, w, p, layer, seq, tm):
    rows, ns = x.shape[0], xs.shape[0]
    assert seq % tm == 0 and tm % B_TAIL == 0 and tm % CONV_B_ROWS == 0 and tm % BACK_ROWS == 0
    tails_per_tile = tm // B_TAIL

    def per_layer(shape):
        return pl.BlockSpec((None,) + shape, lambda i: (layer,) + (0,) * len(shape))

    def tile(width):
        return pl.BlockSpec((tm, width), lambda i: (i, 0))

    def whole(width):
        return pl.BlockSpec((ns, width), lambda i: (0, 0))

    return pl.pallas_call(
        functools.partial(_back_kernel, tm=tm, tiles_per_seq=seq // tm),
        grid=(rows // tm,),
        in_specs=[
            tile(D_MODEL), tile(W_A), tile(W_C), tile(W_B),
            pl.BlockSpec((B_TAIL, W_B), lambda i: (jnp.maximum(i * tails_per_tile - 1, 0), 0)),
            whole(D_MODEL), whole(D_MODEL),
            pl.BlockSpec((None, D_MODEL, D_MODEL), lambda i: (layer, 0, 0), pipeline_mode=pl.Buffered(1)),
            per_layer((K_B, W_B)), per_layer((1, W_B)), per_layer((1, W_B)),
        ],
        out_specs=[tile(D_MODEL), whole(D_MODEL)],
        out_shape=[jax.ShapeDtypeStruct((rows, D_MODEL), F32), jax.ShapeDtypeStruct((ns, D_MODEL), F32)],
        scratch_shapes=[pltpu.VMEM((W_B // LANES, B_TAIL + tm, LANES), F32),
                        pltpu.VMEM((W_B // LANES, tm, LANES), F32),
                        pltpu.VMEM((tm, W_B), BF16), pltpu.VMEM((D_MODEL, D_MODEL), BF16)],
        compiler_params=pltpu.CompilerParams(
            dimension_semantics=("arbitrary",),
            vmem_limit_bytes=_vmem_limit(
                [tm * D_MODEL * 4, tm * (W_A + W_C) * 2, tm * W_B * 4, tm * D_MODEL * 4, 256 * 1024,
                 ns * D_MODEL * 10],
                D_MODEL * D_MODEL * (4 + 2) + (B_TAIL + 2 * tm) * W_B * 4 + tm * W_B * 2,
                2 * BACK_ROWS * BACK_COLS * 4),
        ),
        name="back",
    )(x, ya, yc, ub, ub, xs, mix_s, w, p["conv_b_w"], p["ln_b_g"], p["ln_b_b"])


def _mixer_sample_kernel(
        z_ref, ca_ref, h0_ref, cb_ref, caw_ref, cab_ref, gwr_ref, gwi_ref, br_ref, bi_ref, lam_ref,
        cbw_ref, lbg_ref, lbb_ref, sg_ref, sb_ref, sw0_ref, sb0_ref,
        mix_ref, ca_out_ref, h_out_ref, cb_out_ref, vn_out_ref):
    xa = z_ref[:, OFF_XA:OFF_XA + W_A]
    for k in range(K_A - 2):
        ca_out_ref[k] = ca_ref[k + 1]
    ca_out_ref[K_A - 2] = xa
    sp = _softplus(-lam_ref[...])
    for h in range(H_A):
        cs = slice(h * HD_A, (h + 1) * HD_A)
        conv_h = cab_ref[:, cs] + caw_ref[K_A - 1:K_A, cs] * xa[:, cs]
        for k in range(K_A - 1):
            conv_h = conv_h + caw_ref[k:k + 1, cs] * ca_ref[k, :, cs]
        conv_b = conv_h.astype(BF16)
        pre = jnp.concatenate(
            [jnp.dot(conv_b, gwr_ref[h].astype(BF16), preferred_element_type=F32),
             jnp.dot(conv_b, gwi_ref[h].astype(BF16), preferred_element_type=F32)], axis=-1)
        a, u = _lru_gates(conv_h, pre, br_ref[:, cs], bi_ref[:, cs], sp[:, cs])
        h_new = a * h0_ref[:, cs] + u
        h_out_ref[:, cs] = h_new
        mix_ref[:, OFF_YA + h * HD_A:OFF_YA + (h + 1) * HD_A] = (
            h_new * jax.nn.gelu(z_ref[:, OFF_GA + h * HD_A:OFF_GA + (h + 1) * HD_A])).astype(BF16)

    ub = z_ref[:, OFF_XB:OFF_XB + W_B] * jax.nn.sigmoid(z_ref[:, OFF_GB:OFF_GB + W_B])
    acc = cbw_ref[K_B - 1:K_B, :] * ub
    for k in range(K_B - 1):
        acc = acc + cbw_ref[k:k + 1, :] * cb_ref[k]
    for k in range(K_B - 2):
        cb_out_ref[k] = cb_ref[k + 1]
    cb_out_ref[K_B - 2] = ub
    y_b = _layer_norm(acc, lbg_ref[...], lbb_ref[...])
    mix_ref[:, OFF_YB:OFF_YB + W_B] = jax.nn.silu(y_b).astype(BF16)

    g_c = jax.nn.gelu(z_ref[:, OFF_ZC:OFF_ZC + 2 * W_C])
    v_n = _layer_norm(g_c[:, W_C:], sg_ref[...], sb_ref[...])
    vn_out_ref[...] = v_n
    mix_ref[:, OFF_YC:OFF_YC + W_C] = (g_c[:, :W_C] * (sw0_ref[...] * v_n + sb0_ref[...])).astype(BF16)


def _mixer_sample(z, ca_t, h0, cb_t, p, layer, tb):
    nb = z.shape[0]

    def per_layer(shape):
        return pl.BlockSpec((None,) + shape, lambda b: (layer,) + (0,) * len(shape))

    return pl.pallas_call(
        _mixer_sample_kernel,
        grid=(nb // tb,),
        in_specs=[
            pl.BlockSpec((tb, D_IN), lambda b: (b, 0)),
            pl.BlockSpec((None, K_A - 1, tb, W_A), lambda b: (layer, 0, b, 0)),
            pl.BlockSpec((None, tb, W_A), lambda b: (layer, b, 0)),
            pl.BlockSpec((None, K_B - 1, tb, W_B), lambda b: (layer, 0, b, 0)),
            per_layer((K_A, W_A)), per_layer((1, W_A)),
            per_layer((H_A, HD_A, HD_A)), per_layer((H_A, HD_A, HD_A)),
            per_layer((1, W_A)), per_layer((1, W_A)), per_layer((1, W_A)),
            per_layer((K_B, W_B)), per_layer((1, W_B)), per_layer((1, W_B)),
            per_layer((1, W_C)), per_layer((1, W_C)), per_layer((1, W_C)), per_layer((1, W_C)),
        ],
        out_specs=[
            pl.BlockSpec((tb, D_MODEL), lambda b: (b, 0)),
            pl.BlockSpec((K_A - 1, tb, W_A), lambda b: (0, b, 0)),
            pl.BlockSpec((tb, W_A), lambda b: (b, 0)),
            pl.BlockSpec((K_B - 1, tb, W_B), lambda b: (0, b, 0)),
            pl.BlockSpec((tb, W_C), lambda b: (b, 0)),
        ],
        out_shape=[
            jax.ShapeDtypeStruct((nb, D_MODEL), BF16),
            jax.ShapeDtypeStruct((K_A - 1, nb, W_A), F32),
            jax.ShapeDtypeStruct((nb, W_A), F32),
            jax.ShapeDtypeStruct((K_B - 1, nb, W_B), F32),
            jax.ShapeDtypeStruct((nb, W_C), F32),
        ],
        compiler_params=pltpu.CompilerParams(
            dimension_semantics=("arbitrary",),
            vmem_limit_bytes=_vmem_limit(
                [tb * D_IN * 4, 2 * (K_A - 1) * tb * W_A * 4, 2 * tb * W_A * 4,
                 2 * (K_B - 1) * tb * W_B * 4, tb * D_MODEL * 2, tb * W_C * 4, 1024 * 1024],
                0, 8 * tb * W_A * 4),
        ),
        name="mixer_sample",
    )(z, ca_t, h0, cb_t, p["conv_a_w"], p["conv_a_b"], p["gate_r_w"], p["gate_i_w"], p["gate_r_b"],
      p["gate_i_b"], p["lru_lambda"], p["conv_b_w"], p["ln_b_g"], p["ln_b_b"], p["sgu_ln_g"], p["sgu_ln_b"],
      p["sgu_w00"], p["sgu_b0"])


def kernel(x_prompt, x_sample, state_conv_a, state_lru_h, state_conv_b, norm_mix, w_in, conv_a_w,
           conv_a_b, gate_r_w, gate_r_b, gate_i_w, gate_i_b, lru_lambda, conv_b_w, ln_b_g, ln_b_b,
           sgu_ln_g, sgu_ln_b, sgu_w, sgu_b, w_out, norm_ffn, w_ff1, w_ff2, norm_final):
    depth = w_in.shape[0]
    nb, seq, _ = x_prompt.shape
    ns = x_sample.shape[0]

    def row(v):
        return v[:, None, :]

    p = {
        "conv_a_w": conv_a_w, "conv_a_b": row(conv_a_b),
        "gate_r_w": gate_r_w, "gate_i_w": gate_i_w,
        "gate_r_b": row(gate_r_b), "gate_i_b": row(gate_i_b), "lru_lambda": row(lru_lambda),
        "conv_b_w": conv_b_w, "ln_b_g": row(ln_b_g), "ln_b_b": row(ln_b_b),
        "sgu_ln_g": row(sgu_ln_g), "sgu_ln_b": row(sgu_ln_b),
        "sgu_w": sgu_w, "sgu_b_t": jnp.swapaxes(sgu_b, 1, 2),
        "sgu_w00": row(jnp.repeat(sgu_w[:, :, 0, 0], HD_C, axis=-1)),
        "sgu_b0": row(jnp.repeat(sgu_b[:, :, 0], HD_C, axis=-1)),
    }
    g_mix, g_ffn, g_final = row(norm_mix), row(norm_ffn), norm_final[None, :]
    w_in_b = w_in.astype(BF16)

    ca_t = jnp.swapaxes(state_conv_a, 1, 2)
    cb_t = jnp.swapaxes(state_conv_b, 1, 2)

    xp = x_prompt.reshape(nb * seq, D_MODEL)
    xs = x_sample.reshape(ns, D_MODEL)
    ca_p, h_p, cb_p, ca_s, h_s, cb_s, v_s = [], [], [], [], [], [], []
    for l in range(depth):
        last = g_final if l == depth - 1 else None
        ya, yc, ub, ca_l, h_l, zs = _front(xp, xs, g_mix, w_in_b, p, l, nb, seq, tl=256)
        ca_p.append(ca_l[:, A_TAIL - (K_A - 1):, :])
        h_p.append(h_l[:, 0, :])
        cb_p.append(ub.reshape(nb, seq, W_B)[:, seq - (K_B - 1):, :])
        mix_s, cas_l, hs_l, cbs_l, v_l = _mixer_sample(zs, ca_t, state_lru_h, cb_t, p, l, tb=32)
        ca_s.append(jnp.swapaxes(cas_l, 0, 1))
        h_s.append(hs_l)
        cb_s.append(jnp.swapaxes(cbs_l, 0, 1))
        v_s.append(v_l[:, None, :])
        xp, xs = _back(xp, ya, yc, ub, xs, mix_s, w_out, p, l, seq, tm=512)
        xp, xs = _ffn(xp, xs, g_ffn, w_ff1, w_ff2, l, tm=1024, tf=512, g_final=last)

    return (xp.reshape(nb, seq, D_MODEL), xs.reshape(ns, 1, D_MODEL),
            jnp.stack(ca_p), jnp.stack(h_p), jnp.stack(cb_p),
            jnp.stack(ca_s), jnp.stack(h_s), jnp.stack(cb_s), jnp.stack(v_s))
```

```python
import functools

import jax
import jax.numpy as jnp
from jax import lax
from jax.experimental import pallas as pl
from jax.experimental.pallas import tpu as pltpu

F32 = jnp.float32
BF16 = jnp.bfloat16

D_MODEL = 2048
W_A = 1024
H_A = 8
HD_A = W_A // H_A
K_A = 4
LRU_C = 8.0
W_B = 512
K_B = 31
W_C = 512
H_C = 4
HD_C = W_C // H_C
CHUNK = 128
D_IN = 2 * (W_A + W_B + W_C)
D_FF = 4 * D_MODEL
EPS = 1e-6

OFF_XA, OFF_GA, OFF_XB, OFF_GB, OFF_ZC = 0, W_A, 2 * W_A, 2 * W_A + W_B, 2 * W_A + 2 * W_B
OFF_YA, OFF_YB, OFF_YC = 0, W_A, W_A + W_B

SUBLANES = 8
LANES = 128
A_TAIL = SUBLANES
B_TAIL = 32
V7X_VMEM_BYTES = 64 * 1024 * 1024


def _vmem_limit(block_bytes, scratch_bytes, temp_bytes):
    need = 2 * sum(block_bytes) + scratch_bytes + temp_bytes
    return int(min(need + need // 8, V7X_VMEM_BYTES - 4 * 1024 * 1024))


def _rms(x, g):
    return (x * lax.rsqrt(jnp.mean(x * x, axis=-1, keepdims=True) + EPS)) * g


def _layer_norm(x, g, b):
    xc = x - jnp.mean(x, axis=-1, keepdims=True)
    y = xc * lax.rsqrt(jnp.mean(xc * xc, axis=-1, keepdims=True) + EPS)
    return y * g + b


def _softplus(x):
    return jnp.maximum(x, 0.0) + jnp.log1p(jnp.exp(-jnp.abs(x)))


def _lru_gates(conv_h, pre, b_r, b_i, sp):
    r = jax.nn.sigmoid(pre[:, :HD_A] + b_r)
    i = jax.nn.sigmoid(pre[:, HD_A:] + b_i)
    log_a = (-LRU_C * r) * sp
    a = jnp.exp(log_a)
    t = jnp.tanh(log_a)
    u = jnp.sqrt(-2.0 * t / (1.0 - t)) * (i * conv_h)
    return a, u


def _interleave(pieces, fillers):
    total = sum(cost for cost, _ in pieces)
    done, issued = 0, 0
    for cost, piece in pieces:
        while issued < len(fillers) and fillers[issued][1] * total <= done:
            fillers[issued][0]()
            issued += 1
        piece()
        done += cost
    for filler, _ in fillers[issued:]:
        filler()


FFN_COLS = 512


def _ffn_kernel(*refs, final_norm, with_sample):
    refs = list(refs)
    hf_ref = refs.pop()
    xp_ref = refs[0]
    n_x = 1
    if with_sample:
        xs_ref = refs[1]
        os_ref = refs.pop()
        n_x = 2
    op_ref = refs.pop()
    gf_ref = refs.pop() if final_norm else None
    g_ref, w1_ref, w2_ref = refs[n_x:n_x + 3]
    j = pl.program_id(1)
    tm = xp_ref.shape[0]
    outs = [(op_ref, slice(0, tm))] + ([(os_ref, slice(tm, None))] if with_sample else [])
    srcs = [xp_ref] + ([xs_ref] if with_sample else [])

    @pl.when(j == 0)
    def _():
        for x_ref, (o_ref, rs) in zip(srcs, outs):
            x = x_ref[...]
            hf_ref[rs, :] = _rms(x, g_ref[...]).astype(BF16)
            o_ref[...] = x

    h = jnp.dot(hf_ref[...], w1_ref[...].astype(BF16), preferred_element_type=F32)
    h = jnp.square(jnp.maximum(h, 0.0)).astype(BF16)
    for c0 in range(0, D_MODEL, FFN_COLS):
        cs = slice(c0, c0 + FFN_COLS)
        acc = jnp.dot(h, w2_ref[:, cs].astype(BF16), preferred_element_type=F32)
        for o_ref, rs in outs:
            o_ref[:, cs] += acc[rs, :]

    if final_norm:
        @pl.when(j == pl.num_programs(1) - 1)
        def _():
            for o_ref, _ in outs:
                o_ref[...] = _rms(o_ref[...], gf_ref[...])


def _ffn(xp, xs, g, w1, w2, layer, tm, tf, g_final=None):
    m, ns = xp.shape[0], xs.shape[0]
    final_norm = g_final is not None

    def call(with_sample, n_tiles, lead_args, lead_specs, out_specs, out_shape, aliases):
        rows = tm + (ns if with_sample else 0)
        in_specs = lead_specs + [
            pl.BlockSpec((None, 1, D_MODEL), lambda i, j: (layer, 0, 0)),
            pl.BlockSpec((None, D_MODEL, tf), lambda i, j: (layer, 0, j)),
            pl.BlockSpec((None, tf, D_MODEL), lambda i, j: (layer, j, 0)),
        ]
        args = lead_args + [g, w1, w2]
        if final_norm:
            in_specs.append(pl.BlockSpec((1, D_MODEL), lambda i, j: (0, 0)))
            args.append(g_final)
        return pl.pallas_call(
            functools.partial(_ffn_kernel, final_norm=final_norm, with_sample=with_sample),
            grid=(n_tiles, D_FF // tf),
            in_specs=in_specs,
            out_specs=out_specs,
            out_shape=out_shape,
            input_output_aliases=aliases,
            scratch_shapes=[pltpu.VMEM((rows, D_MODEL), BF16)],
            compiler_params=pltpu.CompilerParams(
                dimension_semantics=("arbitrary", "arbitrary"),
                vmem_limit_bytes=_vmem_limit(
                    [D_MODEL * 4, D_MODEL * tf * 4, tf * D_MODEL * 4, 2 * tm * D_MODEL * 4,
                     2 * ns * D_MODEL * 4],
                    rows * D_MODEL * 2,
                    rows * tf * 6 + rows * FFN_COLS * 4 + 2 * D_MODEL * tf * 2),
            ),
            name="ffn_head" if with_sample else "ffn_tail",
        )(*args)

    def x_tile(first_tile):
        return pl.BlockSpec((tm, D_MODEL), lambda i, j: (i + first_tile, 0))

    def o_tile(first_tile):
        return pl.BlockSpec((tm, D_MODEL), lambda i, j: (i + first_tile, 0))

    whole = pl.BlockSpec((ns, D_MODEL), lambda i, j: (0, 0))
    full = jax.ShapeDtypeStruct((m, D_MODEL), F32)
    op, os = call(True, 1, [xp, xs], [x_tile(0), whole], [o_tile(0), whole],
                  [full, jax.ShapeDtypeStruct((ns, D_MODEL), F32)], {0: 0})
    op = call(False, m // tm - 1, [op], [x_tile(1)], o_tile(1), full, {0: 0})
    return op, os


N_FRONT_PARAMS = 11
N_FRONT_SCRATCH = 10
PROJ_COLS = 512


def _pack_gate_weights(gwr_ref, gwi_ref, gw_ref):
    for h in range(H_A):
        gw_ref[h, :, 0:HD_A] = gwr_ref[h].astype(BF16)
        gw_ref[h, :, HD_A:] = gwi_ref[h].astype(BF16)


def _front_pieces(z_ref, outs, r_out, prm, scr, tl):
    caw_ref, cab_ref, _, _, br_ref, bi_ref, _, sg_ref, sb_ref, _, sbt_ref = prm
    xa_buf, a_s, u_s, _, g_s, _, vn_s, wm_s, sp_s, gw_ref = scr
    _, yc_ref, ub_ref = outs
    pieces = []

    def c_norm(r):
        v = jax.nn.gelu(z_ref[r:r + CHUNK, OFF_ZC + W_C:OFF_ZC + 2 * W_C])
        vn_s[r:r + CHUNK, :] = _layer_norm(v, sg_ref[...], sb_ref[...]).astype(BF16)

    def c_head(r, h):
        cs = slice(h * HD_C, (h + 1) * HD_C)
        mixed = jnp.dot(wm_s[h], vn_s[r:r + CHUNK, cs], preferred_element_type=F32)
        mixed = mixed + sbt_ref[:, h:h + 1]
        u_c = jax.nn.gelu(z_ref[r:r + CHUNK, OFF_ZC + h * HD_C:OFF_ZC + (h + 1) * HD_C])
        yc_ref[r_out + r:r_out + r + CHUNK, cs] = (u_c * mixed).astype(BF16)

    def glu(r):
        ub_ref[r_out + r:r_out + r + CHUNK, :] = (
            z_ref[r:r + CHUNK, OFF_XB:OFF_XB + W_B]
            * jax.nn.sigmoid(z_ref[r:r + CHUNK, OFF_GB:OFF_GB + W_B]))

    def a_head(h):
        cs = slice(h * HD_A, (h + 1) * HD_A)
        xa_buf[A_TAIL:A_TAIL + tl, cs] = z_ref[:, OFF_XA + h * HD_A:OFF_XA + (h + 1) * HD_A]
        xa_full = xa_buf[0:A_TAIL + tl, cs]
        conv_h = cab_ref[:, cs] + caw_ref[K_A - 1:K_A, cs] * xa_full[A_TAIL:, :]
        for k in range(K_A - 1):
            shifted = pltpu.roll(xa_full, K_A - 1 - k, axis=0)[A_TAIL:, :]
            conv_h = conv_h + caw_ref[k:k + 1, cs] * shifted
        pre = jnp.dot(conv_h.astype(BF16), gw_ref[h], preferred_element_type=F32)
        a, u = _lru_gates(conv_h, pre, br_ref[:, cs], bi_ref[:, cs], sp_s[:, cs])
        a_s[:, cs] = a
        u_s[:, cs] = u
        g_s[:, cs] = jax.nn.gelu(z_ref[:, OFF_GA + h * HD_A:OFF_GA + (h + 1) * HD_A])

    other = []
    for r in range(0, tl, CHUNK):
        other.append((1500, functools.partial(c_norm, r)))
        for h in range(H_C):
            other.append((500, functools.partial(c_head, r, h)))
        other.append((600, functools.partial(glu, r)))
    per_gate = -(-len(other) // H_A)
    for h in range(H_A):
        pieces.append((2000 * tl // 256, functools.partial(a_head, h)))
        pieces.extend(other[h * per_gate:(h + 1) * per_gate])
    return pieces


def _front_kernel(x_ref, x_next_ref, xs_ref, g_ref, w_ref, *rest, tl, chunks_per_seq):
    prm = rest[:N_FRONT_PARAMS]
    gwr_ref, gwi_ref, lam_ref, sw_ref = prm[2], prm[3], prm[6], prm[9]
    ya_ref, yc_ref, ub_ref, ca_out_ref, h_out_ref, zs_ref = rest[N_FRONT_PARAMS:N_FRONT_PARAMS + 6]
    z_even, z_odd, hn_s = rest[N_FRONT_PARAMS + 6:N_FRONT_PARAMS + 9]
    scr = rest[N_FRONT_PARAMS + 9:]
    xa_buf, a_s, u_s, h_s, g_s, carry, _, wm_s, sp_s, gw_s = scr
    outs = (ya_ref, yc_ref, ub_ref)
    s = pl.program_id(0)

    def normalise(x_rows_ref, r0):
        hn_s[...] = _rms(x_rows_ref[r0:r0 + tl, :], g_ref[...]).astype(BF16)

    def project_and_mix(x_rows_ref, x_r0, z_next, z_cur, r_out):
        normalise(x_rows_ref, x_r0)

        def project(c0):
            z_next[:, c0:c0 + PROJ_COLS] = jnp.dot(
                hn_s[...], w_ref[:, c0:c0 + PROJ_COLS], preferred_element_type=F32)

        _interleave(_front_pieces(z_cur, outs, r_out, prm, scr, tl),
                    [(functools.partial(project, c0), c0 / D_IN) for c0 in range(0, D_IN, PROJ_COLS)])
        xa_buf[0:A_TAIL, :] = xa_buf[tl:tl + A_TAIL, :]

        def scan_row(t, h):
            h = a_s[pl.ds(t, 1), :] * h + u_s[pl.ds(t, 1), :]
            h_s[pl.ds(t, 1), :] = h
            return h

        carry[...] = lax.fori_loop(0, tl, scan_row, carry[...], unroll=8)
        ya_ref[r_out:r_out + tl, :] = (h_s[...] * g_s[...]).astype(BF16)

    @pl.when(s == 0)
    def _():
        causal = (lax.broadcasted_iota(jnp.int32, (CHUNK, CHUNK), 0)
                  >= lax.broadcasted_iota(jnp.int32, (CHUNK, CHUNK), 1))
        for h in range(H_C):
            wm_s[h] = jnp.where(causal, sw_ref[h], 0.0).astype(BF16)
        sp_s[...] = _softplus(-lam_ref[...])
        _pack_gate_weights(gwr_ref, gwi_ref, gw_s)
        zs_ref[...] = jnp.dot(_rms(xs_ref[...], g_ref[...]).astype(BF16), w_ref[...],
                              preferred_element_type=F32)
        normalise(x_ref, 0)
        z_even[...] = jnp.dot(hn_s[...], w_ref[...], preferred_element_type=F32)

    @pl.when(s % (chunks_per_seq // 2) == 0)
    def _():
        xa_buf[0:A_TAIL, :] = jnp.zeros((A_TAIL, W_A), F32)
        carry[...] = jnp.zeros((1, W_A), F32)

    project_and_mix(x_ref, tl, z_odd, z_even, 0)
    project_and_mix(x_next_ref, 0, z_even, z_odd, tl)
    ca_out_ref[...] = xa_buf[0:A_TAIL, :]
    h_out_ref[...] = carry[...]


def _front(x, xs, g, w, p, layer, nb, seq, tl):
    ns = xs.shape[0]
    chunks_per_seq = seq // tl
    assert seq % tl == 0 and chunks_per_seq % 2 == 0 and tl % CHUNK == 0
    pairs = nb * chunks_per_seq // 2
    pairs_per_seq = chunks_per_seq // 2
    rows = nb * seq

    def per_layer(shape):
        return pl.BlockSpec((None,) + shape, lambda s: (layer,) + (0,) * len(shape))

    def pair(width):
        return pl.BlockSpec((2 * tl, width), lambda s: (s, 0))

    scratch = [
        pltpu.VMEM((tl, D_IN), F32), pltpu.VMEM((tl, D_IN), F32), pltpu.VMEM((tl, D_MODEL), BF16),
        pltpu.VMEM((A_TAIL + tl, W_A), F32),
        pltpu.VMEM((tl, W_A), F32), pltpu.VMEM((tl, W_A), F32), pltpu.VMEM((tl, W_A), F32),
        pltpu.VMEM((tl, W_A), F32), pltpu.VMEM((1, W_A), F32),
        pltpu.VMEM((tl, W_C), BF16), pltpu.VMEM((H_C, CHUNK, CHUNK), BF16), pltpu.VMEM((1, W_A), F32),
        pltpu.VMEM((H_A, HD_A, 2 * HD_A), BF16),
    ]
    assert len(scratch) == 3 + N_FRONT_SCRATCH
    scratch_bytes = (2 * tl * D_IN * 4 + tl * D_MODEL * 2
                     + (A_TAIL + 5 * tl) * W_A * 4 + tl * W_C * 2 + H_C * CHUNK * CHUNK * 2)
    return pl.pallas_call(
        functools.partial(_front_kernel, tl=tl, chunks_per_seq=chunks_per_seq),
        grid=(pairs,),
        in_specs=[
            pair(D_MODEL),
            pl.BlockSpec((tl, D_MODEL), lambda s: (jnp.minimum(2 * s + 2, 2 * pairs - 1), 0)),
            pl.BlockSpec((ns, D_MODEL), lambda s: (0, 0)),
            pl.BlockSpec((None, 1, D_MODEL), lambda s: (layer, 0, 0)),
            pl.BlockSpec((None, D_MODEL, D_IN), lambda s: (layer, 0, 0), pipeline_mode=pl.Buffered(1)),
            per_layer((K_A, W_A)), per_layer((1, W_A)),
            per_layer((H_A, HD_A, HD_A)), per_layer((H_A, HD_A, HD_A)),
            per_layer((1, W_A)), per_layer((1, W_A)), per_layer((1, W_A)),
            per_layer((1, W_C)), per_layer((1, W_C)), per_layer((H_C, CHUNK, CHUNK)),
            per_layer((CHUNK, H_C)),
        ],
        out_specs=[
            pair(W_A), pair(W_C), pair(W_B),
            pl.BlockSpec((None, A_TAIL, W_A), lambda s: (s // pairs_per_seq, 0, 0)),
            pl.BlockSpec((None, 1, W_A), lambda s: (s // pairs_per_seq, 0, 0)),
            pl.BlockSpec((ns, D_IN), lambda s: (0, 0)),
        ],
        out_shape=[
            jax.ShapeDtypeStruct((rows, W_A), BF16),
            jax.ShapeDtypeStruct((rows, W_C), BF16),
            jax.ShapeDtypeStruct((rows, W_B), F32),
            jax.ShapeDtypeStruct((nb, A_TAIL, W_A), F32),
            jax.ShapeDtypeStruct((nb, 1, W_A), F32),
            jax.ShapeDtypeStruct((ns, D_IN), F32),
        ],
        scratch_shapes=scratch,
        compiler_params=pltpu.CompilerParams(
            dimension_semantics=("arbitrary",),
            vmem_limit_bytes=_vmem_limit(
                [3 * tl * D_MODEL * 4, 2 * tl * (W_A + W_C) * 2, 2 * tl * W_B * 4, 1024 * 1024,
                 ns * D_MODEL * 4, ns * D_IN * 4],
                D_MODEL * D_IN * 2 + scratch_bytes, tl * D_IN * 4 + 4 * tl * W_A * 4),
        ),
        name="front",
    )(x, x, xs, g, w, p["conv_a_w"], p["conv_a_b"], p["gate_r_w"], p["gate_i_w"], p["gate_r_b"],
      p["gate_i_b"], p["lru_lambda"], p["sgu_ln_g"], p["sgu_ln_b"], p["sgu_w"], p["sgu_b_t"])


CONV_B_ROWS = 64
NORM_B_ROWS = 32
BACK_ROWS = 256
BACK_COLS = 2048


def _conv_b_tile(w_ref, ub_slab, conv_slab, r0, c, rows):
    first = B_TAIL - (K_B - 1)
    half = rows // 2
    cs = slice(c * LANES, (c + 1) * LANES)
    even = odd = None
    for kp in range(first, first + K_B + 1):
        x = ub_slab[c, pl.ds(r0 + kp, half, stride=2), :]
        if kp < first + K_B:
            term = w_ref[kp - first:kp - first + 1, cs] * x
            even = term if even is None else even + term
        if kp > first:
            term = w_ref[kp - first - 1:kp - first, cs] * x
            odd = term if odd is None else odd + term
    conv_slab[c, pl.ds(r0, half, stride=2), :] = even
    conv_slab[c, pl.ds(r0 + 1, half, stride=2), :] = odd


def _back_kernel(x_ref, ya_ref, yc_ref, ub_ref, ub_prev_ref, xs_ref, mixs_ref, w32_ref, cbw_ref, lbg_ref,
                 lbb_ref, o_ref, os_ref, ub_slab, conv_slab, yb_s, w_ref, *, tm, tiles_per_seq):
    i = pl.program_id(0)

    @pl.when(i == 0)
    def _():
        for c0 in range(0, D_MODEL, PROJ_COLS):
            w_ref[:, c0:c0 + PROJ_COLS] = w32_ref[:, c0:c0 + PROJ_COLS].astype(BF16)
        os_ref[...] = xs_ref[...] + jnp.dot(mixs_ref[...], w_ref[...], preferred_element_type=F32)

    n_slabs = W_B // LANES
    prev = jnp.where(i % tiles_per_seq == 0, 0.0, ub_prev_ref[...])
    for c in range(n_slabs):
        ub_slab[c, 0:B_TAIL, :] = prev[:, c * LANES:(c + 1) * LANES]
        ub_slab[c, B_TAIL:, :] = ub_ref[:, c * LANES:(c + 1) * LANES]

    def conv_rows(r):
        n = CONV_B_ROWS
        for c in range(n_slabs):
            _conv_b_tile(cbw_ref, ub_slab, conv_slab, r, c, n)
        for r1 in range(r, r + n, NORM_B_ROWS):
            conv = jnp.concatenate([conv_slab[c, r1:r1 + NORM_B_ROWS, :] for c in range(n_slabs)], axis=-1)
            y_b = _layer_norm(conv, lbg_ref[...], lbb_ref[...])
            yb_s[r1:r1 + NORM_B_ROWS, :] = jax.nn.silu(y_b).astype(BF16)

    def project(r, c0):
        rs, cs = slice(r, r + BACK_ROWS), slice(c0, c0 + BACK_COLS)
        o_ref[rs, cs] = (
            x_ref[rs, cs]
            + jnp.dot(ya_ref[rs, :], w_ref[OFF_YA:OFF_YA + W_A, cs], preferred_element_type=F32)
            + jnp.dot(yb_s[rs, :], w_ref[OFF_YB:OFF_YB + W_B, cs], preferred_element_type=F32)
            + jnp.dot(yc_ref[rs, :], w_ref[OFF_YC:OFF_YC + W_C, cs], preferred_element_type=F32))

    fillers = [(functools.partial(project, r, c0), (r + BACK_ROWS) / tm)
               for r in range(0, tm, BACK_ROWS) for c0 in range(0, D_MODEL, BACK_COLS)]
    _interleave([(CONV_B_ROWS, functools.partial(conv_rows, r)) for r in range(0, tm, CONV_B_ROWS)],
                fillers)


def _back(x, ya, yc, ub, xs, mix_s, w, p, layer, seq, tm):
    rows, ns = x.shape[0], xs.shape[0]
    assert seq % tm == 0 and tm % B_TAIL == 0 and tm % CONV_B_ROWS == 0 and tm % BACK_ROWS == 0
    tails_per_tile = tm // B_TAIL

    def per_layer(shape):
        return pl.BlockSpec((None,) + shape, lambda i: (layer,) + (0,) * len(shape))

    def tile(width):
        return pl.BlockSpec((tm, width), lambda i: (i, 0))

    def whole(width):
        return pl.BlockSpec((ns, width), lambda i: (0, 0))

    return pl.pallas_call(
        functools.partial(_back_kernel, tm=tm, tiles_per_seq=seq // tm),
        grid=(rows // tm,),
        in_specs=[
            tile(D_MODEL), tile(W_A), tile(W_C), tile(W_B),
            pl.BlockSpec((B_TAIL, W_B), lambda i: (jnp.maximum(i * tails_per_tile - 1, 0), 0)),
            whole(D_MODEL), whole(D_MODEL),
            pl.BlockSpec((None, D_MODEL, D_MODEL), lambda i: (layer, 0, 0), pipeline_mode=pl.Buffered(1)),
            per_layer((K_B, W_B)), per_layer((1, W_B)), per_layer((1, W_B)),
        ],
        out_specs=[tile(D_MODEL), whole(D_MODEL)],
        out_shape=[jax.ShapeDtypeStruct((rows, D_MODEL), F32), jax.ShapeDtypeStruct((ns, D_MODEL), F32)],
        scratch_shapes=[pltpu.VMEM((W_B // LANES, B_TAIL + tm, LANES), F32),
                        pltpu.VMEM((W_B // LANES, tm, LANES), F32),
                        pltpu.VMEM((tm, W_B), BF16), pltpu.VMEM((D_MODEL, D_MODEL), BF16)],
        compiler_params=pltpu.CompilerParams(
            dimension_semantics=("arbitrary",),
            vmem_limit_bytes=_vmem_limit(
                [tm * D_MODEL * 4, tm * (W_A + W_C) * 2, tm * W_B * 4, tm * D_MODEL * 4, 256 * 1024,
                 ns * D_MODEL * 10],
                D_MODEL * D_MODEL * (4 + 2) + (B_TAIL + 2 * tm) * W_B * 4 + tm * W_B * 2,
                2 * BACK_ROWS * BACK_COLS * 4),
        ),
        name="back",
    )(x, ya, yc, ub, ub, xs, mix_s, w, p["conv_b_w"], p["ln_b_g"], p["ln_b_b"])


def _mixer_sample_kernel(
        z_ref, ca_ref, h0_ref, cb_ref, caw_ref, cab_ref, gwr_ref, gwi_ref, br_ref, bi_ref, lam_ref,
        cbw_ref, lbg_ref, lbb_ref, sg_ref, sb_ref, sw0_ref, sb0_ref,
        mix_ref, ca_out_ref, h_out_ref, cb_out_ref, vn_out_ref):
    xa = z_ref[:, OFF_XA:OFF_XA + W_A]
    for k in range(K_A - 2):
        ca_out_ref[k] = ca_ref[k + 1]
    ca_out_ref[K_A - 2] = xa
    sp = _softplus(-lam_ref[...])
    for h in range(H_A):
        cs = slice(h * HD_A, (h + 1) * HD_A)
        conv_h = cab_ref[:, cs] + caw_ref[K_A - 1:K_A, cs] * xa[:, cs]
        for k in range(K_A - 1):
            conv_h = conv_h + caw_ref[k:k + 1, cs] * ca_ref[k, :, cs]
        conv_b = conv_h.astype(BF16)
        pre = jnp.concatenate(
            [jnp.dot(conv_b, gwr_ref[h].astype(BF16), preferred_element_type=F32),
             jnp.dot(conv_b, gwi_ref[h].astype(BF16), preferred_element_type=F32)], axis=-1)
        a, u = _lru_gates(conv_h, pre, br_ref[:, cs], bi_ref[:, cs], sp[:, cs])
        h_new = a * h0_ref[:, cs] + u
        h_out_ref[:, cs] = h_new
        mix_ref[:, OFF_YA + h * HD_A:OFF_YA + (h + 1) * HD_A] = (
            h_new * jax.nn.gelu(z_ref[:, OFF_GA + h * HD_A:OFF_GA + (h + 1) * HD_A])).astype(BF16)

    ub = z_ref[:, OFF_XB:OFF_XB + W_B] * jax.nn.sigmoid(z_ref[:, OFF_GB:OFF_GB + W_B])
    acc = cbw_ref[K_B - 1:K_B, :] * ub
    for k in range(K_B - 1):
        acc = acc + cbw_ref[k:k + 1, :] * cb_ref[k]
    for k in range(K_B - 2):
        cb_out_ref[k] = cb_ref[k + 1]
    cb_out_ref[K_B - 2] = ub
    y_b = _layer_norm(acc, lbg_ref[...], lbb_ref[...])
    mix_ref[:, OFF_YB:OFF_YB + W_B] = jax.nn.silu(y_b).astype(BF16)

    g_c = jax.nn.gelu(z_ref[:, OFF_ZC:OFF_ZC + 2 * W_C])
    v_n = _layer_norm(g_c[:, W_C:], sg_ref[...], sb_ref[...])
    vn_out_ref[...] = v_n
    mix_ref[:, OFF_YC:OFF_YC + W_C] = (g_c[:, :W_C] * (sw0_ref[...] * v_n + sb0_ref[...])).astype(BF16)


def _mixer_sample(z, ca_t, h0, cb_t, p, layer, tb):
    nb = z.shape[0]

    def per_layer(shape):
        return pl.BlockSpec((None,) + shape, lambda b: (layer,) + (0,) * len(shape))

    return pl.pallas_call(
        _mixer_sample_kernel,
        grid=(nb // tb,),
        in_specs=[
            pl.BlockSpec((tb, D_IN), lambda b: (b, 0)),
            pl.BlockSpec((None, K_A - 1, tb, W_A), lambda b: (layer, 0, b, 0)),
            pl.BlockSpec((None, tb, W_A), lambda b: (layer, b, 0)),
            pl.BlockSpec((None, K_B - 1, tb, W_B), lambda b: (layer, 0, b, 0)),
            per_layer((K_A, W_A)), per_layer((1, W_A)),
            per_layer((H_A, HD_A, HD_A)), per_layer((H_A, HD_A, HD_A)),
            per_layer((1, W_A)), per_layer((1, W_A)), per_layer((1, W_A)),
            per_layer((K_B, W_B)), per_layer((1, W_B)), per_layer((1, W_B)),
            per_layer((1, W_C)), per_layer((1, W_C)), per_layer((1, W_C)), per_layer((1, W_C)),
        ],
        out_specs=[
            pl.BlockSpec((tb, D_MODEL), lambda b: (b, 0)),
            pl.BlockSpec((K_A - 1, tb, W_A), lambda b: (0, b, 0)),
            pl.BlockSpec((tb, W_A), lambda b: (b, 0)),
            pl.BlockSpec((K_B - 1, tb, W_B), lambda b: (0, b, 0)),
            pl.BlockSpec((tb, W_C), lambda b: (b, 0)),
        ],
        out_shape=[
            jax.ShapeDtypeStruct((nb, D_MODEL), BF16),
            jax.ShapeDtypeStruct((K_A - 1, nb, W_A), F32),
            jax.ShapeDtypeStruct((nb, W_A), F32),
            jax.ShapeDtypeStruct((K_B - 1, nb, W_B), F32),
            jax.ShapeDtypeStruct((nb, W_C), F32),
        ],
        compiler_params=pltpu.CompilerParams(
            dimension_semantics=("arbitrary",),
            vmem_limit_bytes=_vmem_limit(
                [tb * D_IN * 4, 2 * (K_A - 1) * tb * W_A * 4, 2 * tb * W_A * 4,
                 2 * (K_B - 1) * tb * W_B * 4, tb * D_MODEL * 2, tb * W_C * 4, 1024 * 1024],
                0, 8 * tb * W_A * 4),
        ),
        name="mixer_sample",
    )(z, ca_t, h0, cb_t, p["conv_a_w"], p["conv_a_b"], p["gate_r_w"], p["gate_i_w"], p["gate_r_b"],
      p["gate_i_b"], p["lru_lambda"], p["conv_b_w"], p["ln_b_g"], p["ln_b_b"], p["sgu_ln_g"], p["sgu_ln_b"],
      p["sgu_w00"], p["sgu_b0"])


def kernel(x_prompt, x_sample, state_conv_a, state_lru_h, state_conv_b, norm_mix, w_in, conv_a_w,
           conv_a_b, gate_r_w, gate_r_b, gate_i_w, gate_i_b, lru_lambda, conv_b_w, ln_b_g, ln_b_b,
           sgu_ln_g, sgu_ln_b, sgu_w, sgu_b, w_out, norm_ffn, w_ff1, w_ff2, norm_final):
    depth = w_in.shape[0]
    nb, seq, _ = x_prompt.shape
    ns = x_sample.shape[0]

    def row(v):
        return v[:, None, :]

    p = {
        "conv_a_w": conv_a_w, "conv_a_b": row(conv_a_b),
        "gate_r_w": gate_r_w, "gate_i_w": gate_i_w,
        "gate_r_b": row(gate_r_b), "gate_i_b": row(gate_i_b), "lru_lambda": row(lru_lambda),
        "conv_b_w": conv_b_w, "ln_b_g": row(ln_b_g), "ln_b_b": row(ln_b_b),
        "sgu_ln_g": row(sgu_ln_g), "sgu_ln_b": row(sgu_ln_b),
        "sgu_w": sgu_w, "sgu_b_t": jnp.swapaxes(sgu_b, 1, 2),
        "sgu_w00": row(jnp.repeat(sgu_w[:, :, 0, 0], HD_C, axis=-1)),
        "sgu_b0": row(jnp.repeat(sgu_b[:, :, 0], HD_C, axis=-1)),
    }
    g_mix, g_ffn, g_final = row(norm_mix), row(norm_ffn), norm_final[None, :]
    w_in_b = w_in.astype(BF16)

    ca_t = jnp.swapaxes(state_conv_a, 1, 2)
    cb_t = jnp.swapaxes(state_conv_b, 1, 2)

    xp = x_prompt.reshape(nb * seq, D_MODEL)
    xs = x_sample.reshape(ns, D_MODEL)
    ca_p, h_p, cb_p, ca_s, h_s, cb_s, v_s = [], [], [], [], [], [], []
    for l in range(depth):
        last = g_final if l == depth - 1 else None
        ya, yc, ub, ca_l, h_l, zs = _front(xp, xs, g_mix, w_in_b, p, l, nb, seq, tl=256)
        ca_p.append(ca_l[:, A_TAIL - (K_A - 1):, :])
        h_p.append(h_l[:, 0, :])
        cb_p.append(ub.reshape(nb, seq, W_B)[:, seq - (K_B - 1):, :])
        mix_s, cas_l, hs_l, cbs_l, v_l = _mixer_sample(zs, ca_t, state_lru_h, cb_t, p, l, tb=32)
        ca_s.append(jnp.swapaxes(cas_l, 0, 1))
        h_s.append(hs_l)
        cb_s.append(jnp.swapaxes(cbs_l, 0, 1))
        v_s.append(v_l[:, None, :])
        xp, xs = _back(xp, ya, yc, ub, xs, mix_s, w_out, p, l, seq, tm=512)
        xp, xs = _ffn(xp, xs, g_ffn, w_ff1, w_ff2, l, tm=1024, tf=512, g_final=last)

    return (xp.reshape(nb, seq, D_MODEL), xs.reshape(ns, 1, D_MODEL),
            jnp.stack(ca_p), jnp.stack(h_p), jnp.stack(cb_p),
            jnp.stack(ca_s), jnp.stack(h_s), jnp.stack(cb_s), jnp.stack(v_s))
```

```python
import functools

import jax
import jax.numpy as jnp
from jax import lax
from jax.experimental import pallas as pl
from jax.experimental.pallas import tpu as pltpu

F32 = jnp.float32
BF16 = jnp.bfloat16

D_MODEL = 2048
W_A = 1024
H_A = 8
HD_A = W_A // H_A
K_A = 4
LRU_C = 8.0
W_B = 512
K_B = 31
W_C = 512
H_C = 4
HD_C = W_C // H_C
CHUNK = 128
D_IN = 2 * (W_A + W_B + W_C)
D_FF = 4 * D_MODEL
EPS = 1e-6

OFF_XA, OFF_GA, OFF_XB, OFF_GB, OFF_ZC = 0, W_A, 2 * W_A, 2 * W_A + W_B, 2 * W_A + 2 * W_B
OFF_YA, OFF_YB, OFF_YC = 0, W_A, W_A + W_B

SUBLANES = 8
LANES = 128
A_TAIL = SUBLANES
B_TAIL = 32
V7X_VMEM_BYTES = 64 * 1024 * 1024


def _vmem_limit(block_bytes, scratch_bytes, temp_bytes):
    need = 2 * sum(block_bytes) + scratch_bytes + temp_bytes
    return int(min(need + need // 8, V7X_VMEM_BYTES - 4 * 1024 * 1024))


def _rms(x, g):
    return (x * lax.rsqrt(jnp.mean(x * x, axis=-1, keepdims=True) + EPS)) * g


def _layer_norm(x, g, b):
    xc = x - jnp.mean(x, axis=-1, keepdims=True)
    y = xc * lax.rsqrt(jnp.mean(xc * xc, axis=-1, keepdims=True) + EPS)
    return y * g + b


def _softplus(x):
    return jnp.maximum(x, 0.0) + jnp.log1p(jnp.exp(-jnp.abs(x)))


def _lru_gates(conv_h, pre, b_r, b_i, sp):
    r = jax.nn.sigmoid(pre[:, :HD_A] + b_r)
    i = jax.nn.sigmoid(pre[:, HD_A:] + b_i)
    log_a = (-LRU_C * r) * sp
    a = jnp.exp(log_a)
    t = jnp.tanh(log_a)
    u = jnp.sqrt(-2.0 * t / (1.0 - t)) * (i * conv_h)
    return a, u


def _interleave(pieces, fillers):
    total = sum(cost for cost, _ in pieces)
    done, issued = 0, 0
    for cost, piece in pieces:
        while issued < len(fillers) and fillers[issued][1] * total <= done:
            fillers[issued][0]()
            issued += 1
        piece()
        done += cost
    for filler, _ in fillers[issued:]:
        filler()


FFN_COLS = 512


def _ffn_kernel(*refs, final_norm):
    refs = list(refs)
    hf_ref = refs.pop()
    os_ref = refs.pop()
    op_ref = refs.pop()
    gf_ref = refs.pop() if final_norm else None
    xp_ref, xs_ref, g_ref, w1_ref, w2_ref = refs
    j = pl.program_id(1)
    tm = xp_ref.shape[0]
    outs = [(op_ref, slice(0, tm)), (os_ref, slice(tm, None))]

    @pl.when(j == 0)
    def _():
        for x_ref, (o_ref, rs) in zip((xp_ref, xs_ref), outs):
            x = x_ref[...]
            hf_ref[rs, :] = _rms(x, g_ref[...]).astype(BF16)
            o_ref[...] = x

    h = jnp.dot(hf_ref[...], w1_ref[...].astype(BF16), preferred_element_type=F32)
    h = jnp.square(jnp.maximum(h, 0.0)).astype(BF16)
    for c0 in range(0, D_MODEL, FFN_COLS):
        cs = slice(c0, c0 + FFN_COLS)
        acc = jnp.dot(h, w2_ref[:, cs].astype(BF16), preferred_element_type=F32)
        for o_ref, rs in outs:
            o_ref[:, cs] += acc[rs, :]

    if final_norm:
        @pl.when(j == pl.num_programs(1) - 1)
        def _():
            for o_ref, _ in outs:
                o_ref[...] = _rms(o_ref[...], gf_ref[...])


def _ffn(xp, xs, g, w1, w2, layer, tm, tf, g_final=None):
    m, ns = xp.shape[0], xs.shape[0]
    n_tiles = m // tm
    ts = ns // n_tiles
    assert m % tm == 0 and ns % n_tiles == 0 and ts % (2 * SUBLANES) == 0
    final_norm = g_final is not None
    prompt_tile = pl.BlockSpec((tm, D_MODEL), lambda i, j: (i, 0))
    sample_tile = pl.BlockSpec((ts, D_MODEL), lambda i, j: (i, 0))
    in_specs = [
        prompt_tile, sample_tile,
        pl.BlockSpec((None, 1, D_MODEL), lambda i, j: (layer, 0, 0)),
        pl.BlockSpec((None, D_MODEL, tf), lambda i, j: (layer, 0, j)),
        pl.BlockSpec((None, tf, D_MODEL), lambda i, j: (layer, j, 0)),
    ]
    args = [xp, xs, g, w1, w2]
    if final_norm:
        in_specs.append(pl.BlockSpec((1, D_MODEL), lambda i, j: (0, 0)))
        args.append(g_final)
    rows = tm + ts
    return pl.pallas_call(
        functools.partial(_ffn_kernel, final_norm=final_norm),
        grid=(n_tiles, D_FF // tf),
        in_specs=in_specs,
        out_specs=[prompt_tile, sample_tile],
        out_shape=[jax.ShapeDtypeStruct((m, D_MODEL), F32), jax.ShapeDtypeStruct((ns, D_MODEL), F32)],
        input_output_aliases={0: 0},
        scratch_shapes=[pltpu.VMEM((rows, D_MODEL), BF16)],
        compiler_params=pltpu.CompilerParams(
            dimension_semantics=("arbitrary", "arbitrary"),
            vmem_limit_bytes=_vmem_limit(
                [D_MODEL * 4, D_MODEL * tf * 4, tf * D_MODEL * 4, 2 * rows * D_MODEL * 4],
                rows * D_MODEL * 2,
                rows * tf * 6 + rows * FFN_COLS * 4 + 2 * D_MODEL * tf * 2),
        ),
        name="ffn",
    )(*args)


N_FRONT_PARAMS = 11
N_FRONT_SCRATCH = 10
PROJ_COLS = 512


def _pack_gate_weights(gwr_ref, gwi_ref, gw_ref):
    for h in range(H_A):
        gw_ref[h, :, 0:HD_A] = gwr_ref[h].astype(BF16)
        gw_ref[h, :, HD_A:] = gwi_ref[h].astype(BF16)


def _front_pieces(z_ref, outs, r_out, prm, scr, tl):
    caw_ref, cab_ref, _, _, br_ref, bi_ref, _, sg_ref, sb_ref, _, sbt_ref = prm
    xa_buf, a_s, u_s, _, g_s, _, vn_s, wm_s, sp_s, gw_ref = scr
    _, yc_ref, ub_ref = outs
    pieces = []

    def c_norm(r):
        v = jax.nn.gelu(z_ref[r:r + CHUNK, OFF_ZC + W_C:OFF_ZC + 2 * W_C])
        vn_s[r:r + CHUNK, :] = _layer_norm(v, sg_ref[...], sb_ref[...]).astype(BF16)

    def c_head(r, h):
        cs = slice(h * HD_C, (h + 1) * HD_C)
        mixed = jnp.dot(wm_s[h], vn_s[r:r + CHUNK, cs], preferred_element_type=F32)
        mixed = mixed + sbt_ref[:, h:h + 1]
        u_c = jax.nn.gelu(z_ref[r:r + CHUNK, OFF_ZC + h * HD_C:OFF_ZC + (h + 1) * HD_C])
        yc_ref[r_out + r:r_out + r + CHUNK, cs] = (u_c * mixed).astype(BF16)

    def glu(r):
        ub_ref[r_out + r:r_out + r + CHUNK, :] = (
            z_ref[r:r + CHUNK, OFF_XB:OFF_XB + W_B]
            * jax.nn.sigmoid(z_ref[r:r + CHUNK, OFF_GB:OFF_GB + W_B]))

    def a_head(h):
        cs = slice(h * HD_A, (h + 1) * HD_A)
        xa_buf[A_TAIL:A_TAIL + tl, cs] = z_ref[:, OFF_XA + h * HD_A:OFF_XA + (h + 1) * HD_A]
        xa_full = xa_buf[0:A_TAIL + tl, cs]
        conv_h = cab_ref[:, cs] + caw_ref[K_A - 1:K_A, cs] * xa_full[A_TAIL:, :]
        for k in range(K_A - 1):
            shifted = pltpu.roll(xa_full, K_A - 1 - k, axis=0)[A_TAIL:, :]
            conv_h = conv_h + caw_ref[k:k + 1, cs] * shifted
        pre = jnp.dot(conv_h.astype(BF16), gw_ref[h], preferred_element_type=F32)
        a, u = _lru_gates(conv_h, pre, br_ref[:, cs], bi_ref[:, cs], sp_s[:, cs])
        a_s[:, cs] = a
        u_s[:, cs] = u
        g_s[:, cs] = jax.nn.gelu(z_ref[:, OFF_GA + h * HD_A:OFF_GA + (h + 1) * HD_A])

    other = []
    for r in range(0, tl, CHUNK):
        other.append((1500, functools.partial(c_norm, r)))
        for h in range(H_C):
            other.append((500, functools.partial(c_head, r, h)))
        other.append((600, functools.partial(glu, r)))
    per_gate = -(-len(other) // H_A)
    for h in range(H_A):
        pieces.append((2000 * tl // 256, functools.partial(a_head, h)))
        pieces.extend(other[h * per_gate:(h + 1) * per_gate])
    return pieces


def _front_kernel(x_ref, x_next_ref, xs_ref, g_ref, w_ref, *rest, tl, chunks_per_seq):
    prm = rest[:N_FRONT_PARAMS]
    gwr_ref, gwi_ref, lam_ref, sw_ref = prm[2], prm[3], prm[6], prm[9]
    ya_ref, yc_ref, ub_ref, ca_out_ref, h_out_ref, zs_ref = rest[N_FRONT_PARAMS:N_FRONT_PARAMS + 6]
    z_even, z_odd, hn_s = rest[N_FRONT_PARAMS + 6:N_FRONT_PARAMS + 9]
    scr = rest[N_FRONT_PARAMS + 9:]
    xa_buf, a_s, u_s, h_s, g_s, carry, _, wm_s, sp_s, gw_s = scr
    outs = (ya_ref, yc_ref, ub_ref)
    s = pl.program_id(0)

    def normalise(x_rows_ref, r0):
        hn_s[...] = _rms(x_rows_ref[r0:r0 + tl, :], g_ref[...]).astype(BF16)

    def project_and_mix(x_rows_ref, x_r0, z_next, z_cur, r_out):
        normalise(x_rows_ref, x_r0)

        def project(c0):
            z_next[:, c0:c0 + PROJ_COLS] = jnp.dot(
                hn_s[...], w_ref[:, c0:c0 + PROJ_COLS], preferred_element_type=F32)

        _interleave(_front_pieces(z_cur, outs, r_out, prm, scr, tl),
                    [(functools.partial(project, c0), c0 / D_IN) for c0 in range(0, D_IN, PROJ_COLS)])
        xa_buf[0:A_TAIL, :] = xa_buf[tl:tl + A_TAIL, :]

        def scan_row(t, h):
            h = a_s[pl.ds(t, 1), :] * h + u_s[pl.ds(t, 1), :]
            h_s[pl.ds(t, 1), :] = h
            return h

        carry[...] = lax.fori_loop(0, tl, scan_row, carry[...], unroll=8)
        ya_ref[r_out:r_out + tl, :] = (h_s[...] * g_s[...]).astype(BF16)

    @pl.when(s == 0)
    def _():
        causal = (lax.broadcasted_iota(jnp.int32, (CHUNK, CHUNK), 0)
                  >= lax.broadcasted_iota(jnp.int32, (CHUNK, CHUNK), 1))
        for h in range(H_C):
            wm_s[h] = jnp.where(causal, sw_ref[h], 0.0).astype(BF16)
        sp_s[...] = _softplus(-lam_ref[...])
        _pack_gate_weights(gwr_ref, gwi_ref, gw_s)
        zs_ref[...] = jnp.dot(_rms(xs_ref[...], g_ref[...]).astype(BF16), w_ref[...],
                              preferred_element_type=F32)
        normalise(x_ref, 0)
        z_even[...] = jnp.dot(hn_s[...], w_ref[...], preferred_element_type=F32)

    @pl.when(s % (chunks_per_seq // 2) == 0)
    def _():
        xa_buf[0:A_TAIL, :] = jnp.zeros((A_TAIL, W_A), F32)
        carry[...] = jnp.zeros((1, W_A), F32)

    project_and_mix(x_ref, tl, z_odd, z_even, 0)
    project_and_mix(x_next_ref, 0, z_even, z_odd, tl)
    ca_out_ref[...] = xa_buf[0:A_TAIL, :]
    h_out_ref[...] = carry[...]


def _front(x, xs, g, w, p, layer, nb, seq, tl):
    ns = xs.shape[0]
    chunks_per_seq = seq // tl
    assert seq % tl == 0 and chunks_per_seq % 2 == 0 and tl % CHUNK == 0
    pairs = nb * chunks_per_seq // 2
    pairs_per_seq = chunks_per_seq // 2
    rows = nb * seq

    def per_layer(shape):
        return pl.BlockSpec((None,) + shape, lambda s: (layer,) + (0,) * len(shape))

    def pair(width):
        return pl.BlockSpec((2 * tl, width), lambda s: (s, 0))

    scratch = [
        pltpu.VMEM((tl, D_IN), F32), pltpu.VMEM((tl, D_IN), F32), pltpu.VMEM((tl, D_MODEL), BF16),
        pltpu.VMEM((A_TAIL + tl, W_A), F32),
        pltpu.VMEM((tl, W_A), F32), pltpu.VMEM((tl, W_A), F32), pltpu.VMEM((tl, W_A), F32),
        pltpu.VMEM((tl, W_A), F32), pltpu.VMEM((1, W_A), F32),
        pltpu.VMEM((tl, W_C), BF16), pltpu.VMEM((H_C, CHUNK, CHUNK), BF16), pltpu.VMEM((1, W_A), F32),
        pltpu.VMEM((H_A, HD_A, 2 * HD_A), BF16),
    ]
    assert len(scratch) == 3 + N_FRONT_SCRATCH
    scratch_bytes = (2 * tl * D_IN * 4 + tl * D_MODEL * 2
                     + (A_TAIL + 5 * tl) * W_A * 4 + tl * W_C * 2 + H_C * CHUNK * CHUNK * 2)
    return pl.pallas_call(
        functools.partial(_front_kernel, tl=tl, chunks_per_seq=chunks_per_seq),
        grid=(pairs,),
        in_specs=[
            pair(D_MODEL),
            pl.BlockSpec((tl, D_MODEL), lambda s: (jnp.minimum(2 * s + 2, 2 * pairs - 1), 0)),
            pl.BlockSpec((ns, D_MODEL), lambda s: (0, 0)),
            pl.BlockSpec((None, 1, D_MODEL), lambda s: (layer, 0, 0)),
            pl.BlockSpec((None, D_MODEL, D_IN), lambda s: (layer, 0, 0), pipeline_mode=pl.Buffered(1)),
            per_layer((K_A, W_A)), per_layer((1, W_A)),
            per_layer((H_A, HD_A, HD_A)), per_layer((H_A, HD_A, HD_A)),
            per_layer((1, W_A)), per_layer((1, W_A)), per_layer((1, W_A)),
            per_layer((1, W_C)), per_layer((1, W_C)), per_layer((H_C, CHUNK, CHUNK)),
            per_layer((CHUNK, H_C)),
        ],
        out_specs=[
            pair(W_A), pair(W_C), pair(W_B),
            pl.BlockSpec((None, A_TAIL, W_A), lambda s: (s // pairs_per_seq, 0, 0)),
            pl.BlockSpec((None, 1, W_A), lambda s: (s // pairs_per_seq, 0, 0)),
            pl.BlockSpec((ns, D_IN), lambda s: (0, 0)),
        ],
        out_shape=[
            jax.ShapeDtypeStruct((rows, W_A), BF16),
            jax.ShapeDtypeStruct((rows, W_C), BF16),
            jax.ShapeDtypeStruct((rows, W_B), F32),
            jax.ShapeDtypeStruct((nb, A_TAIL, W_A), F32),
            jax.ShapeDtypeStruct((nb, 1, W_A), F32),
            jax.ShapeDtypeStruct((ns, D_IN), F32),
        ],
        scratch_shapes=scratch,
        compiler_params=pltpu.CompilerParams(
            dimension_semantics=("arbitrary",),
            vmem_limit_bytes=_vmem_limit(
                [3 * tl * D_MODEL * 4, 2 * tl * (W_A + W_C) * 2, 2 * tl * W_B * 4, 1024 * 1024,
                 ns * D_MODEL * 4, ns * D_IN * 4],
                D_MODEL * D_IN * 2 + scratch_bytes, tl * D_IN * 4 + 4 * tl * W_A * 4),
        ),
        name="front",
    )(x, x, xs, g, w, p["conv_a_w"], p["conv_a_b"], p["gate_r_w"], p["gate_i_w"], p["gate_r_b"],
      p["gate_i_b"], p["lru_lambda"], p["sgu_ln_g"], p["sgu_ln_b"], p["sgu_w"], p["sgu_b_t"])


CONV_B_ROWS = 64
NORM_B_ROWS = 32
BACK_ROWS = 256
BACK_COLS = 2048


def _conv_b_tile(w_ref, ub_slab, conv_slab, r0, c, rows):
    first = B_TAIL - (K_B - 1)
    half = rows // 2
    cs = slice(c * LANES, (c + 1) * LANES)
    even = odd = None
    for kp in range(first, first + K_B + 1):
        x = ub_slab[c, pl.ds(r0 + kp, half, stride=2), :]
        if kp < first + K_B:
            term = w_ref[kp - first:kp - first + 1, cs] * x
            even = term if even is None else even + term
        if kp > first:
            term = w_ref[kp - first - 1:kp - first, cs] * x
            odd = term if odd is None else odd + term
    conv_slab[c, pl.ds(r0, half, stride=2), :] = even
    conv_slab[c, pl.ds(r0 + 1, half, stride=2), :] = odd


def _back_kernel(x_ref, ya_ref, yc_ref, ub_ref, ub_prev_ref, xs_ref, mixs_ref, w32_ref, cbw_ref, lbg_ref,
                 lbb_ref, o_ref, os_ref, ub_slab, conv_slab, yb_s, w_ref, *, tm, tiles_per_seq):
    i = pl.program_id(0)

    @pl.when(i == 0)
    def _():
        for c0 in range(0, D_MODEL, PROJ_COLS):
            w_ref[:, c0:c0 + PROJ_COLS] = w32_ref[:, c0:c0 + PROJ_COLS].astype(BF16)
        os_ref[...] = xs_ref[...] + jnp.dot(mixs_ref[...], w_ref[...], preferred_element_type=F32)

    n_slabs = W_B // LANES
    prev = jnp.where(i % tiles_per_seq == 0, 0.0, ub_prev_ref[...])
    for c in range(n_slabs):
        ub_slab[c, 0:B_TAIL, :] = prev[:, c * LANES:(c + 1) * LANES]
        ub_slab[c, B_TAIL:, :] = ub_ref[:, c * LANES:(c + 1) * LANES]

    def conv_rows(r):
        n = CONV_B_ROWS
        for c in range(n_slabs):
            _conv_b_tile(cbw_ref, ub_slab, conv_slab, r, c, n)
        for r1 in range(r, r + n, NORM_B_ROWS):
            conv = jnp.concatenate([conv_slab[c, r1:r1 + NORM_B_ROWS, :] for c in range(n_slabs)], axis=-1)
            y_b = _layer_norm(conv, lbg_ref[...], lbb_ref[...])
            yb_s[r1:r1 + NORM_B_ROWS, :] = jax.nn.silu(y_b).astype(BF16)

    def project(r, c0):
        rs, cs = slice(r, r + BACK_ROWS), slice(c0, c0 + BACK_COLS)
        o_ref[rs, cs] = (
            x_ref[rs, cs]
            + jnp.dot(ya_ref[rs, :], w_ref[OFF_YA:OFF_YA + W_A, cs], preferred_element_type=F32)
            + jnp.dot(yb_s[rs, :], w_ref[OFF_YB:OFF_YB + W_B, cs], preferred_element_type=F32)
            + jnp.dot(yc_ref[rs, :], w_ref[OFF_YC:OFF_YC + W_C, cs], preferred_element_type=F32))

    fillers = [(functools.partial(project, r, c0), (r + BACK_ROWS) / tm)
               for r in range(0, tm, BACK_ROWS) for c0 in range(0, D_MODEL, BACK_COLS)]
    _interleave([(CONV_B_ROWS, functools.partial(conv_rows, r)) for r in range(0, tm, CONV_B_ROWS)],
                fillers)


def _back(x, ya, yc, ub, xs, mix_s, w, p, layer, seq, tm):
    rows, ns = x.shape[0], xs.shape[0]
    assert seq % tm == 0 and tm % B_TAIL == 0 and tm % CONV_B_ROWS == 0 and tm % BACK_ROWS == 0
    tails_per_tile = tm // B_TAIL

    def per_layer(shape):
        return pl.BlockSpec((None,) + shape, lambda i: (layer,) + (0,) * len(shape))

    def tile(width):
        return pl.BlockSpec((tm, width), lambda i: (i, 0))

    def whole(width):
        return pl.BlockSpec((ns, width), lambda i: (0, 0))

    return pl.pallas_call(
        functools.partial(_back_kernel, tm=tm, tiles_per_seq=seq // tm),
        grid=(rows // tm,),
        in_specs=[
            tile(D_MODEL), tile(W_A), tile(W_C), tile(W_B),
            pl.BlockSpec((B_TAIL, W_B), lambda i: (jnp.maximum(i * tails_per_tile - 1, 0), 0)),
            whole(D_MODEL), whole(D_MODEL),
            pl.BlockSpec((None, D_MODEL, D_MODEL), lambda i: (layer, 0, 0), pipeline_mode=pl.Buffered(1)),
            per_layer((K_B, W_B)), per_layer((1, W_B)), per_layer((1, W_B)),
        ],
        out_specs=[tile(D_MODEL), whole(D_MODEL)],
        out_shape=[jax.ShapeDtypeStruct((rows, D_MODEL), F32), jax.ShapeDtypeStruct((ns, D_MODEL), F32)],
        scratch_shapes=[pltpu.VMEM((W_B // LANES, B_TAIL + tm, LANES), F32),
                        pltpu.VMEM((W_B // LANES, tm, LANES), F32),
                        pltpu.VMEM((tm, W_B), BF16), pltpu.VMEM((D_MODEL, D_MODEL), BF16)],
        compiler_params=pltpu.CompilerParams(
            dimension_semantics=("arbitrary",),
            vmem_limit_bytes=_vmem_limit(
                [tm * D_MODEL * 4, tm * (W_A + W_C) * 2, tm * W_B * 4, tm * D_MODEL * 4, 256 * 1024,
                 ns * D_MODEL * 10],
                D_MODEL * D_MODEL * (4 + 2) + (B_TAIL + 2 * tm) * W_B * 4 + tm * W_B * 2,
                2 * BACK_ROWS * BACK_COLS * 4),
        ),
        name="back",
    )(x, ya, yc, ub, ub, xs, mix_s, w, p["conv_b_w"], p["ln_b_g"], p["ln_b_b"])


def _mixer_sample_kernel(
        z_ref, ca_ref, h0_ref, cb_ref, caw_ref, cab_ref, gwr_ref, gwi_ref, br_ref, bi_ref, lam_ref,
        cbw_ref, lbg_ref, lbb_ref, sg_ref, sb_ref, sw0_ref, sb0_ref,
        mix_ref, ca_out_ref, h_out_ref, cb_out_ref, vn_out_ref):
    xa = z_ref[:, OFF_XA:OFF_XA + W_A]
    for k in range(K_A - 2):
        ca_out_ref[k] = ca_ref[k + 1]
    ca_out_ref[K_A - 2] = xa
    sp = _softplus(-lam_ref[...])
    for h in range(H_A):
        cs = slice(h * HD_A, (h + 1) * HD_A)
        conv_h = cab_ref[:, cs] + caw_ref[K_A - 1:K_A, cs] * xa[:, cs]
        for k in range(K_A - 1):
            conv_h = conv_h + caw_ref[k:k + 1, cs] * ca_ref[k, :, cs]
        conv_b = conv_h.astype(BF16)
        pre = jnp.concatenate(
            [jnp.dot(conv_b, gwr_ref[h].astype(BF16), preferred_element_type=F32),
             jnp.dot(conv_b, gwi_ref[h].astype(BF16), preferred_element_type=F32)], axis=-1)
        a, u = _lru_gates(conv_h, pre, br_ref[:, cs], bi_ref[:, cs], sp[:, cs])
        h_new = a * h0_ref[:, cs] + u
        h_out_ref[:, cs] = h_new
        mix_ref[:, OFF_YA + h * HD_A:OFF_YA + (h + 1) * HD_A] = (
            h_new * jax.nn.gelu(z_ref[:, OFF_GA + h * HD_A:OFF_GA + (h + 1) * HD_A])).astype(BF16)

    ub = z_ref[:, OFF_XB:OFF_XB + W_B] * jax.nn.sigmoid(z_ref[:, OFF_GB:OFF_GB + W_B])
    acc = cbw_ref[K_B - 1:K_B, :] * ub
    for k in range(K_B - 1):
        acc = acc + cbw_ref[k:k + 1, :] * cb_ref[k]
    for k in range(K_B - 2):
        cb_out_ref[k] = cb_ref[k + 1]
    cb_out_ref[K_B - 2] = ub
    y_b = _layer_norm(acc, lbg_ref[...], lbb_ref[...])
    mix_ref[:, OFF_YB:OFF_YB + W_B] = jax.nn.silu(y_b).astype(BF16)

    g_c = jax.nn.gelu(z_ref[:, OFF_ZC:OFF_ZC + 2 * W_C])
    v_n = _layer_norm(g_c[:, W_C:], sg_ref[...], sb_ref[...])
    vn_out_ref[...] = v_n
    mix_ref[:, OFF_YC:OFF_YC + W_C] = (g_c[:, :W_C] * (sw0_ref[...] * v_n + sb0_ref[...])).astype(BF16)


def _mixer_sample(z, ca_t, h0, cb_t, p, layer, tb):
    nb = z.shape[0]

    def per_layer(shape):
        return pl.BlockSpec((None,) + shape, lambda b: (layer,) + (0,) * len(shape))

    return pl.pallas_call(
        _mixer_sample_kernel,
        grid=(nb // tb,),
        in_specs=[
            pl.BlockSpec((tb, D_IN), lambda b: (b, 0)),
            pl.BlockSpec((None, K_A - 1, tb, W_A), lambda b: (layer, 0, b, 0)),
            pl.BlockSpec((None, tb, W_A), lambda b: (layer, b, 0)),
            pl.BlockSpec((None, K_B - 1, tb, W_B), lambda b: (layer, 0, b, 0)),
            per_layer((K_A, W_A)), per_layer((1, W_A)),
            per_layer((H_A, HD_A, HD_A)), per_layer((H_A, HD_A, HD_A)),
            per_layer((1, W_A)), per_layer((1, W_A)), per_layer((1, W_A)),
            per_layer((K_B, W_B)), per_layer((1, W_B)), per_layer((1, W_B)),
            per_layer((1, W_C)), per_layer((1, W_C)), per_layer((1, W_C)), per_layer((1, W_C)),
        ],
        out_specs=[
            pl.BlockSpec((tb, D_MODEL), lambda b: (b, 0)),
            pl.BlockSpec((K_A - 1, tb, W_A), lambda b: (0, b, 0)),
            pl.BlockSpec((tb, W_A), lambda b: (b, 0)),
            pl.BlockSpec((K_B - 1, tb, W_B), lambda b: (0, b, 0)),
            pl.BlockSpec((tb, W_C), lambda b: (b, 0)),
        ],
        out_shape=[
            jax.ShapeDtypeStruct((nb, D_MODEL), BF16),
            jax.ShapeDtypeStruct((K_A - 1, nb, W_A), F32),
            jax.ShapeDtypeStruct((nb, W_A), F32),
            jax.ShapeDtypeStruct((K_B - 1, nb, W_B), F32),
            jax.ShapeDtypeStruct((nb, W_C), F32),
        ],
        compiler_params=pltpu.CompilerParams(
            dimension_semantics=("arbitrary",),
            vmem_limit_bytes=_vmem_limit(
                [tb * D_IN * 4, 2 * (K_A - 1) * tb * W_A * 4, 2 * tb * W_A * 4,
                 2 * (K_B - 1) * tb * W_B * 4, tb * D_MODEL * 2, tb * W_C * 4, 1024 * 1024],
                0, 8 * tb * W_A * 4),
        ),
        name="mixer_sample",
    )(z, ca_t, h0, cb_t, p["conv_a_w"], p["conv_a_b"], p["gate_r_w"], p["gate_i_w"], p["gate_r_b"],
      p["gate_i_b"], p["lru_lambda"], p["conv_b_w"], p["ln_b_g"], p["ln_b_b"], p["sgu_ln_g"], p["sgu_ln_b"],
      p["sgu_w00"], p["sgu_b0"])


def kernel(x_prompt, x_sample, state_conv_a, state_lru_h, state_conv_b, norm_mix, w_in, conv_a_w,
           conv_a_b, gate_r_w, gate_r_b, gate_i_w, gate_i_b, lru_lambda, conv_b_w, ln_b_g, ln_b_b,
           sgu_ln_g, sgu_ln_b, sgu_w, sgu_b, w_out, norm_ffn, w_ff1, w_ff2, norm_final):
    depth = w_in.shape[0]
    nb, seq, _ = x_prompt.shape
    ns = x_sample.shape[0]

    def row(v):
        return v[:, None, :]

    p = {
        "conv_a_w": conv_a_w, "conv_a_b": row(conv_a_b),
        "gate_r_w": gate_r_w, "gate_i_w": gate_i_w,
        "gate_r_b": row(gate_r_b), "gate_i_b": row(gate_i_b), "lru_lambda": row(lru_lambda),
        "conv_b_w": conv_b_w, "ln_b_g": row(ln_b_g), "ln_b_b": row(ln_b_b),
        "sgu_ln_g": row(sgu_ln_g), "sgu_ln_b": row(sgu_ln_b),
        "sgu_w": sgu_w, "sgu_b_t": jnp.swapaxes(sgu_b, 1, 2),
        "sgu_w00": row(jnp.repeat(sgu_w[:, :, 0, 0], HD_C, axis=-1)),
        "sgu_b0": row(jnp.repeat(sgu_b[:, :, 0], HD_C, axis=-1)),
    }
    g_mix, g_ffn, g_final = row(norm_mix), row(norm_ffn), norm_final[None, :]
    w_in_b = w_in.astype(BF16)

    ca_t = jnp.swapaxes(state_conv_a, 1, 2)
    cb_t = jnp.swapaxes(state_conv_b, 1, 2)

    xp = x_prompt.reshape(nb * seq, D_MODEL)
    xs = x_sample.reshape(ns, D_MODEL)
    ca_p, h_p, cb_p, ca_s, h_s, cb_s, v_s = [], [], [], [], [], [], []
    for l in range(depth):
        last = g_final if l == depth - 1 else None
        ya, yc, ub, ca_l, h_l, zs = _front(xp, xs, g_mix, w_in_b, p, l, nb, seq, tl=256)
        ca_p.append(ca_l[:, A_TAIL - (K_A - 1):, :])
        h_p.append(h_l[:, 0, :])
        cb_p.append(ub.reshape(nb, seq, W_B)[:, seq - (K_B - 1):, :])
        mix_s, cas_l, hs_l, cbs_l, v_l = _mixer_sample(zs, ca_t, state_lru_h, cb_t, p, l, tb=32)
        ca_s.append(jnp.swapaxes(cas_l, 0, 1))
        h_s.append(hs_l)
        cb_s.append(jnp.swapaxes(cbs_l, 0, 1))
        v_s.append(v_l[:, None, :])
        xp, xs = _back(xp, ya, yc, ub, xs, mix_s, w_out, p, l, seq, tm=512)
        xp, xs = _ffn(xp, xs, g_ffn, w_ff1, w_ff2, l, tm=1024, tf=512, g_final=last)

    return (xp.reshape(nb, seq, D_MODEL), xs.reshape(ns, 1, D_MODEL),
            jnp.stack(ca_p), jnp.stack(h_p), jnp.stack(cb_p),
            jnp.stack(ca_s), jnp.stack(h_s), jnp.stack(cb_s), jnp.stack(v_s))
```

```python
import functools

import jax
import jax.numpy as jnp
from jax import lax
from jax.experimental import pallas as pl
from jax.experimental.pallas import tpu as pltpu

F32 = jnp.float32
BF16 = jnp.bfloat16

D_MODEL = 2048
W_A = 1024
H_A = 8
HD_A = W_A // H_A
K_A = 4
LRU_C = 8.0
W_B = 512
K_B = 31
W_C = 512
H_C = 4
HD_C = W_C // H_C
CHUNK = 128
D_IN = 2 * (W_A + W_B + W_C)
D_FF = 4 * D_MODEL
EPS = 1e-6

OFF_XA, OFF_GA, OFF_XB, OFF_GB, OFF_ZC = 0, W_A, 2 * W_A, 2 * W_A + W_B, 2 * W_A + 2 * W_B
OFF_YA, OFF_YB, OFF_YC = 0, W_A, W_A + W_B

SUBLANES = 8
LANES = 128
A_TAIL = SUBLANES
B_TAIL = 32
V7X_VMEM_BYTES = 64 * 1024 * 1024


def _vmem_limit(block_bytes, scratch_bytes, temp_bytes):
    need = 2 * sum(block_bytes) + scratch_bytes + temp_bytes
    return int(min(need + need // 8, V7X_VMEM_BYTES - 4 * 1024 * 1024))


def _rms(x, g):
    return (x * lax.rsqrt(jnp.mean(x * x, axis=-1, keepdims=True) + EPS)) * g


def _layer_norm(x, g, b):
    xc = x - jnp.mean(x, axis=-1, keepdims=True)
    y = xc * lax.rsqrt(jnp.mean(xc * xc, axis=-1, keepdims=True) + EPS)
    return y * g + b


def _softplus(x):
    return jnp.maximum(x, 0.0) + jnp.log1p(jnp.exp(-jnp.abs(x)))


def _lru_gates(conv_h, pre, b_r, b_i, sp):
    r = jax.nn.sigmoid(pre[:, :HD_A] + b_r)
    i = jax.nn.sigmoid(pre[:, HD_A:] + b_i)
    log_a = (-LRU_C * r) * sp
    a = jnp.exp(log_a)
    t = jnp.tanh(log_a)
    u = jnp.sqrt(-2.0 * t / (1.0 - t)) * (i * conv_h)
    return a, u


def _interleave(pieces, fillers):
    total = sum(cost for cost, _ in pieces)
    done, issued = 0, 0
    for cost, piece in pieces:
        while issued < len(fillers) and fillers[issued][1] * total <= done:
            fillers[issued][0]()
            issued += 1
        piece()
        done += cost
    for filler, _ in fillers[issued:]:
        filler()


FFN_COLS = 512


def _ffn_kernel(*refs, final_norm):
    refs = list(refs)
    hf_ref = refs.pop()
    os_ref = refs.pop()
    op_ref = refs.pop()
    gf_ref = refs.pop() if final_norm else None
    xp_ref, xs_ref, g_ref, w1_ref, w2_ref = refs
    j = pl.program_id(1)
    tm = xp_ref.shape[0]
    outs = [(op_ref, slice(0, tm)), (os_ref, slice(tm, None))]

    @pl.when(j == 0)
    def _():
        for x_ref, (o_ref, rs) in zip((xp_ref, xs_ref), outs):
            x = x_ref[...]
            hf_ref[rs, :] = _rms(x, g_ref[...]).astype(BF16)
            o_ref[...] = x

    h = jnp.dot(hf_ref[...], w1_ref[...].astype(BF16), preferred_element_type=F32)
    h = jnp.square(jnp.maximum(h, 0.0)).astype(BF16)
    for c0 in range(0, D_MODEL, FFN_COLS):
        cs = slice(c0, c0 + FFN_COLS)
        acc = jnp.dot(h, w2_ref[:, cs].astype(BF16), preferred_element_type=F32)
        for o_ref, rs in outs:
            o_ref[:, cs] += acc[rs, :]

    if final_norm:
        @pl.when(j == pl.num_programs(1) - 1)
        def _():
            for o_ref, _ in outs:
                o_ref[...] = _rms(o_ref[...], gf_ref[...])


def _ffn(xp, xs, g, w1, w2, layer, tm, tf, g_final=None):
    m, ns = xp.shape[0], xs.shape[0]
    n_tiles = m // tm
    ts = ns // n_tiles
    assert m % tm == 0 and ns % n_tiles == 0 and ts % (2 * SUBLANES) == 0
    final_norm = g_final is not None
    prompt_tile = pl.BlockSpec((tm, D_MODEL), lambda i, j: (i, 0))
    sample_tile = pl.BlockSpec((ts, D_MODEL), lambda i, j: (i, 0))
    in_specs = [
        prompt_tile, sample_tile,
        pl.BlockSpec((None, 1, D_MODEL), lambda i, j: (layer, 0, 0)),
        pl.BlockSpec((None, D_MODEL, tf), lambda i, j: (layer, 0, j)),
        pl.BlockSpec((None, tf, D_MODEL), lambda i, j: (layer, j, 0)),
    ]
    args = [xp, xs, g, w1, w2]
    if final_norm:
        in_specs.append(pl.BlockSpec((1, D_MODEL), lambda i, j: (0, 0)))
        args.append(g_final)
    rows = tm + ts
    return pl.pallas_call(
        functools.partial(_ffn_kernel, final_norm=final_norm),
        grid=(n_tiles, D_FF // tf),
        in_specs=in_specs,
        out_specs=[prompt_tile, sample_tile],
        out_shape=[jax.ShapeDtypeStruct((m, D_MODEL), F32), jax.ShapeDtypeStruct((ns, D_MODEL), F32)],
        input_output_aliases={0: 0},
        scratch_shapes=[pltpu.VMEM((rows, D_MODEL), BF16)],
        compiler_params=pltpu.CompilerParams(
            dimension_semantics=("arbitrary", "arbitrary"),
            vmem_limit_bytes=_vmem_limit(
                [D_MODEL * 4, D_MODEL * tf * 4, tf * D_MODEL * 4, 2 * rows * D_MODEL * 4],
                rows * D_MODEL * 2,
                rows * tf * 6 + rows * FFN_COLS * 4 + 2 * D_MODEL * tf * 2),
        ),
        name="ffn",
    )(*args)


N_FRONT_PARAMS = 11
N_FRONT_SCRATCH = 10
PROJ_COLS = 512


def _pack_gate_weights(gwr_ref, gwi_ref, gw_ref):
    for h in range(H_A):
        gw_ref[h, :, 0:HD_A] = gwr_ref[h].astype(BF16)
        gw_ref[h, :, HD_A:] = gwi_ref[h].astype(BF16)


def _front_pieces(z_ref, outs, r_out, prm, scr, tl):
    caw_ref, cab_ref, _, _, br_ref, bi_ref, _, sg_ref, sb_ref, _, sbt_ref = prm
    xa_buf, a_s, u_s, _, g_s, _, vn_s, wm_s, sp_s, gw_ref = scr
    _, yc_ref, ub_ref = outs
    pieces = []

    def c_norm(r):
        v = jax.nn.gelu(z_ref[r:r + CHUNK, OFF_ZC + W_C:OFF_ZC + 2 * W_C])
        vn_s[r:r + CHUNK, :] = _layer_norm(v, sg_ref[...], sb_ref[...]).astype(BF16)

    def c_head(r, h):
        cs = slice(h * HD_C, (h + 1) * HD_C)
        mixed = jnp.dot(wm_s[h], vn_s[r:r + CHUNK, cs], preferred_element_type=F32)
        mixed = mixed + sbt_ref[:, h:h + 1]
        u_c = jax.nn.gelu(z_ref[r:r + CHUNK, OFF_ZC + h * HD_C:OFF_ZC + (h + 1) * HD_C])
        yc_ref[r_out + r:r_out + r + CHUNK, cs] = (u_c * mixed).astype(BF16)

    def glu(r):
        ub_ref[r_out + r:r_out + r + CHUNK, :] = (
            z_ref[r:r + CHUNK, OFF_XB:OFF_XB + W_B]
            * jax.nn.sigmoid(z_ref[r:r + CHUNK, OFF_GB:OFF_GB + W_B]))

    def a_head(h):
        cs = slice(h * HD_A, (h + 1) * HD_A)
        xa_buf[A_TAIL:A_TAIL + tl, cs] = z_ref[:, OFF_XA + h * HD_A:OFF_XA + (h + 1) * HD_A]
        xa_full = xa_buf[0:A_TAIL + tl, cs]
        conv_h = cab_ref[:, cs] + caw_ref[K_A - 1:K_A, cs] * xa_full[A_TAIL:, :]
        for k in range(K_A - 1):
            shifted = pltpu.roll(xa_full, K_A - 1 - k, axis=0)[A_TAIL:, :]
            conv_h = conv_h + caw_ref[k:k + 1, cs] * shifted
        pre = jnp.dot(conv_h.astype(BF16), gw_ref[h], preferred_element_type=F32)
        a, u = _lru_gates(conv_h, pre, br_ref[:, cs], bi_ref[:, cs], sp_s[:, cs])
        a_s[:, cs] = a
        u_s[:, cs] = u
        g_s[:, cs] = jax.nn.gelu(z_ref[:, OFF_GA + h * HD_A:OFF_GA + (h + 1) * HD_A])

    other = []
    for r in range(0, tl, CHUNK):
        other.append((1500, functools.partial(c_norm, r)))
        for h in range(H_C):
            other.append((500, functools.partial(c_head, r, h)))
        other.append((600, functools.partial(glu, r)))
    per_gate = -(-len(other) // H_A)
    for h in range(H_A):
        pieces.append((2000 * tl // 256, functools.partial(a_head, h)))
        pieces.extend(other[h * per_gate:(h + 1) * per_gate])
    return pieces


def _front_kernel(x_ref, x_next_ref, xs_ref, g_ref, w_ref, *rest, tl, chunks_per_seq):
    prm = rest[:N_FRONT_PARAMS]
    gwr_ref, gwi_ref, lam_ref, sw_ref = prm[2], prm[3], prm[6], prm[9]
    ya_ref, yc_ref, ub_ref, ca_out_ref, h_out_ref, zs_ref = rest[N_FRONT_PARAMS:N_FRONT_PARAMS + 6]
    z_even, z_odd, hn_s = rest[N_FRONT_PARAMS + 6:N_FRONT_PARAMS + 9]
    scr = rest[N_FRONT_PARAMS + 9:]
    xa_buf, a_s, u_s, h_s, g_s, carry, _, wm_s, sp_s, gw_s = scr
    outs = (ya_ref, yc_ref, ub_ref)
    s = pl.program_id(0)

    def normalise(x_rows_ref, r0):
        hn_s[...] = _rms(x_rows_ref[r0:r0 + tl, :], g_ref[...]).astype(BF16)

    def project_and_mix(x_rows_ref, x_r0, z_next, z_cur, r_out):
        normalise(x_rows_ref, x_r0)

        def project(c0):
            z_next[:, c0:c0 + PROJ_COLS] = jnp.dot(
                hn_s[...], w_ref[:, c0:c0 + PROJ_COLS], preferred_element_type=F32)

        _interleave(_front_pieces(z_cur, outs, r_out, prm, scr, tl),
                    [(functools.partial(project, c0), c0 / D_IN) for c0 in range(0, D_IN, PROJ_COLS)])
        xa_buf[0:A_TAIL, :] = xa_buf[tl:tl + A_TAIL, :]

        def scan_row(t, h):
            h = a_s[pl.ds(t, 1), :] * h + u_s[pl.ds(t, 1), :]
            h_s[pl.ds(t, 1), :] = h
            return h

        carry[...] = lax.fori_loop(0, tl, scan_row, carry[...], unroll=True)
        ya_ref[r_out:r_out + tl, :] = (h_s[...] * g_s[...]).astype(BF16)

    @pl.when(s == 0)
    def _():
        causal = (lax.broadcasted_iota(jnp.int32, (CHUNK, CHUNK), 0)
                  >= lax.broadcasted_iota(jnp.int32, (CHUNK, CHUNK), 1))
        for h in range(H_C):
            wm_s[h] = jnp.where(causal, sw_ref[h], 0.0).astype(BF16)
        sp_s[...] = _softplus(-lam_ref[...])
        _pack_gate_weights(gwr_ref, gwi_ref, gw_s)
        zs_ref[...] = jnp.dot(_rms(xs_ref[...], g_ref[...]).astype(BF16), w_ref[...],
                              preferred_element_type=F32)
        normalise(x_ref, 0)
        z_even[...] = jnp.dot(hn_s[...], w_ref[...], preferred_element_type=F32)

    @pl.when(s % (chunks_per_seq // 2) == 0)
    def _():
        xa_buf[0:A_TAIL, :] = jnp.zeros((A_TAIL, W_A), F32)
        carry[...] = jnp.zeros((1, W_A), F32)

    project_and_mix(x_ref, tl, z_odd, z_even, 0)
    project_and_mix(x_next_ref, 0, z_even, z_odd, tl)
    ca_out_ref[...] = xa_buf[0:A_TAIL, :]
    h_out_ref[...] = carry[...]


def _front(x, xs, g, w, p, layer, nb, seq, tl):
    ns = xs.shape[0]
    chunks_per_seq = seq // tl
    assert seq % tl == 0 and chunks_per_seq % 2 == 0 and tl % CHUNK == 0
    pairs = nb * chunks_per_seq // 2
    pairs_per_seq = chunks_per_seq // 2
    rows = nb * seq

    def per_layer(shape):
        return pl.BlockSpec((None,) + shape, lambda s: (layer,) + (0,) * len(shape))

    def pair(width):
        return pl.BlockSpec((2 * tl, width), lambda s: (s, 0))

    scratch = [
        pltpu.VMEM((tl, D_IN), F32), pltpu.VMEM((tl, D_IN), F32), pltpu.VMEM((tl, D_MODEL), BF16),
        pltpu.VMEM((A_TAIL + tl, W_A), F32),
        pltpu.VMEM((tl, W_A), F32), pltpu.VMEM((tl, W_A), F32), pltpu.VMEM((tl, W_A), F32),
        pltpu.VMEM((tl, W_A), F32), pltpu.VMEM((1, W_A), F32),
        pltpu.VMEM((tl, W_C), BF16), pltpu.VMEM((H_C, CHUNK, CHUNK), BF16), pltpu.VMEM((1, W_A), F32),
        pltpu.VMEM((H_A, HD_A, 2 * HD_A), BF16),
    ]
    assert len(scratch) == 3 + N_FRONT_SCRATCH
    scratch_bytes = (2 * tl * D_IN * 4 + tl * D_MODEL * 2
                     + (A_TAIL + 5 * tl) * W_A * 4 + tl * W_C * 2 + H_C * CHUNK * CHUNK * 2)
    return pl.pallas_call(
        functools.partial(_front_kernel, tl=tl, chunks_per_seq=chunks_per_seq),
        grid=(pairs,),
        in_specs=[
            pair(D_MODEL),
            pl.BlockSpec((tl, D_MODEL), lambda s: (jnp.minimum(2 * s + 2, 2 * pairs - 1), 0)),
            pl.BlockSpec((ns, D_MODEL), lambda s: (0, 0)),
            pl.BlockSpec((None, 1, D_MODEL), lambda s: (layer, 0, 0)),
            pl.BlockSpec((None, D_MODEL, D_IN), lambda s: (layer, 0, 0), pipeline_mode=pl.Buffered(1)),
            per_layer((K_A, W_A)), per_layer((1, W_A)),
            per_layer((H_A, HD_A, HD_A)), per_layer((H_A, HD_A, HD_A)),
            per_layer((1, W_A)), per_layer((1, W_A)), per_layer((1, W_A)),
            per_layer((1, W_C)), per_layer((1, W_C)), per_layer((H_C, CHUNK, CHUNK)),
            per_layer((CHUNK, H_C)),
        ],
        out_specs=[
            pair(W_A), pair(W_C), pair(W_B),
            pl.BlockSpec((None, A_TAIL, W_A), lambda s: (s // pairs_per_seq, 0, 0)),
            pl.BlockSpec((None, 1, W_A), lambda s: (s // pairs_per_seq, 0, 0)),
            pl.BlockSpec((ns, D_IN), lambda s: (0, 0)),
        ],
        out_shape=[
            jax.ShapeDtypeStruct((rows, W_A), BF16),
            jax.ShapeDtypeStruct((rows, W_C), BF16),
            jax.ShapeDtypeStruct((rows, W_B), F32),
            jax.ShapeDtypeStruct((nb, A_TAIL, W_A), F32),
            jax.ShapeDtypeStruct((nb, 1, W_A), F32),
            jax.ShapeDtypeStruct((ns, D_IN), F32),
        ],
        scratch_shapes=scratch,
        compiler_params=pltpu.CompilerParams(
            dimension_semantics=("arbitrary",),
            vmem_limit_bytes=_vmem_limit(
                [3 * tl * D_MODEL * 4, 2 * tl * (W_A + W_C) * 2, 2 * tl * W_B * 4, 1024 * 1024,
                 ns * D_MODEL * 4, ns * D_IN * 4],
                D_MODEL * D_IN * 2 + scratch_bytes, tl * D_IN * 4 + 4 * tl * W_A * 4),
        ),
        name="front",
    )(x, x, xs, g, w, p["conv_a_w"], p["conv_a_b"], p["gate_r_w"], p["gate_i_w"], p["gate_r_b"],
      p["gate_i_b"], p["lru_lambda"], p["sgu_ln_g"], p["sgu_ln_b"], p["sgu_w"], p["sgu_b_t"])


CONV_B_ROWS = 64
NORM_B_ROWS = 32
BACK_ROWS = 256
BACK_COLS = 2048


def _conv_b_tile(w_ref, ub_slab, conv_slab, r0, c, rows):
    first = B_TAIL - (K_B - 1)
    half = rows // 2
    cs = slice(c * LANES, (c + 1) * LANES)
    even = odd = None
    for kp in range(first, first + K_B + 1):
        x = ub_slab[c, pl.ds(r0 + kp, half, stride=2), :]
        if kp < first + K_B:
            term = w_ref[kp - first:kp - first + 1, cs] * x
            even = term if even is None else even + term
        if kp > first:
            term = w_ref[kp - first - 1:kp - first, cs] * x
            odd = term if odd is None else odd + term
    conv_slab[c, pl.ds(r0, half, stride=2), :] = even
    conv_slab[c, pl.ds(r0 + 1, half, stride=2), :] = odd


def _back_kernel(x_ref, ya_ref, yc_ref, ub_ref, ub_prev_ref, xs_ref, mixs_ref, w32_ref, cbw_ref, lbg_ref,
                 lbb_ref, o_ref, os_ref, ub_slab, conv_slab, yb_s, w_ref, *, tm, tiles_per_seq):
    i = pl.program_id(0)

    @pl.when(i == 0)
    def _():
        for c0 in range(0, D_MODEL, PROJ_COLS):
            w_ref[:, c0:c0 + PROJ_COLS] = w32_ref[:, c0:c0 + PROJ_COLS].astype(BF16)
        os_ref[...] = xs_ref[...] + jnp.dot(mixs_ref[...], w_ref[...], preferred_element_type=F32)

    n_slabs = W_B // LANES
    prev = jnp.where(i % tiles_per_seq == 0, 0.0, ub_prev_ref[...])
    for c in range(n_slabs):
        ub_slab[c, 0:B_TAIL, :] = prev[:, c * LANES:(c + 1) * LANES]
        ub_slab[c, B_TAIL:, :] = ub_ref[:, c * LANES:(c + 1) * LANES]

    def conv_rows(r):
        n = CONV_B_ROWS
        for c in range(n_slabs):
            _conv_b_tile(cbw_ref, ub_slab, conv_slab, r, c, n)
        for r1 in range(r, r + n, NORM_B_ROWS):
            conv = jnp.concatenate([conv_slab[c, r1:r1 + NORM_B_ROWS, :] for c in range(n_slabs)], axis=-1)
            y_b = _layer_norm(conv, lbg_ref[...], lbb_ref[...])
            yb_s[r1:r1 + NORM_B_ROWS, :] = jax.nn.silu(y_b).astype(BF16)

    def project(r, c0):
        rs, cs = slice(r, r + BACK_ROWS), slice(c0, c0 + BACK_COLS)
        o_ref[rs, cs] = (
            x_ref[rs, cs]
            + jnp.dot(ya_ref[rs, :], w_ref[OFF_YA:OFF_YA + W_A, cs], preferred_element_type=F32)
            + jnp.dot(yb_s[rs, :], w_ref[OFF_YB:OFF_YB + W_B, cs], preferred_element_type=F32)
            + jnp.dot(yc_ref[rs, :], w_ref[OFF_YC:OFF_YC + W_C, cs], preferred_element_type=F32))

    fillers = [(functools.partial(project, r, c0), (r + BACK_ROWS) / tm)
               for r in range(0, tm, BACK_ROWS) for c0 in range(0, D_MODEL, BACK_COLS)]
    _interleave([(CONV_B_ROWS, functools.partial(conv_rows, r)) for r in range(0, tm, CONV_B_ROWS)],
                fillers)


def _back(x, ya, yc, ub, xs, mix_s, w, p, layer, seq, tm):
    rows, ns = x.shape[0], xs.shape[0]
    assert seq % tm == 0 and tm % B_TAIL == 0 and tm % CONV_B_ROWS == 0 and tm % BACK_ROWS == 0
    tails_per_tile = tm // B_TAIL

    def per_layer(shape):
        return pl.BlockSpec((None,) + shape, lambda i: (layer,) + (0,) * len(shape))

    def tile(width):
        return pl.BlockSpec((tm, width), lambda i: (i, 0))

    def whole(width):
        return pl.BlockSpec((ns, width), lambda i: (0, 0))

    return pl.pallas_call(
        functools.partial(_back_kernel, tm=tm, tiles_per_seq=seq // tm),
        grid=(rows // tm,),
        in_specs=[
            tile(D_MODEL), tile(W_A), tile(W_C), tile(W_B),
            pl.BlockSpec((B_TAIL, W_B), lambda i: (jnp.maximum(i * tails_per_tile - 1, 0), 0)),
            whole(D_MODEL), whole(D_MODEL),
            pl.BlockSpec((None, D_MODEL, D_MODEL), lambda i: (layer, 0, 0), pipeline_mode=pl.Buffered(1)),
            per_layer((K_B, W_B)), per_layer((1, W_B)), per_layer((1, W_B)),
        ],
        out_specs=[tile(D_MODEL), whole(D_MODEL)],
        out_shape=[jax.ShapeDtypeStruct((rows, D_MODEL), F32), jax.ShapeDtypeStruct((ns, D_MODEL), F32)],
        scratch_shapes=[pltpu.VMEM((W_B // LANES, B_TAIL + tm, LANES), F32),
                        pltpu.VMEM((W_B // LANES, tm, LANES), F32),
                        pltpu.VMEM((tm, W_B), BF16), pltpu.VMEM((D_MODEL, D_MODEL), BF16)],
        compiler_params=pltpu.CompilerParams(
            dimension_semantics=("arbitrary",),
            vmem_limit_bytes=_vmem_limit(
                [tm * D_MODEL * 4, tm * (W_A + W_C) * 2, tm * W_B * 4, tm * D_MODEL * 4, 256 * 1024,
                 ns * D_MODEL * 10],
                D_MODEL * D_MODEL * (4 + 2) + (B_TAIL + 2 * tm) * W_B * 4 + tm * W_B * 2,
                2 * BACK_ROWS * BACK_COLS * 4),
        ),
        name="back",
    )(x, ya, yc, ub, ub, xs, mix_s, w, p["conv_b_w"], p["ln_b_g"], p["ln_b_b"])


def _mixer_sample_kernel(
        z_ref, ca_ref, h0_ref, cb_ref, caw_ref, cab_ref, gwr_ref, gwi_ref, br_ref, bi_ref, lam_ref,
        cbw_ref, lbg_ref, lbb_ref, sg_ref, sb_ref, sw0_ref, sb0_ref,
        mix_ref, ca_out_ref, h_out_ref, cb_out_ref, vn_out_ref):
    xa = z_ref[:, OFF_XA:OFF_XA + W_A]
    for k in range(K_A - 2):
        ca_out_ref[k] = ca_ref[k + 1]
    ca_out_ref[K_A - 2] = xa
    sp = _softplus(-lam_ref[...])
    for h in range(H_A):
        cs = slice(h * HD_A, (h + 1) * HD_A)
        conv_h = cab_ref[:, cs] + caw_ref[K_A - 1:K_A, cs] * xa[:, cs]
        for k in range(K_A - 1):
            conv_h = conv_h + caw_ref[k:k + 1, cs] * ca_ref[k, :, cs]
        conv_b = conv_h.astype(BF16)
        pre = jnp.concatenate(
            [jnp.dot(conv_b, gwr_ref[h].astype(BF16), preferred_element_type=F32),
             jnp.dot(conv_b, gwi_ref[h].astype(BF16), preferred_element_type=F32)], axis=-1)
        a, u = _lru_gates(conv_h, pre, br_ref[:, cs], bi_ref[:, cs], sp[:, cs])
        h_new = a * h0_ref[:, cs] + u
        h_out_ref[:, cs] = h_new
        mix_ref[:, OFF_YA + h * HD_A:OFF_YA + (h + 1) * HD_A] = (
            h_new * jax.nn.gelu(z_ref[:, OFF_GA + h * HD_A:OFF_GA + (h + 1) * HD_A])).astype(BF16)

    ub = z_ref[:, OFF_XB:OFF_XB + W_B] * jax.nn.sigmoid(z_ref[:, OFF_GB:OFF_GB + W_B])
    acc = cbw_ref[K_B - 1:K_B, :] * ub
    for k in range(K_B - 1):
        acc = acc + cbw_ref[k:k + 1, :] * cb_ref[k]
    for k in range(K_B - 2):
        cb_out_ref[k] = cb_ref[k + 1]
    cb_out_ref[K_B - 2] = ub
    y_b = _layer_norm(acc, lbg_ref[...], lbb_ref[...])
    mix_ref[:, OFF_YB:OFF_YB + W_B] = jax.nn.silu(y_b).astype(BF16)

    g_c = jax.nn.gelu(z_ref[:, OFF_ZC:OFF_ZC + 2 * W_C])
    v_n = _layer_norm(g_c[:, W_C:], sg_ref[...], sb_ref[...])
    vn_out_ref[...] = v_n
    mix_ref[:, OFF_YC:OFF_YC + W_C] = (g_c[:, :W_C] * (sw0_ref[...] * v_n + sb0_ref[...])).astype(BF16)


def _mixer_sample(z, ca_t, h0, cb_t, p, layer, tb):
    nb = z.shape[0]

    def per_layer(shape):
        return pl.BlockSpec((None,) + shape, lambda b: (layer,) + (0,) * len(shape))

    return pl.pallas_call(
        _mixer_sample_kernel,
        grid=(nb // tb,),
        in_specs=[
            pl.BlockSpec((tb, D_IN), lambda b: (b, 0)),
            pl.BlockSpec((None, K_A - 1, tb, W_A), lambda b: (layer, 0, b, 0)),
            pl.BlockSpec((None, tb, W_A), lambda b: (layer, b, 0)),
            pl.BlockSpec((None, K_B - 1, tb, W_B), lambda b: (layer, 0, b, 0)),
            per_layer((K_A, W_A)), per_layer((1, W_A)),
            per_layer((H_A, HD_A, HD_A)), per_layer((H_A, HD_A, HD_A)),
            per_layer((1, W_A)), per_layer((1, W_A)), per_layer((1, W_A)),
            per_layer((K_B, W_B)), per_layer((1, W_B)), per_layer((1, W_B)),
            per_layer((1, W_C)), per_layer((1, W_C)), per_layer((1, W_C)), per_layer((1, W_C)),
        ],
        out_specs=[
            pl.BlockSpec((tb, D_MODEL), lambda b: (b, 0)),
            pl.BlockSpec((K_A - 1, tb, W_A), lambda b: (0, b, 0)),
            pl.BlockSpec((tb, W_A), lambda b: (b, 0)),
            pl.BlockSpec((K_B - 1, tb, W_B), lambda b: (0, b, 0)),
            pl.BlockSpec((tb, W_C), lambda b: (b, 0)),
        ],
        out_shape=[
            jax.ShapeDtypeStruct((nb, D_MODEL), BF16),
            jax.ShapeDtypeStruct((K_A - 1, nb, W_A), F32),
            jax.ShapeDtypeStruct((nb, W_A), F32),
            jax.ShapeDtypeStruct((K_B - 1, nb, W_B), F32),
            jax.ShapeDtypeStruct((nb, W_C), F32),
        ],
        compiler_params=pltpu.CompilerParams(
            dimension_semantics=("arbitrary",),
            vmem_limit_bytes=_vmem_limit(
                [tb * D_IN * 4, 2 * (K_A - 1) * tb * W_A * 4, 2 * tb * W_A * 4,
                 2 * (K_B - 1) * tb * W_B * 4, tb * D_MODEL * 2, tb * W_C * 4, 1024 * 1024],
                0, 8 * tb * W_A * 4),
        ),
        name="mixer_sample",
    )(z, ca_t, h0, cb_t, p["conv_a_w"], p["conv_a_b"], p["gate_r_w"], p["gate_i_w"], p["gate_r_b"],
      p["gate_i_b"], p["lru_lambda"], p["conv_b_w"], p["ln_b_g"], p["ln_b_b"], p["sgu_ln_g"], p["sgu_ln_b"],
      p["sgu_w00"], p["sgu_b0"])


def kernel(x_prompt, x_sample, state_conv_a, state_lru_h, state_conv_b, norm_mix, w_in, conv_a_w,
           conv_a_b, gate_r_w, gate_r_b, gate_i_w, gate_i_b, lru_lambda, conv_b_w, ln_b_g, ln_b_b,
           sgu_ln_g, sgu_ln_b, sgu_w, sgu_b, w_out, norm_ffn, w_ff1, w_ff2, norm_final):
    depth = w_in.shape[0]
    nb, seq, _ = x_prompt.shape
    ns = x_sample.shape[0]

    def row(v):
        return v[:, None, :]

    p = {
        "conv_a_w": conv_a_w, "conv_a_b": row(conv_a_b),
        "gate_r_w": gate_r_w, "gate_i_w": gate_i_w,
        "gate_r_b": row(gate_r_b), "gate_i_b": row(gate_i_b), "lru_lambda": row(lru_lambda),
        "conv_b_w": conv_b_w, "ln_b_g": row(ln_b_g), "ln_b_b": row(ln_b_b),
        "sgu_ln_g": row(sgu_ln_g), "sgu_ln_b": row(sgu_ln_b),
        "sgu_w": sgu_w, "sgu_b_t": jnp.swapaxes(sgu_b, 1, 2),
        "sgu_w00": row(jnp.repeat(sgu_w[:, :, 0, 0], HD_C, axis=-1)),
        "sgu_b0": row(jnp.repeat(sgu_b[:, :, 0], HD_C, axis=-1)),
    }
    g_mix, g_ffn, g_final = row(norm_mix), row(norm_ffn), norm_final[None, :]
    w_in_b = w_in.astype(BF16)

    ca_t = jnp.swapaxes(state_conv_a, 1, 2)
    cb_t = jnp.swapaxes(state_conv_b, 1, 2)

    xp = x_prompt.reshape(nb * seq, D_MODEL)
    xs = x_sample.reshape(ns, D_MODEL)
    ca_p, h_p, cb_p, ca_s, h_s, cb_s, v_s = [], [], [], [], [], [], []
    for l in range(depth):
        last = g_final if l == depth - 1 else None
        ya, yc, ub, ca_l, h_l, zs = _front(xp, xs, g_mix, w_in_b, p, l, nb, seq, tl=256)
        ca_p.append(ca_l[:, A_TAIL - (K_A - 1):, :])
        h_p.append(h_l[:, 0, :])
        cb_p.append(ub.reshape(nb, seq, W_B)[:, seq - (K_B - 1):, :])
        mix_s, cas_l, hs_l, cbs_l, v_l = _mixer_sample(zs, ca_t, state_lru_h, cb_t, p, l, tb=32)
        ca_s.append(jnp.swapaxes(cas_l, 0, 1))
        h_s.append(hs_l)
        cb_s.append(jnp.swapaxes(cbs_l, 0, 1))
        v_s.append(v_l[:, None, :])
        xp, xs = _back(xp, ya, yc, ub, xs, mix_s, w_out, p, l, seq, tm=512)
        xp, xs = _ffn(xp, xs, g_ffn, w_ff1, w_ff2, l, tm=1024, tf=512, g_final=last)

    return (xp.reshape(nb, seq, D_MODEL), xs.reshape(ns, 1, D_MODEL),
            jnp.stack(ca_p), jnp.stack(h_p), jnp.stack(cb_p),
            jnp.stack(ca_s), jnp.stack(h_s), jnp.stack(cb_s), jnp.stack(v_s))
```

```python
import functools

import jax
import jax.numpy as jnp
from jax import lax
from jax.experimental import pallas as pl
from jax.experimental.pallas import tpu as pltpu

F32 = jnp.float32
BF16 = jnp.bfloat16

D_MODEL = 2048
W_A = 1024
H_A = 8
HD_A = W_A // H_A
K_A = 4
LRU_C = 8.0
W_B = 512
K_B = 31
W_C = 512
H_C = 4
HD_C = W_C // H_C
CHUNK = 128
D_IN = 2 * (W_A + W_B + W_C)
D_FF = 4 * D_MODEL
EPS = 1e-6

OFF_XA, OFF_GA, OFF_XB, OFF_GB, OFF_ZC = 0, W_A, 2 * W_A, 2 * W_A + W_B, 2 * W_A + 2 * W_B
OFF_YA, OFF_YB, OFF_YC = 0, W_A, W_A + W_B

SUBLANES = 8
LANES = 128
A_TAIL = SUBLANES
B_TAIL = 32
V7X_VMEM_BYTES = 64 * 1024 * 1024


def _vmem_limit(block_bytes, scratch_bytes, temp_bytes):
    need = 2 * sum(block_bytes) + scratch_bytes + temp_bytes
    return int(min(need + need // 8, V7X_VMEM_BYTES - 4 * 1024 * 1024))


def _rms(x, g):
    return (x * lax.rsqrt(jnp.mean(x * x, axis=-1, keepdims=True) + EPS)) * g


def _layer_norm(x, g, b):
    xc = x - jnp.mean(x, axis=-1, keepdims=True)
    y = xc * lax.rsqrt(jnp.mean(xc * xc, axis=-1, keepdims=True) + EPS)
    return y * g + b


def _softplus(x):
    return jnp.maximum(x, 0.0) + jnp.log1p(jnp.exp(-jnp.abs(x)))


def _lru_gates(conv_h, pre, b_r, b_i, sp):
    r = jax.nn.sigmoid(pre[:, :HD_A] + b_r)
    i = jax.nn.sigmoid(pre[:, HD_A:] + b_i)
    log_a = (-LRU_C * r) * sp
    a = jnp.exp(log_a)
    t = jnp.tanh(log_a)
    u = jnp.sqrt(-2.0 * t / (1.0 - t)) * (i * conv_h)
    return a, u


def _interleave(pieces, fillers):
    total = sum(cost for cost, _ in pieces)
    done, issued = 0, 0
    for cost, piece in pieces:
        while issued < len(fillers) and fillers[issued][1] * total <= done:
            fillers[issued][0]()
            issued += 1
        piece()
        done += cost
    for filler, _ in fillers[issued:]:
        filler()


FFN_COLS = 512


def _ffn_kernel(*refs, final_norm, cast_next):
    refs = list(refs)
    hf_ref = refs.pop()
    if cast_next:
        wq_ref = refs.pop()
    os_ref = refs.pop()
    op_ref = refs.pop()
    if cast_next:
        wq_ref[...] = refs.pop()[...].astype(BF16)
    gf_ref = refs.pop() if final_norm else None
    xp_ref, xs_ref, g_ref, w1_ref, w2_ref = refs
    j = pl.program_id(1)
    tm = xp_ref.shape[0]
    outs = [(op_ref, slice(0, tm)), (os_ref, slice(tm, None))]

    @pl.when(j == 0)
    def _():
        for x_ref, (o_ref, rs) in zip((xp_ref, xs_ref), outs):
            x = x_ref[...]
            hf_ref[rs, :] = _rms(x, g_ref[...]).astype(BF16)
            o_ref[...] = x

    h = jnp.dot(hf_ref[...], w1_ref[...].astype(BF16), preferred_element_type=F32)
    h = jnp.square(jnp.maximum(h, 0.0)).astype(BF16)
    for c0 in range(0, D_MODEL, FFN_COLS):
        cs = slice(c0, c0 + FFN_COLS)
        acc = jnp.dot(h, w2_ref[:, cs].astype(BF16), preferred_element_type=F32)
        for o_ref, rs in outs:
            o_ref[:, cs] += acc[rs, :]

    if final_norm:
        @pl.when(j == pl.num_programs(1) - 1)
        def _():
            for o_ref, _ in outs:
                o_ref[...] = _rms(o_ref[...], gf_ref[...])


def _ffn(xp, xs, g, w1, w2, layer, tm, tf, g_final=None, w_in=None):
    m, ns = xp.shape[0], xs.shape[0]
    n_tiles = m // tm
    ts = ns // n_tiles
    assert m % tm == 0 and ns % n_tiles == 0 and ts % (2 * SUBLANES) == 0
    final_norm = g_final is not None
    prompt_tile = pl.BlockSpec((tm, D_MODEL), lambda i, j: (i, 0))
    sample_tile = pl.BlockSpec((ts, D_MODEL), lambda i, j: (i, 0))
    in_specs = [
        prompt_tile, sample_tile,
        pl.BlockSpec((None, 1, D_MODEL), lambda i, j: (layer, 0, 0)),
        pl.BlockSpec((None, D_MODEL, tf), lambda i, j: (layer, 0, j)),
        pl.BlockSpec((None, tf, D_MODEL), lambda i, j: (layer, j, 0)),
    ]
    args = [xp, xs, g, w1, w2]
    if final_norm:
        in_specs.append(pl.BlockSpec((1, D_MODEL), lambda i, j: (0, 0)))
        args.append(g_final)
    rows = tm + ts
    n_f = D_FF // tf
    out_specs = [prompt_tile, sample_tile]
    out_shape = [jax.ShapeDtypeStruct((m, D_MODEL), F32), jax.ShapeDtypeStruct((ns, D_MODEL), F32)]
    if w_in is not None:
        w_rows = D_MODEL // (n_tiles * n_f)
        assert D_MODEL % (n_tiles * n_f) == 0 and w_rows % (2 * SUBLANES) == 0
        in_specs.append(pl.BlockSpec((None, w_rows, D_IN), lambda i, j: (layer + 1, i * n_f + j, 0)))
        args.append(w_in)
        out_specs.append(pl.BlockSpec((w_rows, D_IN), lambda i, j: (i * n_f + j, 0)))
        out_shape.append(jax.ShapeDtypeStruct((D_MODEL, D_IN), BF16))
    return pl.pallas_call(
        functools.partial(_ffn_kernel, final_norm=final_norm, cast_next=w_in is not None),
        grid=(n_tiles, n_f),
        in_specs=in_specs,
        out_specs=out_specs,
        out_shape=out_shape,
        input_output_aliases={0: 0},
        scratch_shapes=[pltpu.VMEM((rows, D_MODEL), BF16)],
        compiler_params=pltpu.CompilerParams(
            dimension_semantics=("arbitrary", "arbitrary"),
            vmem_limit_bytes=_vmem_limit(
                [D_MODEL * 4, D_MODEL * tf * 4, tf * D_MODEL * 4, 2 * rows * D_MODEL * 4],
                rows * D_MODEL * 2,
                rows * tf * 6 + rows * FFN_COLS * 4 + 2 * D_MODEL * tf * 2),
        ),
        name="ffn",
    )(*args)


N_FRONT_PARAMS = 11
N_FRONT_SCRATCH = 10
PROJ_COLS = 512


def _pack_gate_weights(gwr_ref, gwi_ref, gw_ref):
    for h in range(H_A):
        gw_ref[h, :, 0:HD_A] = gwr_ref[h].astype(BF16)
        gw_ref[h, :, HD_A:] = gwi_ref[h].astype(BF16)


def _front_pieces(z_ref, outs, r_out, prm, scr, tl):
    caw_ref, cab_ref, _, _, br_ref, bi_ref, _, sg_ref, sb_ref, _, sbt_ref = prm
    xa_buf, a_s, u_s, _, g_s, _, vn_s, wm_s, sp_s, gw_ref = scr
    _, yc_ref, ub_ref = outs
    pieces = []

    def c_norm(r):
        v = jax.nn.gelu(z_ref[r:r + CHUNK, OFF_ZC + W_C:OFF_ZC + 2 * W_C])
        vn_s[r:r + CHUNK, :] = _layer_norm(v, sg_ref[...], sb_ref[...]).astype(BF16)

    def c_head(r, h):
        cs = slice(h * HD_C, (h + 1) * HD_C)
        mixed = jnp.dot(wm_s[h], vn_s[r:r + CHUNK, cs], preferred_element_type=F32)
        mixed = mixed + sbt_ref[:, h:h + 1]
        u_c = jax.nn.gelu(z_ref[r:r + CHUNK, OFF_ZC + h * HD_C:OFF_ZC + (h + 1) * HD_C])
        yc_ref[r_out + r:r_out + r + CHUNK, cs] = (u_c * mixed).astype(BF16)

    def glu(r):
        ub_ref[r_out + r:r_out + r + CHUNK, :] = (
            z_ref[r:r + CHUNK, OFF_XB:OFF_XB + W_B]
            * jax.nn.sigmoid(z_ref[r:r + CHUNK, OFF_GB:OFF_GB + W_B]))

    def a_head(h):
        cs = slice(h * HD_A, (h + 1) * HD_A)
        xa_buf[A_TAIL:A_TAIL + tl, cs] = z_ref[:, OFF_XA + h * HD_A:OFF_XA + (h + 1) * HD_A]
        xa_full = xa_buf[0:A_TAIL + tl, cs]
        conv_h = cab_ref[:, cs] + caw_ref[K_A - 1:K_A, cs] * xa_full[A_TAIL:, :]
        for k in range(K_A - 1):
            shifted = pltpu.roll(xa_full, K_A - 1 - k, axis=0)[A_TAIL:, :]
            conv_h = conv_h + caw_ref[k:k + 1, cs] * shifted
        pre = jnp.dot(conv_h.astype(BF16), gw_ref[h], preferred_element_type=F32)
        a, u = _lru_gates(conv_h, pre, br_ref[:, cs], bi_ref[:, cs], sp_s[:, cs])
        a_s[:, cs] = a
        u_s[:, cs] = u
        g_s[:, cs] = jax.nn.gelu(z_ref[:, OFF_GA + h * HD_A:OFF_GA + (h + 1) * HD_A])

    other = []
    for r in range(0, tl, CHUNK):
        other.append((1500, functools.partial(c_norm, r)))
        for h in range(H_C):
            other.append((500, functools.partial(c_head, r, h)))
        other.append((600, functools.partial(glu, r)))
    per_gate = -(-len(other) // H_A)
    for h in range(H_A):
        pieces.append((2000 * tl // 256, functools.partial(a_head, h)))
        pieces.extend(other[h * per_gate:(h + 1) * per_gate])
    return pieces


def _front_kernel(x_ref, x_next_ref, xs_ref, g_ref, w_ref, *rest, tl, chunks_per_seq):
    prm = rest[:N_FRONT_PARAMS]
    gwr_ref, gwi_ref, lam_ref, sw_ref = prm[2], prm[3], prm[6], prm[9]
    ya_ref, yc_ref, ub_ref, ca_out_ref, h_out_ref, zs_ref = rest[N_FRONT_PARAMS:N_FRONT_PARAMS + 6]
    z_even, z_odd, hn_s = rest[N_FRONT_PARAMS + 6:N_FRONT_PARAMS + 9]
    scr = rest[N_FRONT_PARAMS + 9:]
    xa_buf, a_s, u_s, h_s, g_s, carry, _, wm_s, sp_s, gw_s = scr
    outs = (ya_ref, yc_ref, ub_ref)
    s = pl.program_id(0)

    def normalise(x_rows_ref, r0):
        hn_s[...] = _rms(x_rows_ref[r0:r0 + tl, :], g_ref[...]).astype(BF16)

    def project_and_mix(x_rows_ref, x_r0, z_next, z_cur, r_out):
        normalise(x_rows_ref, x_r0)

        def project(c0):
            z_next[:, c0:c0 + PROJ_COLS] = jnp.dot(
                hn_s[...], w_ref[:, c0:c0 + PROJ_COLS], preferred_element_type=F32)

        _interleave(_front_pieces(z_cur, outs, r_out, prm, scr, tl),
                    [(functools.partial(project, c0), c0 / D_IN) for c0 in range(0, D_IN, PROJ_COLS)])
        xa_buf[0:A_TAIL, :] = xa_buf[tl:tl + A_TAIL, :]

        def scan_row(t, h):
            h = a_s[pl.ds(t, 1), :] * h + u_s[pl.ds(t, 1), :]
            h_s[pl.ds(t, 1), :] = h
            return h

        carry[...] = lax.fori_loop(0, tl, scan_row, carry[...], unroll=True)
        ya_ref[r_out:r_out + tl, :] = (h_s[...] * g_s[...]).astype(BF16)

    @pl.when(s == 0)
    def _():
        causal = (lax.broadcasted_iota(jnp.int32, (CHUNK, CHUNK), 0)
                  >= lax.broadcasted_iota(jnp.int32, (CHUNK, CHUNK), 1))
        for h in range(H_C):
            wm_s[h] = jnp.where(causal, sw_ref[h], 0.0).astype(BF16)
        sp_s[...] = _softplus(-lam_ref[...])
        _pack_gate_weights(gwr_ref, gwi_ref, gw_s)
        zs_ref[...] = jnp.dot(_rms(xs_ref[...], g_ref[...]).astype(BF16), w_ref[...],
                              preferred_element_type=F32)
        normalise(x_ref, 0)
        z_even[...] = jnp.dot(hn_s[...], w_ref[...], preferred_element_type=F32)

    @pl.when(s % (chunks_per_seq // 2) == 0)
    def _():
        xa_buf[0:A_TAIL, :] = jnp.zeros((A_TAIL, W_A), F32)
        carry[...] = jnp.zeros((1, W_A), F32)

    project_and_mix(x_ref, tl, z_odd, z_even, 0)
    project_and_mix(x_next_ref, 0, z_even, z_odd, tl)
    ca_out_ref[...] = xa_buf[0:A_TAIL, :]
    h_out_ref[...] = carry[...]


def _front(x, xs, g, w, p, layer, nb, seq, tl):
    ns = xs.shape[0]
    chunks_per_seq = seq // tl
    assert seq % tl == 0 and chunks_per_seq % 2 == 0 and tl % CHUNK == 0
    pairs = nb * chunks_per_seq // 2
    pairs_per_seq = chunks_per_seq // 2
    rows = nb * seq

    def per_layer(shape):
        return pl.BlockSpec((None,) + shape, lambda s: (layer,) + (0,) * len(shape))

    def pair(width):
        return pl.BlockSpec((2 * tl, width), lambda s: (s, 0))

    scratch = [
        pltpu.VMEM((tl, D_IN), F32), pltpu.VMEM((tl, D_IN), F32), pltpu.VMEM((tl, D_MODEL), BF16),
        pltpu.VMEM((A_TAIL + tl, W_A), F32),
        pltpu.VMEM((tl, W_A), F32), pltpu.VMEM((tl, W_A), F32), pltpu.VMEM((tl, W_A), F32),
        pltpu.VMEM((tl, W_A), F32), pltpu.VMEM((1, W_A), F32),
        pltpu.VMEM((tl, W_C), BF16), pltpu.VMEM((H_C, CHUNK, CHUNK), BF16), pltpu.VMEM((1, W_A), F32),
        pltpu.VMEM((H_A, HD_A, 2 * HD_A), BF16),
    ]
    assert len(scratch) == 3 + N_FRONT_SCRATCH
    scratch_bytes = (2 * tl * D_IN * 4 + tl * D_MODEL * 2
                     + (A_TAIL + 5 * tl) * W_A * 4 + tl * W_C * 2 + H_C * CHUNK * CHUNK * 2)
    return pl.pallas_call(
        functools.partial(_front_kernel, tl=tl, chunks_per_seq=chunks_per_seq),
        grid=(pairs,),
        in_specs=[
            pair(D_MODEL),
            pl.BlockSpec((tl, D_MODEL), lambda s: (jnp.minimum(2 * s + 2, 2 * pairs - 1), 0)),
            pl.BlockSpec((ns, D_MODEL), lambda s: (0, 0)),
            pl.BlockSpec((None, 1, D_MODEL), lambda s: (layer, 0, 0)),
            pl.BlockSpec((D_MODEL, D_IN), lambda s: (0, 0), pipeline_mode=pl.Buffered(1)),
            per_layer((K_A, W_A)), per_layer((1, W_A)),
            per_layer((H_A, HD_A, HD_A)), per_layer((H_A, HD_A, HD_A)),
            per_layer((1, W_A)), per_layer((1, W_A)), per_layer((1, W_A)),
            per_layer((1, W_C)), per_layer((1, W_C)), per_layer((H_C, CHUNK, CHUNK)),
            per_layer((CHUNK, H_C)),
        ],
        out_specs=[
            pair(W_A), pair(W_C), pair(W_B),
            pl.BlockSpec((None, A_TAIL, W_A), lambda s: (s // pairs_per_seq, 0, 0)),
            pl.BlockSpec((None, 1, W_A), lambda s: (s // pairs_per_seq, 0, 0)),
            pl.BlockSpec((ns, D_IN), lambda s: (0, 0)),
        ],
        out_shape=[
            jax.ShapeDtypeStruct((rows, W_A), BF16),
            jax.ShapeDtypeStruct((rows, W_C), BF16),
            jax.ShapeDtypeStruct((rows, W_B), F32),
            jax.ShapeDtypeStruct((nb, A_TAIL, W_A), F32),
            jax.ShapeDtypeStruct((nb, 1, W_A), F32),
            jax.ShapeDtypeStruct((ns, D_IN), F32),
        ],
        scratch_shapes=scratch,
        compiler_params=pltpu.CompilerParams(
            dimension_semantics=("arbitrary",),
            vmem_limit_bytes=_vmem_limit(
                [3 * tl * D_MODEL * 4, 2 * tl * (W_A + W_C) * 2, 2 * tl * W_B * 4, 1024 * 1024,
                 ns * D_MODEL * 4, ns * D_IN * 4],
                D_MODEL * D_IN * 2 + scratch_bytes, tl * D_IN * 4 + 4 * tl * W_A * 4),
        ),
        name="front",
    )(x, x, xs, g, w, p["conv_a_w"], p["conv_a_b"], p["gate_r_w"], p["gate_i_w"], p["gate_r_b"],
      p["gate_i_b"], p["lru_lambda"], p["sgu_ln_g"], p["sgu_ln_b"], p["sgu_w"], p["sgu_b_t"])


CONV_B_ROWS = 64
NORM_B_ROWS = 32
BACK_ROWS = 256
BACK_COLS = 2048


def _conv_b_tile(w_ref, ub_slab, conv_slab, r0, c, rows):
    first = B_TAIL - (K_B - 1)
    half = rows // 2
    cs = slice(c * LANES, (c + 1) * LANES)
    even = odd = None
    for kp in range(first, first + K_B + 1):
        x = ub_slab[c, pl.ds(r0 + kp, half, stride=2), :]
        if kp < first + K_B:
            term = w_ref[kp - first:kp - first + 1, cs] * x
            even = term if even is None else even + term
        if kp > first:
            term = w_ref[kp - first - 1:kp - first, cs] * x
            odd = term if odd is None else odd + term
    conv_slab[c, pl.ds(r0, half, stride=2), :] = even
    conv_slab[c, pl.ds(r0 + 1, half, stride=2), :] = odd


def _back_kernel(x_ref, ya_ref, yc_ref, ub_ref, ub_prev_ref, xs_ref, mixs_ref, w32_ref, cbw_ref, lbg_ref,
                 lbb_ref, o_ref, os_ref, ub_slab, conv_slab, yb_s, w_ref, *, tm, tiles_per_seq):
    i = pl.program_id(0)

    @pl.when(i == 0)
    def _():
        for c0 in range(0, D_MODEL, PROJ_COLS):
            w_ref[:, c0:c0 + PROJ_COLS] = w32_ref[:, c0:c0 + PROJ_COLS].astype(BF16)
        os_ref[...] = xs_ref[...] + jnp.dot(mixs_ref[...], w_ref[...], preferred_element_type=F32)

    n_slabs = W_B // LANES
    prev = jnp.where(i % tiles_per_seq == 0, 0.0, ub_prev_ref[...])
    for c in range(n_slabs):
        ub_slab[c, 0:B_TAIL, :] = prev[:, c * LANES:(c + 1) * LANES]
        ub_slab[c, B_TAIL:, :] = ub_ref[:, c * LANES:(c + 1) * LANES]

    def conv_rows(r):
        n = CONV_B_ROWS
        for c in range(n_slabs):
            _conv_b_tile(cbw_ref, ub_slab, conv_slab, r, c, n)
        for r1 in range(r, r + n, NORM_B_ROWS):
            conv = jnp.concatenate([conv_slab[c, r1:r1 + NORM_B_ROWS, :] for c in range(n_slabs)], axis=-1)
            y_b = _layer_norm(conv, lbg_ref[...], lbb_ref[...])
            yb_s[r1:r1 + NORM_B_ROWS, :] = jax.nn.silu(y_b).astype(BF16)

    def project(r, c0):
        rs, cs = slice(r, r + BACK_ROWS), slice(c0, c0 + BACK_COLS)
        o_ref[rs, cs] = (
            x_ref[rs, cs]
            + jnp.dot(ya_ref[rs, :], w_ref[OFF_YA:OFF_YA + W_A, cs], preferred_element_type=F32)
            + jnp.dot(yb_s[rs, :], w_ref[OFF_YB:OFF_YB + W_B, cs], preferred_element_type=F32)
            + jnp.dot(yc_ref[rs, :], w_ref[OFF_YC:OFF_YC + W_C, cs], preferred_element_type=F32))

    fillers = [(functools.partial(project, r, c0), (r + BACK_ROWS) / tm)
               for r in range(0, tm, BACK_ROWS) for c0 in range(0, D_MODEL, BACK_COLS)]
    _interleave([(CONV_B_ROWS, functools.partial(conv_rows, r)) for r in range(0, tm, CONV_B_ROWS)],
                fillers)


def _back(x, ya, yc, ub, xs, mix_s, w, p, layer, seq, tm):
    rows, ns = x.shape[0], xs.shape[0]
    assert seq % tm == 0 and tm % B_TAIL == 0 and tm % CONV_B_ROWS == 0 and tm % BACK_ROWS == 0
    tails_per_tile = tm // B_TAIL

    def per_layer(shape):
        return pl.BlockSpec((None,) + shape, lambda i: (layer,) + (0,) * len(shape))

    def tile(width):
        return pl.BlockSpec((tm, width), lambda i: (i, 0))

    def whole(width):
        return pl.BlockSpec((ns, width), lambda i: (0, 0))

    return pl.pallas_call(
        functools.partial(_back_kernel, tm=tm, tiles_per_seq=seq // tm),
        grid=(rows // tm,),
        in_specs=[
            tile(D_MODEL), tile(W_A), tile(W_C), tile(W_B),
            pl.BlockSpec((B_TAIL, W_B), lambda i: (jnp.maximum(i * tails_per_tile - 1, 0), 0)),
            whole(D_MODEL), whole(D_MODEL),
            pl.BlockSpec((None, D_MODEL, D_MODEL), lambda i: (layer, 0, 0), pipeline_mode=pl.Buffered(1)),
            per_layer((K_B, W_B)), per_layer((1, W_B)), per_layer((1, W_B)),
        ],
        out_specs=[tile(D_MODEL), whole(D_MODEL)],
        out_shape=[jax.ShapeDtypeStruct((rows, D_MODEL), F32), jax.ShapeDtypeStruct((ns, D_MODEL), F32)],
        scratch_shapes=[pltpu.VMEM((W_B // LANES, B_TAIL + tm, LANES), F32),
                        pltpu.VMEM((W_B // LANES, tm, LANES), F32),
                        pltpu.VMEM((tm, W_B), BF16), pltpu.VMEM((D_MODEL, D_MODEL), BF16)],
        compiler_params=pltpu.CompilerParams(
            dimension_semantics=("arbitrary",),
            vmem_limit_bytes=_vmem_limit(
                [tm * D_MODEL * 4, tm * (W_A + W_C) * 2, tm * W_B * 4, tm * D_MODEL * 4, 256 * 1024,
                 ns * D_MODEL * 10],
                D_MODEL * D_MODEL * (4 + 2) + (B_TAIL + 2 * tm) * W_B * 4 + tm * W_B * 2,
                2 * BACK_ROWS * BACK_COLS * 4),
        ),
        name="back",
    )(x, ya, yc, ub, ub, xs, mix_s, w, p["conv_b_w"], p["ln_b_g"], p["ln_b_b"])


def _mixer_sample_kernel(
        z_ref, ca_ref, h0_ref, cb_ref, caw_ref, cab_ref, gwr_ref, gwi_ref, br_ref, bi_ref, lam_ref,
        cbw_ref, lbg_ref, lbb_ref, sg_ref, sb_ref, sw0_ref, sb0_ref,
        mix_ref, ca_out_ref, h_out_ref, cb_out_ref, vn_out_ref):
    xa = z_ref[:, OFF_XA:OFF_XA + W_A]
    for k in range(K_A - 2):
        ca_out_ref[k] = ca_ref[k + 1]
    ca_out_ref[K_A - 2] = xa
    sp = _softplus(-lam_ref[...])
    for h in range(H_A):
        cs = slice(h * HD_A, (h + 1) * HD_A)
        conv_h = cab_ref[:, cs] + caw_ref[K_A - 1:K_A, cs] * xa[:, cs]
        for k in range(K_A - 1):
            conv_h = conv_h + caw_ref[k:k + 1, cs] * ca_ref[k, :, cs]
        conv_b = conv_h.astype(BF16)
        pre = jnp.concatenate(
            [jnp.dot(conv_b, gwr_ref[h].astype(BF16), preferred_element_type=F32),
             jnp.dot(conv_b, gwi_ref[h].astype(BF16), preferred_element_type=F32)], axis=-1)
        a, u = _lru_gates(conv_h, pre, br_ref[:, cs], bi_ref[:, cs], sp[:, cs])
        h_new = a * h0_ref[:, cs] + u
        h_out_ref[:, cs] = h_new
        mix_ref[:, OFF_YA + h * HD_A:OFF_YA + (h + 1) * HD_A] = (
            h_new * jax.nn.gelu(z_ref[:, OFF_GA + h * HD_A:OFF_GA + (h + 1) * HD_A])).astype(BF16)

    ub = z_ref[:, OFF_XB:OFF_XB + W_B] * jax.nn.sigmoid(z_ref[:, OFF_GB:OFF_GB + W_B])
    acc = cbw_ref[K_B - 1:K_B, :] * ub
    for k in range(K_B - 1):
        acc = acc + cbw_ref[k:k + 1, :] * cb_ref[k]
    for k in range(K_B - 2):
        cb_out_ref[k] = cb_ref[k + 1]
    cb_out_ref[K_B - 2] = ub
    y_b = _layer_norm(acc, lbg_ref[...], lbb_ref[...])
    mix_ref[:, OFF_YB:OFF_YB + W_B] = jax.nn.silu(y_b).astype(BF16)

    g_c = jax.nn.gelu(z_ref[:, OFF_ZC:OFF_ZC + 2 * W_C])
    v_n = _layer_norm(g_c[:, W_C:], sg_ref[...], sb_ref[...])
    vn_out_ref[...] = v_n
    mix_ref[:, OFF_YC:OFF_YC + W_C] = (g_c[:, :W_C] * (sw0_ref[...] * v_n + sb0_ref[...])).astype(BF16)


def _mixer_sample(z, ca_t, h0, cb_t, p, layer, tb):
    nb = z.shape[0]

    def per_layer(shape):
        return pl.BlockSpec((None,) + shape, lambda b: (layer,) + (0,) * len(shape))

    return pl.pallas_call(
        _mixer_sample_kernel,
        grid=(nb // tb,),
        in_specs=[
            pl.BlockSpec((tb, D_IN), lambda b: (b, 0)),
            pl.BlockSpec((None, K_A - 1, tb, W_A), lambda b: (layer, 0, b, 0)),
            pl.BlockSpec((None, tb, W_A), lambda b: (layer, b, 0)),
            pl.BlockSpec((None, K_B - 1, tb, W_B), lambda b: (layer, 0, b, 0)),
            per_layer((K_A, W_A)), per_layer((1, W_A)),
            per_layer((H_A, HD_A, HD_A)), per_layer((H_A, HD_A, HD_A)),
            per_layer((1, W_A)), per_layer((1, W_A)), per_layer((1, W_A)),
            per_layer((K_B, W_B)), per_layer((1, W_B)), per_layer((1, W_B)),
            per_layer((1, W_C)), per_layer((1, W_C)), per_layer((1, W_C)), per_layer((1, W_C)),
        ],
        out_specs=[
            pl.BlockSpec((tb, D_MODEL), lambda b: (b, 0)),
            pl.BlockSpec((K_A - 1, tb, W_A), lambda b: (0, b, 0)),
            pl.BlockSpec((tb, W_A), lambda b: (b, 0)),
            pl.BlockSpec((K_B - 1, tb, W_B), lambda b: (0, b, 0)),
            pl.BlockSpec((tb, W_C), lambda b: (b, 0)),
        ],
        out_shape=[
            jax.ShapeDtypeStruct((nb, D_MODEL), BF16),
            jax.ShapeDtypeStruct((K_A - 1, nb, W_A), F32),
            jax.ShapeDtypeStruct((nb, W_A), F32),
            jax.ShapeDtypeStruct((K_B - 1, nb, W_B), F32),
            jax.ShapeDtypeStruct((nb, W_C), F32),
        ],
        compiler_params=pltpu.CompilerParams(
            dimension_semantics=("arbitrary",),
            vmem_limit_bytes=_vmem_limit(
                [tb * D_IN * 4, 2 * (K_A - 1) * tb * W_A * 4, 2 * tb * W_A * 4,
                 2 * (K_B - 1) * tb * W_B * 4, tb * D_MODEL * 2, tb * W_C * 4, 1024 * 1024],
                0, 8 * tb * W_A * 4),
        ),
        name="mixer_sample",
    )(z, ca_t, h0, cb_t, p["conv_a_w"], p["conv_a_b"], p["gate_r_w"], p["gate_i_w"], p["gate_r_b"],
      p["gate_i_b"], p["lru_lambda"], p["conv_b_w"], p["ln_b_g"], p["ln_b_b"], p["sgu_ln_g"], p["sgu_ln_b"],
      p["sgu_w00"], p["sgu_b0"])


def kernel(x_prompt, x_sample, state_conv_a, state_lru_h, state_conv_b, norm_mix, w_in, conv_a_w,
           conv_a_b, gate_r_w, gate_r_b, gate_i_w, gate_i_b, lru_lambda, conv_b_w, ln_b_g, ln_b_b,
           sgu_ln_g, sgu_ln_b, sgu_w, sgu_b, w_out, norm_ffn, w_ff1, w_ff2, norm_final):
    depth = w_in.shape[0]
    nb, seq, _ = x_prompt.shape
    ns = x_sample.shape[0]

    def row(v):
        return v[:, None, :]

    p = {
        "conv_a_w": conv_a_w, "conv_a_b": row(conv_a_b),
        "gate_r_w": gate_r_w, "gate_i_w": gate_i_w,
        "gate_r_b": row(gate_r_b), "gate_i_b": row(gate_i_b), "lru_lambda": row(lru_lambda),
        "conv_b_w": conv_b_w, "ln_b_g": row(ln_b_g), "ln_b_b": row(ln_b_b),
        "sgu_ln_g": row(sgu_ln_g), "sgu_ln_b": row(sgu_ln_b),
        "sgu_w": sgu_w, "sgu_b_t": jnp.swapaxes(sgu_b, 1, 2),
        "sgu_w00": row(jnp.repeat(sgu_w[:, :, 0, 0], HD_C, axis=-1)),
        "sgu_b0": row(jnp.repeat(sgu_b[:, :, 0], HD_C, axis=-1)),
    }
    g_mix, g_ffn, g_final = row(norm_mix), row(norm_ffn), norm_final[None, :]
    w_in_l = w_in[0].astype(BF16)

    ca_t = jnp.swapaxes(state_conv_a, 1, 2)
    cb_t = jnp.swapaxes(state_conv_b, 1, 2)

    xp = x_prompt.reshape(nb * seq, D_MODEL)
    xs = x_sample.reshape(ns, D_MODEL)
    ca_p, h_p, cb_p, ca_s, h_s, cb_s, v_s = [], [], [], [], [], [], []
    for l in range(depth):
        ya, yc, ub, ca_l, h_l, zs = _front(xp, xs, g_mix, w_in_l, p, l, nb, seq, tl=256)
        ca_p.append(ca_l[:, A_TAIL - (K_A - 1):, :])
        h_p.append(h_l[:, 0, :])
        cb_p.append(ub.reshape(nb, seq, W_B)[:, seq - (K_B - 1):, :])
        mix_s, cas_l, hs_l, cbs_l, v_l = _mixer_sample(zs, ca_t, state_lru_h, cb_t, p, l, tb=32)
        ca_s.append(jnp.swapaxes(cas_l, 0, 1))
        h_s.append(hs_l)
        cb_s.append(jnp.swapaxes(cbs_l, 0, 1))
        v_s.append(v_l[:, None, :])
        xp, xs = _back(xp, ya, yc, ub, xs, mix_s, w_out, p, l, seq, tm=512)
        if l + 1 < depth:
            xp, xs, w_in_l = _ffn(xp, xs, g_ffn, w_ff1, w_ff2, l, tm=1024, tf=512, w_in=w_in)
        else:
            xp, xs = _ffn(xp, xs, g_ffn, w_ff1, w_ff2, l, tm=1024, tf=512, g_final=g_final)

    return (xp.reshape(nb, seq, D_MODEL), xs.reshape(ns, 1, D_MODEL),
            jnp.stack(ca_p), jnp.stack(h_p), jnp.stack(cb_p),
            jnp.stack(ca_s), jnp.stack(h_s), jnp.stack(cb_s), jnp.stack(v_s))
```

```python
import functools

import jax
import jax.numpy as jnp
from jax import lax
from jax.experimental import pallas as pl
from jax.experimental.pallas import tpu as pltpu

F32 = jnp.float32
BF16 = jnp.bfloat16

D_MODEL = 2048
W_A = 1024
H_A = 8
HD_A = W_A // H_A
K_A = 4
LRU_C = 8.0
W_B = 512
K_B = 31
W_C = 512
H_C = 4
HD_C = W_C // H_C
CHUNK = 128
D_IN = 2 * (W_A + W_B + W_C)
D_FF = 4 * D_MODEL
EPS = 1e-6

OFF_XA, OFF_GA, OFF_XB, OFF_GB, OFF_ZC = 0, W_A, 2 * W_A, 2 * W_A + W_B, 2 * W_A + 2 * W_B
OFF_YA, OFF_YB, OFF_YC = 0, W_A, W_A + W_B

SUBLANES = 8
LANES = 128
A_TAIL = SUBLANES
B_TAIL = 32
V7X_VMEM_BYTES = 64 * 1024 * 1024


def _vmem_limit(block_bytes, scratch_bytes, temp_bytes):
    need = 2 * sum(block_bytes) + scratch_bytes + temp_bytes
    return int(min(need + need // 8, V7X_VMEM_BYTES - 4 * 1024 * 1024))


def _rms(x, g):
    return (x * lax.rsqrt(jnp.mean(x * x, axis=-1, keepdims=True) + EPS)) * g


def _layer_norm(x, g, b):
    xc = x - jnp.mean(x, axis=-1, keepdims=True)
    y = xc * lax.rsqrt(jnp.mean(xc * xc, axis=-1, keepdims=True) + EPS)
    return y * g + b


def _softplus(x):
    return jnp.maximum(x, 0.0) + jnp.log1p(jnp.exp(-jnp.abs(x)))


def _lru_gates(conv_h, pre, b_r, b_i, sp):
    r = jax.nn.sigmoid(pre[:, :HD_A] + b_r)
    i = jax.nn.sigmoid(pre[:, HD_A:] + b_i)
    log_a = (-LRU_C * r) * sp
    a = jnp.exp(log_a)
    t = jnp.tanh(log_a)
    u = jnp.sqrt(-2.0 * t / (1.0 - t)) * (i * conv_h)
    return a, u


def _interleave(pieces, fillers):
    total = sum(cost for cost, _ in pieces)
    done, issued = 0, 0
    for cost, piece in pieces:
        while issued < len(fillers) and fillers[issued][1] * total <= done:
            fillers[issued][0]()
            issued += 1
        piece()
        done += cost
    for filler, _ in fillers[issued:]:
        filler()


FFN_COLS = 512


def _ffn_kernel(*refs, final_norm, cast_next):
    refs = list(refs)
    hf_ref = refs.pop()
    if cast_next:
        wq_ref = refs.pop()
    os_ref = refs.pop()
    op_ref = refs.pop()
    if cast_next:
        wq_ref[...] = refs.pop()[...].astype(BF16)
    gf_ref = refs.pop() if final_norm else None
    xp_ref, xs_ref, g_ref, w1_ref, w2_ref = refs
    j = pl.program_id(1)
    tm = xp_ref.shape[0]
    outs = [(op_ref, slice(0, tm)), (os_ref, slice(tm, None))]

    @pl.when(j == 0)
    def _():
        for x_ref, (o_ref, rs) in zip((xp_ref, xs_ref), outs):
            x = x_ref[...]
            hf_ref[rs, :] = _rms(x, g_ref[...]).astype(BF16)
            o_ref[...] = x

    h = jnp.dot(hf_ref[...], w1_ref[...].astype(BF16), preferred_element_type=F32)
    h = jnp.square(jnp.maximum(h, 0.0)).astype(BF16)
    for c0 in range(0, D_MODEL, FFN_COLS):
        cs = slice(c0, c0 + FFN_COLS)
        acc = jnp.dot(h, w2_ref[:, cs].astype(BF16), preferred_element_type=F32)
        for o_ref, rs in outs:
            o_ref[:, cs] += acc[rs, :]

    if final_norm:
        @pl.when(j == pl.num_programs(1) - 1)
        def _():
            for o_ref, _ in outs:
                o_ref[...] = _rms(o_ref[...], gf_ref[...])


def _ffn(xp, xs, g, w1, w2, layer, tm, tf, g_final=None, w_in=None):
    m, ns = xp.shape[0], xs.shape[0]
    n_tiles = m // tm
    ts = ns // n_tiles
    assert m % tm == 0 and ns % n_tiles == 0 and ts % (2 * SUBLANES) == 0
    final_norm = g_final is not None
    prompt_tile = pl.BlockSpec((tm, D_MODEL), lambda i, j: (i, 0))
    sample_tile = pl.BlockSpec((ts, D_MODEL), lambda i, j: (i, 0))
    in_specs = [
        prompt_tile, sample_tile,
        pl.BlockSpec((None, 1, D_MODEL), lambda i, j: (layer, 0, 0)),
        pl.BlockSpec((None, D_MODEL, tf), lambda i, j: (layer, 0, j)),
        pl.BlockSpec((None, tf, D_MODEL), lambda i, j: (layer, j, 0)),
    ]
    args = [xp, xs, g, w1, w2]
    if final_norm:
        in_specs.append(pl.BlockSpec((1, D_MODEL), lambda i, j: (0, 0)))
        args.append(g_final)
    rows = tm + ts
    n_f = D_FF // tf
    out_specs = [prompt_tile, sample_tile]
    out_shape = [jax.ShapeDtypeStruct((m, D_MODEL), F32), jax.ShapeDtypeStruct((ns, D_MODEL), F32)]
    if w_in is not None:
        w_rows = D_MODEL // (n_tiles * n_f)
        assert D_MODEL % (n_tiles * n_f) == 0 and w_rows % (2 * SUBLANES) == 0
        in_specs.append(pl.BlockSpec((None, w_rows, D_IN), lambda i, j: (layer + 1, i * n_f + j, 0)))
        args.append(w_in)
        out_specs.append(pl.BlockSpec((w_rows, D_IN), lambda i, j: (i * n_f + j, 0)))
        out_shape.append(jax.ShapeDtypeStruct((D_MODEL, D_IN), BF16))
    return pl.pallas_call(
        functools.partial(_ffn_kernel, final_norm=final_norm, cast_next=w_in is not None),
        grid=(n_tiles, n_f),
        in_specs=in_specs,
        out_specs=out_specs,
        out_shape=out_shape,
        input_output_aliases={0: 0},
        scratch_shapes=[pltpu.VMEM((rows, D_MODEL), BF16)],
        compiler_params=pltpu.CompilerParams(
            dimension_semantics=("arbitrary", "arbitrary"),
            vmem_limit_bytes=_vmem_limit(
                [D_MODEL * 4, D_MODEL * tf * 4, tf * D_MODEL * 4, 2 * rows * D_MODEL * 4],
                rows * D_MODEL * 2,
                rows * tf * 6 + rows * FFN_COLS * 4 + 2 * D_MODEL * tf * 2),
        ),
        name="ffn",
    )(*args)


N_FRONT_PARAMS = 11
N_FRONT_SCRATCH = 10
PROJ_COLS = 512


def _pack_gate_weights(gwr_ref, gwi_ref, gw_ref):
    for h in range(H_A):
        gw_ref[h, :, 0:HD_A] = gwr_ref[h].astype(BF16)
        gw_ref[h, :, HD_A:] = gwi_ref[h].astype(BF16)


def _front_pieces(z_ref, outs, r_out, prm, scr, tl):
    caw_ref, cab_ref, _, _, br_ref, bi_ref, _, sg_ref, sb_ref, _, sbt_ref = prm
    xa_buf, a_s, u_s, _, g_s, _, vn_s, wm_s, sp_s, gw_ref = scr
    _, yc_ref, ub_ref = outs
    pieces = []

    def c_norm(r):
        v = jax.nn.gelu(z_ref[r:r + CHUNK, OFF_ZC + W_C:OFF_ZC + 2 * W_C])
        vn_s[r:r + CHUNK, :] = _layer_norm(v, sg_ref[...], sb_ref[...]).astype(BF16)

    def c_head(r, h):
        cs = slice(h * HD_C, (h + 1) * HD_C)
        mixed = jnp.dot(wm_s[h], vn_s[r:r + CHUNK, cs], preferred_element_type=F32)
        mixed = mixed + sbt_ref[:, h:h + 1]
        u_c = jax.nn.gelu(z_ref[r:r + CHUNK, OFF_ZC + h * HD_C:OFF_ZC + (h + 1) * HD_C])
        yc_ref[r_out + r:r_out + r + CHUNK, cs] = (u_c * mixed).astype(BF16)

    def glu(r):
        ub_ref[r_out + r:r_out + r + CHUNK, :] = (
            z_ref[r:r + CHUNK, OFF_XB:OFF_XB + W_B]
            * jax.nn.sigmoid(z_ref[r:r + CHUNK, OFF_GB:OFF_GB + W_B]))

    def a_head(h):
        cs = slice(h * HD_A, (h + 1) * HD_A)
        xa_buf[A_TAIL:A_TAIL + tl, cs] = z_ref[:, OFF_XA + h * HD_A:OFF_XA + (h + 1) * HD_A]
        xa_full = xa_buf[0:A_TAIL + tl, cs]
        conv_h = cab_ref[:, cs] + caw_ref[K_A - 1:K_A, cs] * xa_full[A_TAIL:, :]
        for k in range(K_A - 1):
            shifted = pltpu.roll(xa_full, K_A - 1 - k, axis=0)[A_TAIL:, :]
            conv_h = conv_h + caw_ref[k:k + 1, cs] * shifted
        pre = jnp.dot(conv_h.astype(BF16), gw_ref[h], preferred_element_type=F32)
        a, u = _lru_gates(conv_h, pre, br_ref[:, cs], bi_ref[:, cs], sp_s[:, cs])
        a_s[:, cs] = a
        u_s[:, cs] = u
        g_s[:, cs] = jax.nn.gelu(z_ref[:, OFF_GA + h * HD_A:OFF_GA + (h + 1) * HD_A])

    other = []
    for r in range(0, tl, CHUNK):
        other.append((1500, functools.partial(c_norm, r)))
        for h in range(H_C):
            other.append((500, functools.partial(c_head, r, h)))
        other.append((600, functools.partial(glu, r)))
    per_gate = -(-len(other) // H_A)
    for h in range(H_A):
        pieces.extend(other[h * per_gate:(h + 1) * per_gate])
        pieces.append((2000 * tl // 256, functools.partial(a_head, h)))
    return pieces


def _front_kernel(x_ref, x_next_ref, xs_ref, g_ref, w_ref, *rest, tl, chunks_per_seq):
    prm = rest[:N_FRONT_PARAMS]
    gwr_ref, gwi_ref, lam_ref, sw_ref = prm[2], prm[3], prm[6], prm[9]
    ya_ref, yc_ref, ub_ref, ca_out_ref, h_out_ref, zs_ref = rest[N_FRONT_PARAMS:N_FRONT_PARAMS + 6]
    z_even, z_odd, hn_s = rest[N_FRONT_PARAMS + 6:N_FRONT_PARAMS + 9]
    scr = rest[N_FRONT_PARAMS + 9:]
    xa_buf, a_s, u_s, h_s, g_s, carry, _, wm_s, sp_s, gw_s = scr
    outs = (ya_ref, yc_ref, ub_ref)
    s = pl.program_id(0)

    def normalise(x_rows_ref, r0):
        hn_s[...] = _rms(x_rows_ref[r0:r0 + tl, :], g_ref[...]).astype(BF16)

    def project_and_mix(x_rows_ref, x_r0, z_next, z_cur, r_out):
        normalise(x_rows_ref, x_r0)

        def project(c0):
            z_next[:, c0:c0 + PROJ_COLS] = jnp.dot(
                hn_s[...], w_ref[:, c0:c0 + PROJ_COLS], preferred_element_type=F32)

        _interleave(_front_pieces(z_cur, outs, r_out, prm, scr, tl),
                    [(functools.partial(project, c0), c0 / D_IN) for c0 in range(0, D_IN, PROJ_COLS)])
        xa_buf[0:A_TAIL, :] = xa_buf[tl:tl + A_TAIL, :]

        def scan_row(t, h):
            h = a_s[pl.ds(t, 1), :] * h + u_s[pl.ds(t, 1), :]
            h_s[pl.ds(t, 1), :] = h
            return h

        carry[...] = lax.fori_loop(0, tl, scan_row, carry[...], unroll=True)
        ya_ref[r_out:r_out + tl, :] = (h_s[...] * g_s[...]).astype(BF16)

    @pl.when(s == 0)
    def _():
        causal = (lax.broadcasted_iota(jnp.int32, (CHUNK, CHUNK), 0)
                  >= lax.broadcasted_iota(jnp.int32, (CHUNK, CHUNK), 1))
        for h in range(H_C):
            wm_s[h] = jnp.where(causal, sw_ref[h], 0.0).astype(BF16)
        sp_s[...] = _softplus(-lam_ref[...])
        _pack_gate_weights(gwr_ref, gwi_ref, gw_s)
        zs_ref[...] = jnp.dot(_rms(xs_ref[...], g_ref[...]).astype(BF16), w_ref[...],
                              preferred_element_type=F32)
        normalise(x_ref, 0)
        z_even[...] = jnp.dot(hn_s[...], w_ref[...], preferred_element_type=F32)

    @pl.when(s % (chunks_per_seq // 2) == 0)
    def _():
        xa_buf[0:A_TAIL, :] = jnp.zeros((A_TAIL, W_A), F32)
        carry[...] = jnp.zeros((1, W_A), F32)

    project_and_mix(x_ref, tl, z_odd, z_even, 0)
    project_and_mix(x_next_ref, 0, z_even, z_odd, tl)
    ca_out_ref[...] = xa_buf[0:A_TAIL, :]
    h_out_ref[...] = carry[...]


def _front(x, xs, g, w, p, layer, nb, seq, tl):
    ns = xs.shape[0]
    chunks_per_seq = seq // tl
    assert seq % tl == 0 and chunks_per_seq % 2 == 0 and tl % CHUNK == 0
    pairs = nb * chunks_per_seq // 2
    pairs_per_seq = chunks_per_seq // 2
    rows = nb * seq

    def per_layer(shape):
        return pl.BlockSpec((None,) + shape, lambda s: (layer,) + (0,) * len(shape))

    def pair(width):
        return pl.BlockSpec((2 * tl, width), lambda s: (s, 0))

    scratch = [
        pltpu.VMEM((tl, D_IN), F32), pltpu.VMEM((tl, D_IN), F32), pltpu.VMEM((tl, D_MODEL), BF16),
        pltpu.VMEM((A_TAIL + tl, W_A), F32),
        pltpu.VMEM((tl, W_A), F32), pltpu.VMEM((tl, W_A), F32), pltpu.VMEM((tl, W_A), F32),
        pltpu.VMEM((tl, W_A), F32), pltpu.VMEM((1, W_A), F32),
        pltpu.VMEM((tl, W_C), BF16), pltpu.VMEM((H_C, CHUNK, CHUNK), BF16), pltpu.VMEM((1, W_A), F32),
        pltpu.VMEM((H_A, HD_A, 2 * HD_A), BF16),
    ]
    assert len(scratch) == 3 + N_FRONT_SCRATCH
    scratch_bytes = (2 * tl * D_IN * 4 + tl * D_MODEL * 2
                     + (A_TAIL + 5 * tl) * W_A * 4 + tl * W_C * 2 + H_C * CHUNK * CHUNK * 2)
    return pl.pallas_call(
        functools.partial(_front_kernel, tl=tl, chunks_per_seq=chunks_per_seq),
        grid=(pairs,),
        in_specs=[
            pair(D_MODEL),
            pl.BlockSpec((tl, D_MODEL), lambda s: (jnp.minimum(2 * s + 2, 2 * pairs - 1), 0)),
            pl.BlockSpec((ns, D_MODEL), lambda s: (0, 0)),
            pl.BlockSpec((None, 1, D_MODEL), lambda s: (layer, 0, 0)),
            pl.BlockSpec((D_MODEL, D_IN), lambda s: (0, 0), pipeline_mode=pl.Buffered(1)),
            per_layer((K_A, W_A)), per_layer((1, W_A)),
            per_layer((H_A, HD_A, HD_A)), per_layer((H_A, HD_A, HD_A)),
            per_layer((1, W_A)), per_layer((1, W_A)), per_layer((1, W_A)),
            per_layer((1, W_C)), per_layer((1, W_C)), per_layer((H_C, CHUNK, CHUNK)),
            per_layer((CHUNK, H_C)),
        ],
        out_specs=[
            pair(W_A), pair(W_C), pair(W_B),
            pl.BlockSpec((None, A_TAIL, W_A), lambda s: (s // pairs_per_seq, 0, 0)),
            pl.BlockSpec((None, 1, W_A), lambda s: (s // pairs_per_seq, 0, 0)),
            pl.BlockSpec((ns, D_IN), lambda s: (0, 0)),
        ],
        out_shape=[
            jax.ShapeDtypeStruct((rows, W_A), BF16),
            jax.ShapeDtypeStruct((rows, W_C), BF16),
            jax.ShapeDtypeStruct((rows, W_B), F32),
            jax.ShapeDtypeStruct((nb, A_TAIL, W_A), F32),
            jax.ShapeDtypeStruct((nb, 1, W_A), F32),
            jax.ShapeDtypeStruct((ns, D_IN), F32),
        ],
        scratch_shapes=scratch,
        compiler_params=pltpu.CompilerParams(
            dimension_semantics=("arbitrary",),
            vmem_limit_bytes=_vmem_limit(
                [3 * tl * D_MODEL * 4, 2 * tl * (W_A + W_C) * 2, 2 * tl * W_B * 4, 1024 * 1024,
                 ns * D_MODEL * 4, ns * D_IN * 4],
                D_MODEL * D_IN * 2 + scratch_bytes, tl * D_IN * 4 + 4 * tl * W_A * 4),
        ),
        name="front",
    )(x, x, xs, g, w, p["conv_a_w"], p["conv_a_b"], p["gate_r_w"], p["gate_i_w"], p["gate_r_b"],
      p["gate_i_b"], p["lru_lambda"], p["sgu_ln_g"], p["sgu_ln_b"], p["sgu_w"], p["sgu_b_t"])


CONV_B_ROWS = 64
NORM_B_ROWS = 32
BACK_ROWS = 256
BACK_COLS = 2048


def _conv_b_tile(w_ref, ub_slab, conv_slab, r0, c, rows):
    first = B_TAIL - (K_B - 1)
    half = rows // 2
    cs = slice(c * LANES, (c + 1) * LANES)
    even = odd = None
    for kp in range(first, first + K_B + 1):
        x = ub_slab[c, pl.ds(r0 + kp, half, stride=2), :]
        if kp < first + K_B:
            term = w_ref[kp - first:kp - first + 1, cs] * x
            even = term if even is None else even + term
        if kp > first:
            term = w_ref[kp - first - 1:kp - first, cs] * x
            odd = term if odd is None else odd + term
    conv_slab[c, pl.ds(r0, half, stride=2), :] = even
    conv_slab[c, pl.ds(r0 + 1, half, stride=2), :] = odd


def _back_kernel(x_ref, ya_ref, yc_ref, ub_ref, ub_prev_ref, xs_ref, mixs_ref, w32_ref, cbw_ref, lbg_ref,
                 lbb_ref, o_ref, os_ref, ub_slab, conv_slab, yb_s, w_ref, *, tm, tiles_per_seq):
    i = pl.program_id(0)

    @pl.when(i == 0)
    def _():
        for c0 in range(0, D_MODEL, PROJ_COLS):
            w_ref[:, c0:c0 + PROJ_COLS] = w32_ref[:, c0:c0 + PROJ_COLS].astype(BF16)
        os_ref[...] = xs_ref[...] + jnp.dot(mixs_ref[...], w_ref[...], preferred_element_type=F32)

    n_slabs = W_B // LANES
    prev = jnp.where(i % tiles_per_seq == 0, 0.0, ub_prev_ref[...])
    for c in range(n_slabs):
        ub_slab[c, 0:B_TAIL, :] = prev[:, c * LANES:(c + 1) * LANES]
        ub_slab[c, B_TAIL:, :] = ub_ref[:, c * LANES:(c + 1) * LANES]

    def conv_rows(r):
        n = CONV_B_ROWS
        for c in range(n_slabs):
            _conv_b_tile(cbw_ref, ub_slab, conv_slab, r, c, n)
        for r1 in range(r, r + n, NORM_B_ROWS):
            conv = jnp.concatenate([conv_slab[c, r1:r1 + NORM_B_ROWS, :] for c in range(n_slabs)], axis=-1)
            y_b = _layer_norm(conv, lbg_ref[...], lbb_ref[...])
            yb_s[r1:r1 + NORM_B_ROWS, :] = jax.nn.silu(y_b).astype(BF16)

    def project(r, c0):
        rs, cs = slice(r, r + BACK_ROWS), slice(c0, c0 + BACK_COLS)
        o_ref[rs, cs] = (
            x_ref[rs, cs]
            + jnp.dot(ya_ref[rs, :], w_ref[OFF_YA:OFF_YA + W_A, cs], preferred_element_type=F32)
            + jnp.dot(yb_s[rs, :], w_ref[OFF_YB:OFF_YB + W_B, cs], preferred_element_type=F32)
            + jnp.dot(yc_ref[rs, :], w_ref[OFF_YC:OFF_YC + W_C, cs], preferred_element_type=F32))

    fillers = [(functools.partial(project, r, c0), (r + BACK_ROWS) / tm)
               for r in range(0, tm, BACK_ROWS) for c0 in range(0, D_MODEL, BACK_COLS)]
    _interleave([(CONV_B_ROWS, functools.partial(conv_rows, r)) for r in range(0, tm, CONV_B_ROWS)],
                fillers)


def _back(x, ya, yc, ub, xs, mix_s, w, p, layer, seq, tm):
    rows, ns = x.shape[0], xs.shape[0]
    assert seq % tm == 0 and tm % B_TAIL == 0 and tm % CONV_B_ROWS == 0 and tm % BACK_ROWS == 0
    tails_per_tile = tm // B_TAIL

    def per_layer(shape):
        return pl.BlockSpec((None,) + shape, lambda i: (layer,) + (0,) * len(shape))

    def tile(width):
        return pl.BlockSpec((tm, width), lambda i: (i, 0))

    def whole(width):
        return pl.BlockSpec((ns, width), lambda i: (0, 0))

    return pl.pallas_call(
        functools.partial(_back_kernel, tm=tm, tiles_per_seq=seq // tm),
        grid=(rows // tm,),
        in_specs=[
            tile(D_MODEL), tile(W_A), tile(W_C), tile(W_B),
            pl.BlockSpec((B_TAIL, W_B), lambda i: (jnp.maximum(i * tails_per_tile - 1, 0), 0)),
            whole(D_MODEL), whole(D_MODEL),
            pl.BlockSpec((None, D_MODEL, D_MODEL), lambda i: (layer, 0, 0), pipeline_mode=pl.Buffered(1)),
            per_layer((K_B, W_B)), per_layer((1, W_B)), per_layer((1, W_B)),
        ],
        out_specs=[tile(D_MODEL), whole(D_MODEL)],
        out_shape=[jax.ShapeDtypeStruct((rows, D_MODEL), F32), jax.ShapeDtypeStruct((ns, D_MODEL), F32)],
        scratch_shapes=[pltpu.VMEM((W_B // LANES, B_TAIL + tm, LANES), F32),
                        pltpu.VMEM((W_B // LANES, tm, LANES), F32),
                        pltpu.VMEM((tm, W_B), BF16), pltpu.VMEM((D_MODEL, D_MODEL), BF16)],
        compiler_params=pltpu.CompilerParams(
            dimension_semantics=("arbitrary",),
            vmem_limit_bytes=_vmem_limit(
                [tm * D_MODEL * 4, tm * (W_A + W_C) * 2, tm * W_B * 4, tm * D_MODEL * 4, 256 * 1024,
                 ns * D_MODEL * 10],
                D_MODEL * D_MODEL * (4 + 2) + (B_TAIL + 2 * tm) * W_B * 4 + tm * W_B * 2,
                2 * BACK_ROWS * BACK_COLS * 4),
        ),
        name="back",
    )(x, ya, yc, ub, ub, xs, mix_s, w, p["conv_b_w"], p["ln_b_g"], p["ln_b_b"])


def _mixer_sample_kernel(
        z_ref, ca_ref, h0_ref, cb_ref, caw_ref, cab_ref, gwr_ref, gwi_ref, br_ref, bi_ref, lam_ref,
        cbw_ref, lbg_ref, lbb_ref, sg_ref, sb_ref, sw0_ref, sb0_ref,
        mix_ref, ca_out_ref, h_out_ref, cb_out_ref, vn_out_ref):
    xa = z_ref[:, OFF_XA:OFF_XA + W_A]
    for k in range(K_A - 2):
        ca_out_ref[k] = ca_ref[k + 1]
    ca_out_ref[K_A - 2] = xa
    sp = _softplus(-lam_ref[...])
    for h in range(H_A):
        cs = slice(h * HD_A, (h + 1) * HD_A)
        conv_h = cab_ref[:, cs] + caw_ref[K_A - 1:K_A, cs] * xa[:, cs]
        for k in range(K_A - 1):
            conv_h = conv_h + caw_ref[k:k + 1, cs] * ca_ref[k, :, cs]
        conv_b = conv_h.astype(BF16)
        pre = jnp.concatenate(
            [jnp.dot(conv_b, gwr_ref[h].astype(BF16), preferred_element_type=F32),
             jnp.dot(conv_b, gwi_ref[h].astype(BF16), preferred_element_type=F32)], axis=-1)
        a, u = _lru_gates(conv_h, pre, br_ref[:, cs], bi_ref[:, cs], sp[:, cs])
        h_new = a * h0_ref[:, cs] + u
        h_out_ref[:, cs] = h_new
        mix_ref[:, OFF_YA + h * HD_A:OFF_YA + (h + 1) * HD_A] = (
            h_new * jax.nn.gelu(z_ref[:, OFF_GA + h * HD_A:OFF_GA + (h + 1) * HD_A])).astype(BF16)

    ub = z_ref[:, OFF_XB:OFF_XB + W_B] * jax.nn.sigmoid(z_ref[:, OFF_GB:OFF_GB + W_B])
    acc = cbw_ref[K_B - 1:K_B, :] * ub
    for k in range(K_B - 1):
        acc = acc + cbw_ref[k:k + 1, :] * cb_ref[k]
    for k in range(K_B - 2):
        cb_out_ref[k] = cb_ref[k + 1]
    cb_out_ref[K_B - 2] = ub
    y_b = _layer_norm(acc, lbg_ref[...], lbb_ref[...])
    mix_ref[:, OFF_YB:OFF_YB + W_B] = jax.nn.silu(y_b).astype(BF16)

    g_c = jax.nn.gelu(z_ref[:, OFF_ZC:OFF_ZC + 2 * W_C])
    v_n = _layer_norm(g_c[:, W_C:], sg_ref[...], sb_ref[...])
    vn_out_ref[...] = v_n
    mix_ref[:, OFF_YC:OFF_YC + W_C] = (g_c[:, :W_C] * (sw0_ref[...] * v_n + sb0_ref[...])).astype(BF16)


def _mixer_sample(z, ca_t, h0, cb_t, p, layer, tb):
    nb = z.shape[0]

    def per_layer(shape):
        return pl.BlockSpec((None,) + shape, lambda b: (layer,) + (0,) * len(shape))

    return pl.pallas_call(
        _mixer_sample_kernel,
        grid=(nb // tb,),
        in_specs=[
            pl.BlockSpec((tb, D_IN), lambda b: (b, 0)),
            pl.BlockSpec((None, K_A - 1, tb, W_A), lambda b: (layer, 0, b, 0)),
            pl.BlockSpec((None, tb, W_A), lambda b: (layer, b, 0)),
            pl.BlockSpec((None, K_B - 1, tb, W_B), lambda b: (layer, 0, b, 0)),
            per_layer((K_A, W_A)), per_layer((1, W_A)),
            per_layer((H_A, HD_A, HD_A)), per_layer((H_A, HD_A, HD_A)),
            per_layer((1, W_A)), per_layer((1, W_A)), per_layer((1, W_A)),
            per_layer((K_B, W_B)), per_layer((1, W_B)), per_layer((1, W_B)),
            per_layer((1, W_C)), per_layer((1, W_C)), per_layer((1, W_C)), per_layer((1, W_C)),
        ],
        out_specs=[
            pl.BlockSpec((tb, D_MODEL), lambda b: (b, 0)),
            pl.BlockSpec((K_A - 1, tb, W_A), lambda b: (0, b, 0)),
            pl.BlockSpec((tb, W_A), lambda b: (b, 0)),
            pl.BlockSpec((K_B - 1, tb, W_B), lambda b: (0, b, 0)),
            pl.BlockSpec((tb, W_C), lambda b: (b, 0)),
        ],
        out_shape=[
            jax.ShapeDtypeStruct((nb, D_MODEL), BF16),
            jax.ShapeDtypeStruct((K_A - 1, nb, W_A), F32),
            jax.ShapeDtypeStruct((nb, W_A), F32),
            jax.ShapeDtypeStruct((K_B - 1, nb, W_B), F32),
            jax.ShapeDtypeStruct((nb, W_C), F32),
        ],
        compiler_params=pltpu.CompilerParams(
            dimension_semantics=("arbitrary",),
            vmem_limit_bytes=_vmem_limit(
                [tb * D_IN * 4, 2 * (K_A - 1) * tb * W_A * 4, 2 * tb * W_A * 4,
                 2 * (K_B - 1) * tb * W_B * 4, tb * D_MODEL * 2, tb * W_C * 4, 1024 * 1024],
                0, 8 * tb * W_A * 4),
        ),
        name="mixer_sample",
    )(z, ca_t, h0, cb_t, p["conv_a_w"], p["conv_a_b"], p["gate_r_w"], p["gate_i_w"], p["gate_r_b"],
      p["gate_i_b"], p["lru_lambda"], p["conv_b_w"], p["ln_b_g"], p["ln_b_b"], p["sgu_ln_g"], p["sgu_ln_b"],
      p["sgu_w00"], p["sgu_b0"])


def kernel(x_prompt, x_sample, state_conv_a, state_lru_h, state_conv_b, norm_mix, w_in, conv_a_w,
           conv_a_b, gate_r_w, gate_r_b, gate_i_w, gate_i_b, lru_lambda, conv_b_w, ln_b_g, ln_b_b,
           sgu_ln_g, sgu_ln_b, sgu_w, sgu_b, w_out, norm_ffn, w_ff1, w_ff2, norm_final):
    depth = w_in.shape[0]
    nb, seq, _ = x_prompt.shape
    ns = x_sample.shape[0]

    def row(v):
        return v[:, None, :]

    p = {
        "conv_a_w": conv_a_w, "conv_a_b": row(conv_a_b),
        "gate_r_w": gate_r_w, "gate_i_w": gate_i_w,
        "gate_r_b": row(gate_r_b), "gate_i_b": row(gate_i_b), "lru_lambda": row(lru_lambda),
        "conv_b_w": conv_b_w, "ln_b_g": row(ln_b_g), "ln_b_b": row(ln_b_b),
        "sgu_ln_g": row(sgu_ln_g), "sgu_ln_b": row(sgu_ln_b),
        "sgu_w": sgu_w, "sgu_b_t": jnp.swapaxes(sgu_b, 1, 2),
        "sgu_w00": row(jnp.repeat(sgu_w[:, :, 0, 0], HD_C, axis=-1)),
        "sgu_b0": row(jnp.repeat(sgu_b[:, :, 0], HD_C, axis=-1)),
    }
    g_mix, g_ffn, g_final = row(norm_mix), row(norm_ffn), norm_final[None, :]
    w_in_l = w_in[0].astype(BF16)

    ca_t = jnp.swapaxes(state_conv_a, 1, 2)
    cb_t = jnp.swapaxes(state_conv_b, 1, 2)

    xp = x_prompt.reshape(nb * seq, D_MODEL)
    xs = x_sample.reshape(ns, D_MODEL)
    ca_p, h_p, cb_p, ca_s, h_s, cb_s, v_s = [], [], [], [], [], [], []
    for l in range(depth):
        ya, yc, ub, ca_l, h_l, zs = _front(xp, xs, g_mix, w_in_l, p, l, nb, seq, tl=256)
        ca_p.append(ca_l[:, A_TAIL - (K_A - 1):, :])
        h_p.append(h_l[:, 0, :])
        cb_p.append(ub.reshape(nb, seq, W_B)[:, seq - (K_B - 1):, :])
        mix_s, cas_l, hs_l, cbs_l, v_l = _mixer_sample(zs, ca_t, state_lru_h, cb_t, p, l, tb=32)
        ca_s.append(jnp.swapaxes(cas_l, 0, 1))
        h_s.append(hs_l)
        cb_s.append(jnp.swapaxes(cbs_l, 0, 1))
        v_s.append(v_l[:, None, :])
        xp, xs = _back(xp, ya, yc, ub, xs, mix_s, w_out, p, l, seq, tm=512)
        if l + 1 < depth:
            xp, xs, w_in_l = _ffn(xp, xs, g_ffn, w_ff1, w_ff2, l, tm=1024, tf=512, w_in=w_in)
        else:
            xp, xs = _ffn(xp, xs, g_ffn, w_ff1, w_ff2, l, tm=1024, tf=512, g_final=g_final)

    return (xp.reshape(nb, seq, D_MODEL), xs.reshape(ns, 1, D_MODEL),
            jnp.stack(ca_p), jnp.stack(h_p), jnp.stack(cb_p),
            jnp.stack(ca_s), jnp.stack(h_s), jnp.stack(cb_s), jnp.stack(v_s))
```

```python
import functools

import jax
import jax.numpy as jnp
from jax import lax
from jax.experimental import pallas as pl
from jax.experimental.pallas import tpu as pltpu

F32 = jnp.float32
BF16 = jnp.bfloat16

D_MODEL = 2048
W_A = 1024
H_A = 8
HD_A = W_A // H_A
K_A = 4
LRU_C = 8.0
W_B = 512
K_B = 31
W_C = 512
H_C = 4
HD_C = W_C // H_C
CHUNK = 128
D_IN = 2 * (W_A + W_B + W_C)
D_FF = 4 * D_MODEL
EPS = 1e-6

OFF_XA, OFF_GA, OFF_XB, OFF_GB, OFF_ZC = 0, W_A, 2 * W_A, 2 * W_A + W_B, 2 * W_A + 2 * W_B
OFF_YA, OFF_YB, OFF_YC = 0, W_A, W_A + W_B

SUBLANES = 8
LANES = 128
A_TAIL = SUBLANES
B_TAIL = 32
V7X_VMEM_BYTES = 64 * 1024 * 1024


def _vmem_limit(block_bytes, scratch_bytes, temp_bytes):
    need = 2 * sum(block_bytes) + scratch_bytes + temp_bytes
    return int(min(need + need // 8, V7X_VMEM_BYTES - 4 * 1024 * 1024))


def _rms(x, g):
    return (x * lax.rsqrt(jnp.mean(x * x, axis=-1, keepdims=True) + EPS)) * g


def _layer_norm(x, g, b):
    xc = x - jnp.mean(x, axis=-1, keepdims=True)
    y = xc * lax.rsqrt(jnp.mean(xc * xc, axis=-1, keepdims=True) + EPS)
    return y * g + b


def _softplus(x):
    return jnp.maximum(x, 0.0) + jnp.log1p(jnp.exp(-jnp.abs(x)))


def _lru_gates(conv_h, pre, b_r, b_i, sp):
    r = jax.nn.sigmoid(pre[:, :HD_A] + b_r)
    i = jax.nn.sigmoid(pre[:, HD_A:] + b_i)
    log_a = (-LRU_C * r) * sp
    a = jnp.exp(log_a)
    t = jnp.tanh(log_a)
    u = jnp.sqrt(-2.0 * t / (1.0 - t)) * (i * conv_h)
    return a, u


def _interleave(pieces, fillers):
    total = sum(cost for cost, _ in pieces)
    done, issued = 0, 0
    for cost, piece in pieces:
        while issued < len(fillers) and fillers[issued][1] * total <= done:
            fillers[issued][0]()
            issued += 1
        piece()
        done += cost
    for filler, _ in fillers[issued:]:
        filler()


FFN_COLS = 512


def _ffn_kernel(*refs, final_norm, cast_next):
    refs = list(refs)
    hf_ref = refs.pop()
    if cast_next:
        wq_ref = refs.pop()
    os_ref = refs.pop()
    op_ref = refs.pop()
    if cast_next:
        wq_ref[...] = refs.pop()[...].astype(BF16)
    gf_ref = refs.pop() if final_norm else None
    xp_ref, xs_ref, g_ref, w1_ref, w2_ref = refs
    j = pl.program_id(1)
    tm = xp_ref.shape[0]
    outs = [(op_ref, slice(0, tm)), (os_ref, slice(tm, None))]

    @pl.when(j == 0)
    def _():
        for x_ref, (o_ref, rs) in zip((xp_ref, xs_ref), outs):
            x = x_ref[...]
            hf_ref[rs, :] = _rms(x, g_ref[...]).astype(BF16)
            o_ref[...] = x

    h = jnp.dot(hf_ref[...], w1_ref[...].astype(BF16), preferred_element_type=F32)
    h = jnp.square(jnp.maximum(h, 0.0)).astype(BF16)
    for c0 in range(0, D_MODEL, FFN_COLS):
        cs = slice(c0, c0 + FFN_COLS)
        acc = jnp.dot(h, w2_ref[:, cs].astype(BF16), preferred_element_type=F32)
        for o_ref, rs in outs:
            o_ref[:, cs] += acc[rs, :]

    if final_norm:
        @pl.when(j == pl.num_programs(1) - 1)
        def _():
            for o_ref, _ in outs:
                o_ref[...] = _rms(o_ref[...], gf_ref[...])


def _ffn(xp, xs, g, w1, w2, layer, tm, tf, g_final=None, w_in=None):
    m, ns = xp.shape[0], xs.shape[0]
    n_tiles = m // tm
    ts = ns // n_tiles
    assert m % tm == 0 and ns % n_tiles == 0 and ts % (2 * SUBLANES) == 0
    final_norm = g_final is not None
    prompt_tile = pl.BlockSpec((tm, D_MODEL), lambda i, j: (i, 0))
    sample_tile = pl.BlockSpec((ts, D_MODEL), lambda i, j: (i, 0))
    in_specs = [
        prompt_tile, sample_tile,
        pl.BlockSpec((None, 1, D_MODEL), lambda i, j: (layer, 0, 0)),
        pl.BlockSpec((None, D_MODEL, tf), lambda i, j: (layer, 0, j)),
        pl.BlockSpec((None, tf, D_MODEL), lambda i, j: (layer, j, 0)),
    ]
    args = [xp, xs, g, w1, w2]
    if final_norm:
        in_specs.append(pl.BlockSpec((1, D_MODEL), lambda i, j: (0, 0)))
        args.append(g_final)
    rows = tm + ts
    n_f = D_FF // tf
    out_specs = [prompt_tile, sample_tile]
    out_shape = [jax.ShapeDtypeStruct((m, D_MODEL), F32), jax.ShapeDtypeStruct((ns, D_MODEL), F32)]
    if w_in is not None:
        w_rows = D_MODEL // (n_tiles * n_f)
        assert D_MODEL % (n_tiles * n_f) == 0 and w_rows % (2 * SUBLANES) == 0
        in_specs.append(pl.BlockSpec((None, w_rows, D_IN), lambda i, j: (layer + 1, i * n_f + j, 0)))
        args.append(w_in)
        out_specs.append(pl.BlockSpec((w_rows, D_IN), lambda i, j: (i * n_f + j, 0)))
        out_shape.append(jax.ShapeDtypeStruct((D_MODEL, D_IN), BF16))
    return pl.pallas_call(
        functools.partial(_ffn_kernel, final_norm=final_norm, cast_next=w_in is not None),
        grid=(n_tiles, n_f),
        in_specs=in_specs,
        out_specs=out_specs,
        out_shape=out_shape,
        input_output_aliases={0: 0},
        scratch_shapes=[pltpu.VMEM((rows, D_MODEL), BF16)],
        compiler_params=pltpu.CompilerParams(
            dimension_semantics=("arbitrary", "arbitrary"),
            vmem_limit_bytes=_vmem_limit(
                [D_MODEL * 4, D_MODEL * tf * 4, tf * D_MODEL * 4, 2 * rows * D_MODEL * 4],
                rows * D_MODEL * 2,
                rows * tf * 6 + rows * FFN_COLS * 4 + 2 * D_MODEL * tf * 2),
        ),
        name="ffn",
    )(*args)


N_FRONT_PARAMS = 11
N_FRONT_SCRATCH = 10
PROJ_COLS = 512


def _pack_gate_weights(gwr_ref, gwi_ref, gw_ref):
    for h in range(H_A):
        gw_ref[h, :, 0:HD_A] = gwr_ref[h].astype(BF16)
        gw_ref[h, :, HD_A:] = gwi_ref[h].astype(BF16)


def _front_pieces(z_ref, outs, r_out, prm, scr, tl):
    caw_ref, cab_ref, _, _, br_ref, bi_ref, _, sg_ref, sb_ref, _, sbt_ref = prm
    xa_buf, a_s, u_s, _, g_s, _, vn_s, wm_s, sp_s, gw_ref = scr
    _, yc_ref, ub_ref = outs
    pieces = []

    def c_norm(r):
        v = jax.nn.gelu(z_ref[r:r + CHUNK, OFF_ZC + W_C:OFF_ZC + 2 * W_C])
        vn_s[r:r + CHUNK, :] = _layer_norm(v, sg_ref[...], sb_ref[...]).astype(BF16)

    def c_head(r, h):
        cs = slice(h * HD_C, (h + 1) * HD_C)
        mixed = jnp.dot(wm_s[h], vn_s[r:r + CHUNK, cs], preferred_element_type=F32)
        mixed = mixed + sbt_ref[:, h:h + 1]
        u_c = jax.nn.gelu(z_ref[r:r + CHUNK, OFF_ZC + h * HD_C:OFF_ZC + (h + 1) * HD_C])
        yc_ref[r_out + r:r_out + r + CHUNK, cs] = (u_c * mixed).astype(BF16)

    def glu(r):
        ub_ref[r_out + r:r_out + r + CHUNK, :] = (
            z_ref[r:r + CHUNK, OFF_XB:OFF_XB + W_B]
            * jax.nn.sigmoid(z_ref[r:r + CHUNK, OFF_GB:OFF_GB + W_B]))

    def a_head(h):
        cs = slice(h * HD_A, (h + 1) * HD_A)
        xa_buf[A_TAIL:A_TAIL + tl, cs] = z_ref[:, OFF_XA + h * HD_A:OFF_XA + (h + 1) * HD_A]
        xa_full = xa_buf[0:A_TAIL + tl, cs]
        conv_h = cab_ref[:, cs] + caw_ref[K_A - 1:K_A, cs] * xa_full[A_TAIL:, :]
        for k in range(K_A - 1):
            shifted = pltpu.roll(xa_full, K_A - 1 - k, axis=0)[A_TAIL:, :]
            conv_h = conv_h + caw_ref[k:k + 1, cs] * shifted
        pre = jnp.dot(conv_h.astype(BF16), gw_ref[h], preferred_element_type=F32)
        a, u = _lru_gates(conv_h, pre, br_ref[:, cs], bi_ref[:, cs], sp_s[:, cs])
        a_s[:, cs] = a
        u_s[:, cs] = u
        g_s[:, cs] = jax.nn.gelu(z_ref[:, OFF_GA + h * HD_A:OFF_GA + (h + 1) * HD_A])

    other = []
    for r in range(0, tl, CHUNK):
        other.append((1500, functools.partial(c_norm, r)))
        for h in range(H_C):
            other.append((500, functools.partial(c_head, r, h)))
        other.append((600, functools.partial(glu, r)))
    per_gate = -(-len(other) // H_A)
    for h in range(H_A):
        pieces.extend(other[h * per_gate:(h + 1) * per_gate])
        pieces.append((2000 * tl // 256, functools.partial(a_head, h)))
    return pieces


def _front_kernel(x_ref, x_next_ref, xs_ref, g_ref, w_ref, *rest, tl, chunks_per_seq):
    prm = rest[:N_FRONT_PARAMS]
    gwr_ref, gwi_ref, lam_ref, sw_ref = prm[2], prm[3], prm[6], prm[9]
    ya_ref, yc_ref, ub_ref, ca_out_ref, h_out_ref, zs_ref = rest[N_FRONT_PARAMS:N_FRONT_PARAMS + 6]
    z_even, z_odd, hn_s = rest[N_FRONT_PARAMS + 6:N_FRONT_PARAMS + 9]
    scr = rest[N_FRONT_PARAMS + 9:]
    xa_buf, a_s, u_s, h_s, g_s, carry, _, wm_s, sp_s, gw_s = scr
    outs = (ya_ref, yc_ref, ub_ref)
    s = pl.program_id(0)

    def normalise(x_rows_ref, r0):
        hn_s[...] = _rms(x_rows_ref[r0:r0 + tl, :], g_ref[...]).astype(BF16)

    def project_and_mix(x_rows_ref, x_r0, z_next, z_cur, r_out):
        normalise(x_rows_ref, x_r0)

        def project(c0):
            z_next[:, c0:c0 + PROJ_COLS] = jnp.dot(
                hn_s[...], w_ref[:, c0:c0 + PROJ_COLS], preferred_element_type=F32)

        _interleave(_front_pieces(z_cur, outs, r_out, prm, scr, tl),
                    [(functools.partial(project, c0), c0 / D_IN) for c0 in range(0, D_IN, PROJ_COLS)])
        xa_buf[0:A_TAIL, :] = xa_buf[tl:tl + A_TAIL, :]

        def scan_row(t, h):
            h = a_s[pl.ds(t, 1), :] * h + u_s[pl.ds(t, 1), :]
            h_s[pl.ds(t, 1), :] = h
            return h

        carry[...] = lax.fori_loop(0, tl, scan_row, carry[...], unroll=True)
        ya_ref[r_out:r_out + tl, :] = (h_s[...] * g_s[...]).astype(BF16)

    @pl.when(s == 0)
    def _():
        causal = (lax.broadcasted_iota(jnp.int32, (CHUNK, CHUNK), 0)
                  >= lax.broadcasted_iota(jnp.int32, (CHUNK, CHUNK), 1))
        for h in range(H_C):
            wm_s[h] = jnp.where(causal, sw_ref[h], 0.0).astype(BF16)
        sp_s[...] = _softplus(-lam_ref[...])
        _pack_gate_weights(gwr_ref, gwi_ref, gw_s)
        normalise(x_ref, 0)
        lhs = jnp.concatenate([hn_s[...], _rms(xs_ref[...], g_ref[...]).astype(BF16)], axis=0)
        for c0 in range(0, D_IN, PROJ_COLS):
            z = jnp.dot(lhs, w_ref[:, c0:c0 + PROJ_COLS], preferred_element_type=F32)
            z_even[:, c0:c0 + PROJ_COLS] = z[0:tl, :]
            zs_ref[:, c0:c0 + PROJ_COLS] = z[tl:, :]

    @pl.when(s % (chunks_per_seq // 2) == 0)
    def _():
        xa_buf[0:A_TAIL, :] = jnp.zeros((A_TAIL, W_A), F32)
        carry[...] = jnp.zeros((1, W_A), F32)

    project_and_mix(x_ref, tl, z_odd, z_even, 0)
    project_and_mix(x_next_ref, 0, z_even, z_odd, tl)
    ca_out_ref[...] = xa_buf[0:A_TAIL, :]
    h_out_ref[...] = carry[...]


def _front(x, xs, g, w, p, layer, nb, seq, tl):
    ns = xs.shape[0]
    chunks_per_seq = seq // tl
    assert seq % tl == 0 and chunks_per_seq % 2 == 0 and tl % CHUNK == 0
    pairs = nb * chunks_per_seq // 2
    pairs_per_seq = chunks_per_seq // 2
    rows = nb * seq

    def per_layer(shape):
        return pl.BlockSpec((None,) + shape, lambda s: (layer,) + (0,) * len(shape))

    def pair(width):
        return pl.BlockSpec((2 * tl, width), lambda s: (s, 0))

    scratch = [
        pltpu.VMEM((tl, D_IN), F32), pltpu.VMEM((tl, D_IN), F32), pltpu.VMEM((tl, D_MODEL), BF16),
        pltpu.VMEM((A_TAIL + tl, W_A), F32),
        pltpu.VMEM((tl, W_A), F32), pltpu.VMEM((tl, W_A), F32), pltpu.VMEM((tl, W_A), F32),
        pltpu.VMEM((tl, W_A), F32), pltpu.VMEM((1, W_A), F32),
        pltpu.VMEM((tl, W_C), BF16), pltpu.VMEM((H_C, CHUNK, CHUNK), BF16), pltpu.VMEM((1, W_A), F32),
        pltpu.VMEM((H_A, HD_A, 2 * HD_A), BF16),
    ]
    assert len(scratch) == 3 + N_FRONT_SCRATCH
    scratch_bytes = (2 * tl * D_IN * 4 + tl * D_MODEL * 2
                     + (A_TAIL + 5 * tl) * W_A * 4 + tl * W_C * 2 + H_C * CHUNK * CHUNK * 2)
    return pl.pallas_call(
        functools.partial(_front_kernel, tl=tl, chunks_per_seq=chunks_per_seq),
        grid=(pairs,),
        in_specs=[
            pair(D_MODEL),
            pl.BlockSpec((tl, D_MODEL), lambda s: (jnp.minimum(2 * s + 2, 2 * pairs - 1), 0)),
            pl.BlockSpec((ns, D_MODEL), lambda s: (0, 0)),
            pl.BlockSpec((None, 1, D_MODEL), lambda s: (layer, 0, 0)),
            pl.BlockSpec((D_MODEL, D_IN), lambda s: (0, 0), pipeline_mode=pl.Buffered(1)),
            per_layer((K_A, W_A)), per_layer((1, W_A)),
            per_layer((H_A, HD_A, HD_A)), per_layer((H_A, HD_A, HD_A)),
            per_layer((1, W_A)), per_layer((1, W_A)), per_layer((1, W_A)),
            per_layer((1, W_C)), per_layer((1, W_C)), per_layer((H_C, CHUNK, CHUNK)),
            per_layer((CHUNK, H_C)),
        ],
        out_specs=[
            pair(W_A), pair(W_C), pair(W_B),
            pl.BlockSpec((None, A_TAIL, W_A), lambda s: (s // pairs_per_seq, 0, 0)),
            pl.BlockSpec((None, 1, W_A), lambda s: (s // pairs_per_seq, 0, 0)),
            pl.BlockSpec((ns, D_IN), lambda s: (0, 0)),
        ],
        out_shape=[
            jax.ShapeDtypeStruct((rows, W_A), BF16),
            jax.ShapeDtypeStruct((rows, W_C), BF16),
            jax.ShapeDtypeStruct((rows, W_B), F32),
            jax.ShapeDtypeStruct((nb, A_TAIL, W_A), F32),
            jax.ShapeDtypeStruct((nb, 1, W_A), F32),
            jax.ShapeDtypeStruct((ns, D_IN), F32),
        ],
        scratch_shapes=scratch,
        compiler_params=pltpu.CompilerParams(
            dimension_semantics=("arbitrary",),
            vmem_limit_bytes=_vmem_limit(
                [3 * tl * D_MODEL * 4, 2 * tl * (W_A + W_C) * 2, 2 * tl * W_B * 4, 1024 * 1024,
                 ns * D_MODEL * 4, ns * D_IN * 4],
                D_MODEL * D_IN * 2 + scratch_bytes, tl * D_IN * 4 + 4 * tl * W_A * 4),
        ),
        name="front",
    )(x, x, xs, g, w, p["conv_a_w"], p["conv_a_b"], p["gate_r_w"], p["gate_i_w"], p["gate_r_b"],
      p["gate_i_b"], p["lru_lambda"], p["sgu_ln_g"], p["sgu_ln_b"], p["sgu_w"], p["sgu_b_t"])


CONV_B_ROWS = 64
NORM_B_ROWS = 32
BACK_ROWS = 256
BACK_COLS = 2048


def _conv_b_tile(w_ref, ub_slab, conv_slab, r0, c, rows):
    first = B_TAIL - (K_B - 1)
    half = rows // 2
    cs = slice(c * LANES, (c + 1) * LANES)
    even = odd = None
    for kp in range(first, first + K_B + 1):
        x = ub_slab[c, pl.ds(r0 + kp, half, stride=2), :]
        if kp < first + K_B:
            term = w_ref[kp - first:kp - first + 1, cs] * x
            even = term if even is None else even + term
        if kp > first:
            term = w_ref[kp - first - 1:kp - first, cs] * x
            odd = term if odd is None else odd + term
    conv_slab[c, pl.ds(r0, half, stride=2), :] = even
    conv_slab[c, pl.ds(r0 + 1, half, stride=2), :] = odd


def _back_kernel(x_ref, ya_ref, yc_ref, ub_ref, ub_prev_ref, xs_ref, mixs_ref, w32_ref, cbw_ref, lbg_ref,
                 lbb_ref, o_ref, os_ref, ub_slab, conv_slab, yb_s, w_ref, *, tm, tiles_per_seq):
    i = pl.program_id(0)

    @pl.when(i == 0)
    def _():
        for c0 in range(0, D_MODEL, PROJ_COLS):
            w_ref[:, c0:c0 + PROJ_COLS] = w32_ref[:, c0:c0 + PROJ_COLS].astype(BF16)
        os_ref[...] = xs_ref[...] + jnp.dot(mixs_ref[...], w_ref[...], preferred_element_type=F32)

    n_slabs = W_B // LANES
    prev = jnp.where(i % tiles_per_seq == 0, 0.0, ub_prev_ref[...])
    for c in range(n_slabs):
        ub_slab[c, 0:B_TAIL, :] = prev[:, c * LANES:(c + 1) * LANES]
        ub_slab[c, B_TAIL:, :] = ub_ref[:, c * LANES:(c + 1) * LANES]

    def conv_rows(r):
        n = CONV_B_ROWS
        for c in range(n_slabs):
            _conv_b_tile(cbw_ref, ub_slab, conv_slab, r, c, n)
        for r1 in range(r, r + n, NORM_B_ROWS):
            conv = jnp.concatenate([conv_slab[c, r1:r1 + NORM_B_ROWS, :] for c in range(n_slabs)], axis=-1)
            y_b = _layer_norm(conv, lbg_ref[...], lbb_ref[...])
            yb_s[r1:r1 + NORM_B_ROWS, :] = jax.nn.silu(y_b).astype(BF16)

    def project(r, c0):
        rs, cs = slice(r, r + BACK_ROWS), slice(c0, c0 + BACK_COLS)
        o_ref[rs, cs] = (
            x_ref[rs, cs]
            + jnp.dot(ya_ref[rs, :], w_ref[OFF_YA:OFF_YA + W_A, cs], preferred_element_type=F32)
            + jnp.dot(yb_s[rs, :], w_ref[OFF_YB:OFF_YB + W_B, cs], preferred_element_type=F32)
            + jnp.dot(yc_ref[rs, :], w_ref[OFF_YC:OFF_YC + W_C, cs], preferred_element_type=F32))

    fillers = [(functools.partial(project, r, c0), (r + BACK_ROWS) / tm)
               for r in range(0, tm, BACK_ROWS) for c0 in range(0, D_MODEL, BACK_COLS)]
    _interleave([(CONV_B_ROWS, functools.partial(conv_rows, r)) for r in range(0, tm, CONV_B_ROWS)],
                fillers)


def _back(x, ya, yc, ub, xs, mix_s, w, p, layer, seq, tm):
    rows, ns = x.shape[0], xs.shape[0]
    assert seq % tm == 0 and tm % B_TAIL == 0 and tm % CONV_B_ROWS == 0 and tm % BACK_ROWS == 0
    tails_per_tile = tm // B_TAIL

    def per_layer(shape):
        return pl.BlockSpec((None,) + shape, lambda i: (layer,) + (0,) * len(shape))

    def tile(width):
        return pl.BlockSpec((tm, width), lambda i: (i, 0))

    def whole(width):
        return pl.BlockSpec((ns, width), lambda i: (0, 0))

    return pl.pallas_call(
        functools.partial(_back_kernel, tm=tm, tiles_per_seq=seq // tm),
        grid=(rows // tm,),
        in_specs=[
            tile(D_MODEL), tile(W_A), tile(W_C), tile(W_B),
            pl.BlockSpec((B_TAIL, W_B), lambda i: (jnp.maximum(i * tails_per_tile - 1, 0), 0)),
            whole(D_MODEL), whole(D_MODEL),
            pl.BlockSpec((None, D_MODEL, D_MODEL), lambda i: (layer, 0, 0), pipeline_mode=pl.Buffered(1)),
            per_layer((K_B, W_B)), per_layer((1, W_B)), per_layer((1, W_B)),
        ],
        out_specs=[tile(D_MODEL), whole(D_MODEL)],
        out_shape=[jax.ShapeDtypeStruct((rows, D_MODEL), F32), jax.ShapeDtypeStruct((ns, D_MODEL), F32)],
        scratch_shapes=[pltpu.VMEM((W_B // LANES, B_TAIL + tm, LANES), F32),
                        pltpu.VMEM((W_B // LANES, tm, LANES), F32),
                        pltpu.VMEM((tm, W_B), BF16), pltpu.VMEM((D_MODEL, D_MODEL), BF16)],
        compiler_params=pltpu.CompilerParams(
            dimension_semantics=("arbitrary",),
            vmem_limit_bytes=_vmem_limit(
                [tm * D_MODEL * 4, tm * (W_A + W_C) * 2, tm * W_B * 4, tm * D_MODEL * 4, 256 * 1024,
                 ns * D_MODEL * 10],
                D_MODEL * D_MODEL * (4 + 2) + (B_TAIL + 2 * tm) * W_B * 4 + tm * W_B * 2,
                2 * BACK_ROWS * BACK_COLS * 4),
        ),
        name="back",
    )(x, ya, yc, ub, ub, xs, mix_s, w, p["conv_b_w"], p["ln_b_g"], p["ln_b_b"])


def _mixer_sample_kernel(
        z_ref, ca_ref, h0_ref, cb_ref, caw_ref, cab_ref, gwr_ref, gwi_ref, br_ref, bi_ref, lam_ref,
        cbw_ref, lbg_ref, lbb_ref, sg_ref, sb_ref, sw0_ref, sb0_ref,
        mix_ref, ca_out_ref, h_out_ref, cb_out_ref, vn_out_ref):
    xa = z_ref[:, OFF_XA:OFF_XA + W_A]
    for k in range(K_A - 2):
        ca_out_ref[k] = ca_ref[k + 1]
    ca_out_ref[K_A - 2] = xa
    sp = _softplus(-lam_ref[...])
    for h in range(H_A):
        cs = slice(h * HD_A, (h + 1) * HD_A)
        conv_h = cab_ref[:, cs] + caw_ref[K_A - 1:K_A, cs] * xa[:, cs]
        for k in range(K_A - 1):
            conv_h = conv_h + caw_ref[k:k + 1, cs] * ca_ref[k, :, cs]
        conv_b = conv_h.astype(BF16)
        pre = jnp.concatenate(
            [jnp.dot(conv_b, gwr_ref[h].astype(BF16), preferred_element_type=F32),
             jnp.dot(conv_b, gwi_ref[h].astype(BF16), preferred_element_type=F32)], axis=-1)
        a, u = _lru_gates(conv_h, pre, br_ref[:, cs], bi_ref[:, cs], sp[:, cs])
        h_new = a * h0_ref[:, cs] + u
        h_out_ref[:, cs] = h_new
        mix_ref[:, OFF_YA + h * HD_A:OFF_YA + (h + 1) * HD_A] = (
            h_new * jax.nn.gelu(z_ref[:, OFF_GA + h * HD_A:OFF_GA + (h + 1) * HD_A])).astype(BF16)

    ub = z_ref[:, OFF_XB:OFF_XB + W_B] * jax.nn.sigmoid(z_ref[:, OFF_GB:OFF_GB + W_B])
    acc = cbw_ref[K_B - 1:K_B, :] * ub
    for k in range(K_B - 1):
        acc = acc + cbw_ref[k:k + 1, :] * cb_ref[k]
    for k in range(K_B - 2):
        cb_out_ref[k] = cb_ref[k + 1]
    cb_out_ref[K_B - 2] = ub
    y_b = _layer_norm(acc, lbg_ref[...], lbb_ref[...])
    mix_ref[:, OFF_YB:OFF_YB + W_B] = jax.nn.silu(y_b).astype(BF16)

    g_c = jax.nn.gelu(z_ref[:, OFF_ZC:OFF_ZC + 2 * W_C])
    v_n = _layer_norm(g_c[:, W_C:], sg_ref[...], sb_ref[...])
    vn_out_ref[...] = v_n
    mix_ref[:, OFF_YC:OFF_YC + W_C] = (g_c[:, :W_C] * (sw0_ref[...] * v_n + sb0_ref[...])).astype(BF16)


def _mixer_sample(z, ca_t, h0, cb_t, p, layer, tb):
    nb = z.shape[0]

    def per_layer(shape):
        return pl.BlockSpec((None,) + shape, lambda b: (layer,) + (0,) * len(shape))

    return pl.pallas_call(
        _mixer_sample_kernel,
        grid=(nb // tb,),
        in_specs=[
            pl.BlockSpec((tb, D_IN), lambda b: (b, 0)),
            pl.BlockSpec((None, K_A - 1, tb, W_A), lambda b: (layer, 0, b, 0)),
            pl.BlockSpec((None, tb, W_A), lambda b: (layer, b, 0)),
            pl.BlockSpec((None, K_B - 1, tb, W_B), lambda b: (layer, 0, b, 0)),
            per_layer((K_A, W_A)), per_layer((1, W_A)),
            per_layer((H_A, HD_A, HD_A)), per_layer((H_A, HD_A, HD_A)),
            per_layer((1, W_A)), per_layer((1, W_A)), per_layer((1, W_A)),
            per_layer((K_B, W_B)), per_layer((1, W_B)), per_layer((1, W_B)),
            per_layer((1, W_C)), per_layer((1, W_C)), per_layer((1, W_C)), per_layer((1, W_C)),
        ],
        out_specs=[
            pl.BlockSpec((tb, D_MODEL), lambda b: (b, 0)),
            pl.BlockSpec((K_A - 1, tb, W_A), lambda b: (0, b, 0)),
            pl.BlockSpec((tb, W_A), lambda b: (b, 0)),
            pl.BlockSpec((K_B - 1, tb, W_B), lambda b: (0, b, 0)),
            pl.BlockSpec((tb, W_C), lambda b: (b, 0)),
        ],
        out_shape=[
            jax.ShapeDtypeStruct((nb, D_MODEL), BF16),
            jax.ShapeDtypeStruct((K_A - 1, nb, W_A), F32),
            jax.ShapeDtypeStruct((nb, W_A), F32),
            jax.ShapeDtypeStruct((K_B - 1, nb, W_B), F32),
            jax.ShapeDtypeStruct((nb, W_C), F32),
        ],
        compiler_params=pltpu.CompilerParams(
            dimension_semantics=("arbitrary",),
            vmem_limit_bytes=_vmem_limit(
                [tb * D_IN * 4, 2 * (K_A - 1) * tb * W_A * 4, 2 * tb * W_A * 4,
                 2 * (K_B - 1) * tb * W_B * 4, tb * D_MODEL * 2, tb * W_C * 4, 1024 * 1024],
                0, 8 * tb * W_A * 4),
        ),
        name="mixer_sample",
    )(z, ca_t, h0, cb_t, p["conv_a_w"], p["conv_a_b"], p["gate_r_w"], p["gate_i_w"], p["gate_r_b"],
      p["gate_i_b"], p["lru_lambda"], p["conv_b_w"], p["ln_b_g"], p["ln_b_b"], p["sgu_ln_g"], p["sgu_ln_b"],
      p["sgu_w00"], p["sgu_b0"])


def kernel(x_prompt, x_sample, state_conv_a, state_lru_h, state_conv_b, norm_mix, w_in, conv_a_w,
           conv_a_b, gate_r_w, gate_r_b, gate_i_w, gate_i_b, lru_lambda, conv_b_w, ln_b_g, ln_b_b,
           sgu_ln_g, sgu_ln_b, sgu_w, sgu_b, w_out, norm_ffn, w_ff1, w_ff2, norm_final):
    depth = w_in.shape[0]
    nb, seq, _ = x_prompt.shape
    ns = x_sample.shape[0]

    def row(v):
        return v[:, None, :]

    p = {
        "conv_a_w": conv_a_w, "conv_a_b": row(conv_a_b),
        "gate_r_w": gate_r_w, "gate_i_w": gate_i_w,
        "gate_r_b": row(gate_r_b), "gate_i_b": row(gate_i_b), "lru_lambda": row(lru_lambda),
        "conv_b_w": conv_b_w, "ln_b_g": row(ln_b_g), "ln_b_b": row(ln_b_b),
        "sgu_ln_g": row(sgu_ln_g), "sgu_ln_b": row(sgu_ln_b),
        "sgu_w": sgu_w, "sgu_b_t": jnp.swapaxes(sgu_b, 1, 2),
        "sgu_w00": row(jnp.repeat(sgu_w[:, :, 0, 0], HD_C, axis=-1)),
        "sgu_b0": row(jnp.repeat(sgu_b[:, :, 0], HD_C, axis=-1)),
    }
    g_mix, g_ffn, g_final = row(norm_mix), row(norm_ffn), norm_final[None, :]
    w_in_l = w_in[0].astype(BF16)

    ca_t = jnp.swapaxes(state_conv_a, 1, 2)
    cb_t = jnp.swapaxes(state_conv_b, 1, 2)

    xp = x_prompt.reshape(nb * seq, D_MODEL)
    xs = x_sample.reshape(ns, D_MODEL)
    ca_p, h_p, cb_p, ca_s, h_s, cb_s, v_s = [], [], [], [], [], [], []
    for l in range(depth):
        ya, yc, ub, ca_l, h_l, zs = _front(xp, xs, g_mix, w_in_l, p, l, nb, seq, tl=256)
        ca_p.append(ca_l[:, A_TAIL - (K_A - 1):, :])
        h_p.append(h_l[:, 0, :])
        cb_p.append(ub.reshape(nb, seq, W_B)[:, seq - (K_B - 1):, :])
        mix_s, cas_l, hs_l, cbs_l, v_l = _mixer_sample(zs, ca_t, state_lru_h, cb_t, p, l, tb=32)
        ca_s.append(jnp.swapaxes(cas_l, 0, 1))
        h_s.append(hs_l)
        cb_s.append(jnp.swapaxes(cbs_l, 0, 1))
        v_s.append(v_l[:, None, :])
        xp, xs = _back(xp, ya, yc, ub, xs, mix_s, w_out, p, l, seq, tm=512)
        if l + 1 < depth:
            xp, xs, w_in_l = _ffn(xp, xs, g_ffn, w_ff1, w_ff2, l, tm=1024, tf=512, w_in=w_in)
        else:
            xp, xs = _ffn(xp, xs, g_ffn, w_ff1, w_ff2, l, tm=1024, tf=512, g_final=g_final)

    return (xp.reshape(nb, seq, D_MODEL), xs.reshape(ns, 1, D_MODEL),
            jnp.stack(ca_p), jnp.stack(h_p), jnp.stack(cb_p),
            jnp.stack(ca_s), jnp.stack(h_s), jnp.stack(cb_s), jnp.stack(v_s))
```
